```python
import jax, jax.numpy as jnp
from jax import lax
import numpy as np

D_MODEL = 4096
BATCH = 32
SEQ = 256
DEPTH = 1
DEC_BATCH = 4
DEC_SEQ = 1024
PAST_LEN = 512

GRID_W = 64
D_A = D_MODEL // 2
HEAD_A = 64
H_A = D_A // HEAD_A
R_DECAY = 128
R_A = 128
R_G = 480
LN_X_EPS = 64e-5
H_B = 16
Q_LORA = 1024
KV_LORA = 512
NOPE_DIM = 128
ROPE_DIM = 64
V_DIM = 128
QK_DIM = NOPE_DIM + ROPE_DIM
D_B = H_B * V_DIM
ROPE_THETA = 10000.0
Q_BLOCK = 128
D_FF = ((8 * D_MODEL // 3 + 255) // 256) * 256
NORM_EPS = 1e-6
IN_SIZES = (D_A, D_A, D_A, R_DECAY, R_DECAY, R_A, R_A, R_G, Q_LORA, KV_LORA, ROPE_DIM, D_MODEL, D_MODEL)
D_IN = sum(IN_SIZES)

kernel_name = 'hybrid_rwkv7_mla_prefix_dit_step'


def split_points():
    return [int(s) for s in np.cumsum(IN_SIZES)[:-1]]


def rms_norm(x, w):
    xf = x.astype(jnp.float32)
    y = xf * lax.rsqrt(jnp.mean(xf * xf, axis=-1, keepdims=True) + NORM_EPS)
    return (y * w.astype(jnp.float32)).astype(x.dtype)


def adaln_modulation(cond, w_mod, b_mod):
    m = jax.nn.silu(cond) @ w_mod + b_mod
    return m.reshape(cond.shape[0], 6, D_MODEL)


def axial_rope_tables(n_tokens):
    rows = n_tokens // GRID_W
    row = jnp.repeat(jnp.arange(rows), GRID_W).astype(jnp.float32)
    col = jnp.tile(jnp.arange(GRID_W), rows).astype(jnp.float32)
    n_freq = ROPE_DIM // 4
    inv = ROPE_THETA ** (-jnp.arange(n_freq, dtype=jnp.float32) / n_freq)
    ang = jnp.concatenate([row[:, None] * inv, col[:, None] * inv], axis=-1)
    return jnp.cos(ang), jnp.sin(ang)


def apply_axial_rope(x, cos, sin):
    x_nope, x_rot = x[..., :NOPE_DIM], x[..., NOPE_DIM:]
    x1, x2 = x_rot[..., :ROPE_DIM // 2], x_rot[..., ROPE_DIM // 2:]
    c, s = cos[None, :, None, :], sin[None, :, None, :]
    rot = jnp.concatenate([x1 * c - x2 * s, x1 * s + x2 * c], axis=-1).astype(x.dtype)
    return jnp.concatenate([x_nope, rot], axis=-1)


def wkv7_scan(r, decay, k, v, a, b, s0, reverse):
    def step(s, inp):
        r_t, w_t, k_t, v_t, a_t, b_t = inp
        sa = jnp.einsum('bhvk,bhk->bhv', s, a_t)
        s = s * w_t[:, :, None, :] + sa[..., None] * b_t[:, :, None, :] + v_t[..., None] * k_t[:, :, None, :]
        return s, jnp.einsum('bhvk,bhk->bhv', s, r_t)
    xs = tuple(jnp.swapaxes(t, 0, 1) for t in (r, decay, k, v, a, b))
    s_final, ys = lax.scan(step, s0.astype(jnp.float32), xs, reverse=reverse)
    return jnp.swapaxes(ys, 0, 1), s_final


def rwkv7_bidirectional(r, k, v, lw, la, lg, s0, lp):
    B, T, _ = r.shape
    f32 = jnp.float32

    def heads(t):
        return t.astype(f32).reshape(t.shape[:-1] + (H_A, HEAD_A))

    w = -jax.nn.softplus(-(lp['rwkv_w0'][:, None, None, :]
                           + jnp.einsum('dbtr,drc->dbtc', jnp.tanh(lw), lp['rwkv_w_lora_b']))) - 0.5
    decay = jnp.exp(-jnp.exp(w.astype(f32)))
    a = jax.nn.sigmoid(lp['rwkv_a0'][:, None, None, :] + jnp.einsum('dbtr,drc->dbtc', la, lp['rwkv_a_lora_b']))
    g = jax.nn.sigmoid(lg) @ lp['rwkv_g_lora_b']
    kk = heads(k * lp['rwkv_k_k'])
    kk = kk * lax.rsqrt(jnp.sum(kk * kk, axis=-1, keepdims=True) + 1e-12)
    k_dir = heads(k[None] * (1 + (a - 1) * lp['rwkv_k_a']))
    a_h, w_h = heads(a), heads(decay)
    r_h, v_h = heads(r), heads(v)
    y_f, s_f = wkv7_scan(r_h, w_h[0], k_dir[0], v_h, -kk, kk * a_h[0], s0[:, 0], False)
    y_b, s_b = wkv7_scan(r_h, w_h[1], k_dir[1], v_h, -kk, kk * a_h[1], s0[:, 1], True)
    y = y_f + y_b
    mu = jnp.mean(y, axis=-1, keepdims=True)
    var = jnp.mean(jnp.square(y - mu), axis=-1, keepdims=True)
    y = ((y - mu) * lax.rsqrt(var + LN_X_EPS)).reshape(B, T, D_A) * lp['rwkv_ln_w'] + lp['rwkv_ln_b']
    bonus = jnp.sum(r_h[None] * k_dir * lp['rwkv_r_k'], axis=-1, keepdims=True) * v_h[None]
    y = (y + bonus.sum(0).reshape(B, T, D_A)) * g
    return y.astype(r.dtype), jnp.stack([s_f, s_b], axis=1).astype(r.dtype)


def mla_query(q_lat, lp):
    B, T, _ = q_lat.shape
    q = rms_norm(q_lat, lp['mla_q_norm_w']) @ lp['mla_w_uq']
    return rms_norm(q.reshape(B, T, H_B, QK_DIM), lp['mla_q_head_norm'])


def mla_keys_values(c_kv, k_pe, lp):
    B, T, _ = c_kv.shape
    kv = (c_kv @ lp['mla_w_ukv']).reshape(B, T, H_B, NOPE_DIM + V_DIM)
    k_nope, v = kv[..., :NOPE_DIM], kv[..., NOPE_DIM:]
    k = jnp.concatenate([k_nope, jnp.broadcast_to(k_pe[:, :, None, :], (B, T, H_B, ROPE_DIM))], axis=-1)
    return rms_norm(k, lp['mla_k_head_norm']), v


def block_attention(q, k, v):
    B, Tq, H, Dh = q.shape
    n_blk = Tq // Q_BLOCK
    qb = q.reshape(B, n_blk, Q_BLOCK, H, Dh).transpose(1, 0, 2, 3, 4)
    scale = Dh ** -0.5

    def one_block(qi):
        s = jnp.einsum('bqhd,bkhd->bhqk', qi, k).astype(jnp.float32) * scale
        p = jax.nn.softmax(s, axis=-1).astype(v.dtype)
        return jnp.einsum('bhqk,bkhd->bqhd', p, v)

    out = lax.map(one_block, qb)
    return out.transpose(1, 0, 2, 3, 4).reshape(B, Tq, H, v.shape[-1])


def trunk_layer(x, mod, lp, s0, ctx_ckv, ctx_kpe, rope):
    B, T, _ = x.shape
    shift1, scale1, gate1, shift2, scale2, gate2 = (mod[:, i, None, :] for i in range(6))
    h = rms_norm(x, lp['norm_mix_w']) * (1 + scale1) + shift1
    (r, k, v, lw_f, lw_b, la_f, la_b, lg, q_lat, kv_lat, k_pe, gl_a, gl_b) = jnp.split(
        h @ lp['w_in'], split_points(), axis=-1)
    y_a, s_new = rwkv7_bidirectional(r, k, v, jnp.stack([lw_f, lw_b]), jnp.stack([la_f, la_b]), lg, s0, lp)
    c_kv = rms_norm(kv_lat, lp['mla_kv_norm_w'])
    q = mla_query(q_lat, lp)
    kh, vh = mla_keys_values(c_kv, k_pe, lp)
    if rope is not None:
        cos, sin = rope
        q = apply_axial_rope(q, cos, sin)
        kh = apply_axial_rope(kh, cos, sin)
        k_ctx, v_ctx = mla_keys_values(ctx_ckv, ctx_kpe, lp)
        kh = jnp.concatenate([k_ctx, kh], axis=1)
        vh = jnp.concatenate([v_ctx, vh], axis=1)
    y_b = block_attention(q, kh, vh).reshape(B, T, D_B)
    mixed = jax.nn.sigmoid(gl_a) * (y_a @ lp['w_o_rwkv']) + jax.nn.sigmoid(gl_b) * (y_b @ lp['w_o_mla'])
    x = x + gate1 * (mixed @ lp['w_out'])
    h2 = rms_norm(x, lp['norm_ffn_w']) * (1 + scale2) + shift2
    g_ffn, up = jnp.split(h2 @ lp['w_ffn_in'], 2, axis=-1)
    x = x + gate2 * ((jax.nn.silu(g_ffn) * up) @ lp['w_ffn_out'])
    return x, s_new, c_kv, k_pe


def setup_inputs(seed: int = 0) -> dict:
    key = jax.random.key(seed)
    ks = iter(jax.random.split(key, 40))

    def nrm(shape, s):
        return jax.random.normal(next(ks), shape, jnp.float32) * s

    L = DEPTH
    return {
        'x_prompt': nrm((BATCH, SEQ, D_MODEL), 1.0),
        'x_sample': nrm((DEC_BATCH, DEC_SEQ, D_MODEL), 1.0),
        'c': nrm((DEC_BATCH, D_MODEL), 1.0),
        'state_rwkv': nrm((DEC_BATCH, L, 2, H_A, HEAD_A, HEAD_A), 0.1),
        'cache_mla_ckv': nrm((DEC_BATCH, L, PAST_LEN, KV_LORA), 1.0),
        'cache_mla_kpe': nrm((DEC_BATCH, L, PAST_LEN, ROPE_DIM), 1.0),
        'c_ctx': nrm((D_MODEL,), 1.0),
        'w_mod': nrm((L, D_MODEL, 6 * D_MODEL), 0.5 * D_MODEL ** -0.5),
        'b_mod': nrm((L, 6 * D_MODEL), 0.02),
        'norm_mix_w': 1.0 + nrm((L, D_MODEL), 0.02),
        'w_in': nrm((L, D_MODEL, D_IN), D_MODEL ** -0.5),
        'rwkv_w0': -2.0 + nrm((L, 2, D_A), 0.5),
        'rwkv_w_lora_b': nrm((L, 2, R_DECAY, D_A), 0.1 * R_DECAY ** -0.5),
        'rwkv_a0': nrm((L, 2, D_A), 0.1),
        'rwkv_a_lora_b': nrm((L, 2, R_A, D_A), 0.1 * R_A ** -0.5),
        'rwkv_g_lora_b': nrm((L, R_G, D_A), R_G ** -0.5),
        'rwkv_k_k': 0.85 + nrm((L, D_A), 0.02),
        'rwkv_k_a': 1.0 + nrm((L, D_A), 0.02),
        'rwkv_r_k': nrm((L, H_A, HEAD_A), 0.1),
        'rwkv_ln_w': 1.0 + nrm((L, D_A), 0.02),
        'rwkv_ln_b': nrm((L, D_A), 0.02),
        'mla_q_norm_w': 1.0 + nrm((L, Q_LORA), 0.02),
        'mla_w_uq': nrm((L, Q_LORA, H_B * QK_DIM), Q_LORA ** -0.5),
        'mla_kv_norm_w': 1.0 + nrm((L, KV_LORA), 0.02),
        'mla_w_ukv': nrm((L, KV_LORA, H_B * (NOPE_DIM + V_DIM)), KV_LORA ** -0.5),
        'mla_q_head_norm': 1.0 + nrm((L, QK_DIM), 0.02),
        'mla_k_head_norm': 1.0 + nrm((L, QK_DIM), 0.02),
        'w_o_rwkv': nrm((L, D_A, D_MODEL), D_A ** -0.5),
        'w_o_mla': nrm((L, D_B, D_MODEL), D_B ** -0.5),
        'w_out': nrm((L, D_MODEL, D_MODEL), D_MODEL ** -0.5),
        'norm_ffn_w': 1.0 + nrm((L, D_MODEL), 0.02),
        'w_ffn_in': nrm((L, D_MODEL, 2 * D_FF), D_MODEL ** -0.5),
        'w_ffn_out': nrm((L, D_FF, D_MODEL), D_FF ** -0.5),
    }


def reference(x_prompt, x_sample, c, state_rwkv, cache_mla_ckv, cache_mla_kpe, c_ctx, w_mod, b_mod,
              norm_mix_w, w_in, rwkv_w0, rwkv_w_lora_b, rwkv_a0, rwkv_a_lora_b, rwkv_g_lora_b, rwkv_k_k,
              rwkv_k_a, rwkv_r_k, rwkv_ln_w, rwkv_ln_b, mla_q_norm_w, mla_w_uq, mla_kv_norm_w, mla_w_ukv,
              mla_q_head_norm, mla_k_head_norm, w_o_rwkv, w_o_mla, w_out, norm_ffn_w, w_ffn_in, w_ffn_out):
    rope = axial_rope_tables(x_sample.shape[1])
    zero_state = jnp.zeros((x_prompt.shape[0], 2, H_A, HEAD_A, HEAD_A), jnp.float32)
    y_p, y_s = x_prompt, x_sample
    new_s, new_ckv, new_kpe = [], [], []
    for l in range(DEPTH):
        lp = {
            'norm_mix_w': norm_mix_w[l], 'w_in': w_in[l],
            'rwkv_w0': rwkv_w0[l], 'rwkv_w_lora_b': rwkv_w_lora_b[l], 'rwkv_a0': rwkv_a0[l],
            'rwkv_a_lora_b': rwkv_a_lora_b[l], 'rwkv_g_lora_b': rwkv_g_lora_b[l], 'rwkv_k_k': rwkv_k_k[l],
            'rwkv_k_a': rwkv_k_a[l], 'rwkv_r_k': rwkv_r_k[l], 'rwkv_ln_w': rwkv_ln_w[l], 'rwkv_ln_b': rwkv_ln_b[l],
            'mla_q_norm_w': mla_q_norm_w[l], 'mla_w_uq': mla_w_uq[l], 'mla_kv_norm_w': mla_kv_norm_w[l],
            'mla_w_ukv': mla_w_ukv[l], 'mla_q_head_norm': mla_q_head_norm[l], 'mla_k_head_norm': mla_k_head_norm[l],
            'w_o_rwkv': w_o_rwkv[l], 'w_o_mla': w_o_mla[l], 'w_out': w_out[l],
            'norm_ffn_w': norm_ffn_w[l], 'w_ffn_in': w_ffn_in[l], 'w_ffn_out': w_ffn_out[l],
        }
        mod_ctx = adaln_modulation(c_ctx[None, :], w_mod[l], b_mod[l])
        mod_lat = adaln_modulation(c, w_mod[l], b_mod[l])
        y_p, s_l, ckv_l, kpe_l = trunk_layer(y_p, mod_ctx, lp, zero_state, None, None, None)
        new_s.append(s_l)
        new_ckv.append(ckv_l)
        new_kpe.append(kpe_l)
        y_s, _, _, _ = trunk_layer(y_s, mod_lat, lp, state_rwkv[:, l], cache_mla_ckv[:, l], cache_mla_kpe[:, l], rope)
    return (y_p, y_s, jnp.stack(new_s, axis=1), jnp.stack(new_ckv, axis=1), jnp.stack(new_kpe, axis=1))
```

```python
import functools

import numpy as np
import jax
import jax.numpy as jnp
from jax import lax
from jax.experimental import pallas as pl
from jax.experimental.pallas import tpu as pltpu

F32 = jnp.float32
BF = jnp.bfloat16

D_MODEL = 4096
BATCH, SEQ = 32, 256
DEC_BATCH, DEC_SEQ, PAST_LEN = 4, 1024, 512
GRID_W = 64
D_A, HEAD_A = 2048, 64
H_A = D_A // HEAD_A
R_LORA, R_G = 128, 480
LN_X_EPS = 64e-5
H_B, Q_LORA, KV_LORA = 16, 1024, 512
NOPE_DIM, ROPE_DIM, V_DIM = 128, 64, 128
QK_DIM = NOPE_DIM + ROPE_DIM
ROPE_THETA = 10000.0
D_FF = 11008
NORM_EPS = 1e-6

N_PROMPT = BATCH * SEQ
N_SAMPLE = DEC_BATCH * DEC_SEQ
N_TOK = N_PROMPT + N_SAMPLE

LANES = 128
R_G_PAD = 512
QK_PAD = 256
D_FF_PAD = 11264
C_R, C_K, C_V = 0, 2048, 4096
C_LORA, C_LG, C_QLAT, C_KVLAT, C_KPE = 6144, 6656, 7168, 8192, 8704
C_GLA, C_GLB, D_IN_PAD = 9216, 13312, 17408
WKV_CHUNK = 64
PAIR = 2 * HEAD_A
N_PAIR = D_A // PAIR

VMEM_LIMIT = 56 * 2**20


def _cparams(sem):
    return pltpu.CompilerParams(dimension_semantics=sem, vmem_limit_bytes=VMEM_LIMIT)


def _dot(a, b):
    return jnp.dot(a, b, preferred_element_type=F32)


def _dot_nt(a, b):
    return lax.dot_general(a, b, (((1,), (1,)), ((), ())), preferred_element_type=F32)


def _dot_tn(a, b):
    return lax.dot_general(a, b, (((0,), (0,)), ((), ())), preferred_element_type=F32)


def _mod_row(i, tm):
    n_prompt_tiles = N_PROMPT // tm
    tiles_per_seq = DEC_SEQ // tm
    return jnp.where(i < n_prompt_tiles, 0, 1 + (i - n_prompt_tiles) // tiles_per_seq)


def _mod_kernel(c_ref, w_ref, b_ref, o_ref):
    c = c_ref[...]
    s = c * jax.nn.sigmoid(c)
    o_ref[...] = _dot(s.astype(BF), w_ref[...].astype(BF)) + b_ref[...]


def _modulation(cond8, w_mod, b_mod):
    tn = 512
    n = w_mod.shape[1]
    return pl.pallas_call(
        _mod_kernel,
        out_shape=jax.ShapeDtypeStruct((8, n), F32),
        grid=(n // tn,),
        in_specs=[pl.BlockSpec((8, D_MODEL), lambda j: (0, 0)),
                  pl.BlockSpec((D_MODEL, tn), lambda j: (0, j)),
                  pl.BlockSpec((1, tn), lambda j: (0, j))],
        out_specs=pl.BlockSpec((8, tn), lambda j: (0, j)),
        compiler_params=_cparams(("arbitrary",)),
    )(cond8, w_mod, b_mod.reshape(1, n))


def _normmod(x_ref, nw_ref, mod_ref, shift_idx, scale_idx):
    x = x_ref[...]
    y = x * lax.rsqrt(jnp.mean(x * x, axis=-1, keepdims=True) + NORM_EPS)
    h = (y * nw_ref[...]) * (1.0 + mod_ref[scale_idx:scale_idx + 1, :]) + mod_ref[shift_idx:shift_idx + 1, :]
    return h.astype(BF)


def _in_proj_kernel(x_ref, nw_ref, mod_ref, w_ref, o_ref, h_ref):
    @pl.when(pl.program_id(1) == 0)
    def _():
        h_ref[...] = _normmod(x_ref, nw_ref, mod_ref, 0, 1)

    o_ref[...] = _dot(h_ref[...], w_ref[...])


def _in_proj(x, norm_w, mod, w_in_p):
    tm, tn = 512, 512
    return pl.pallas_call(
        _in_proj_kernel,
        out_shape=jax.ShapeDtypeStruct((N_TOK, D_IN_PAD), F32),
        grid=(N_TOK // tm, D_IN_PAD // tn),
        in_specs=[pl.BlockSpec((tm, D_MODEL), lambda i, j: (i, 0)),
                  pl.BlockSpec((1, D_MODEL), lambda i, j: (0, 0)),
                  pl.BlockSpec((None, 6, D_MODEL), lambda i, j: (_mod_row(i, tm), 0, 0)),
                  pl.BlockSpec((D_MODEL, tn), lambda i, j: (0, j))],
        out_specs=pl.BlockSpec((tm, tn), lambda i, j: (i, j)),
        scratch_shapes=[pltpu.VMEM((tm, D_MODEL), BF)],
        compiler_params=_cparams(("parallel", "arbitrary")),
    )(x, norm_w.reshape(1, D_MODEL), mod, w_in_p)


def _ffn_in_kernel(x_ref, nw_ref, mod_ref, wg_ref, wu_ref, o_ref, h_ref):
    @pl.when(pl.program_id(1) == 0)
    def _():
        h_ref[...] = _normmod(x_ref, nw_ref, mod_ref, 3, 4)

    h = h_ref[...]
    g = _dot(h, wg_ref[...])
    u = _dot(h, wu_ref[...])
    o_ref[...] = (g * jax.nn.sigmoid(g) * u).astype(BF)


def _ffn_in(x, norm_w, mod, w_ffn_in_p):
    tm, tn = 512, 512
    nj = D_FF_PAD // tn
    return pl.pallas_call(
        _ffn_in_kernel,
        out_shape=jax.ShapeDtypeStruct((N_TOK, D_FF_PAD), BF),
        grid=(N_TOK // tm, nj),
        in_specs=[pl.BlockSpec((tm, D_MODEL), lambda i, j: (i, 0)),
                  pl.BlockSpec((1, D_MODEL), lambda i, j: (0, 0)),
                  pl.BlockSpec((None, 6, D_MODEL), lambda i, j: (_mod_row(i, tm), 0, 0)),
                  pl.BlockSpec((D_MODEL, tn), lambda i, j: (0, j)),
                  pl.BlockSpec((D_MODEL, tn), lambda i, j: (0, j + nj))],
        out_specs=pl.BlockSpec((tm, tn), lambda i, j: (i, j)),
        scratch_shapes=[pltpu.VMEM((tm, D_MODEL), BF)],
        compiler_params=_cparams(("parallel", "arbitrary")),
    )(x, norm_w.reshape(1, D_MODEL), mod, w_ffn_in_p, w_ffn_in_p)


def _resid_proj_kernel(a_ref, w_ref, x_ref, mod_ref, o_ref, acc_ref, *, gate_idx):
    k = pl.program_id(2)

    @pl.when(k == 0)
    def _():
        acc_ref[...] = jnp.zeros_like(acc_ref)

    acc_ref[...] += _dot(a_ref[...], w_ref[...])

    @pl.when(k == pl.num_programs(2) - 1)
    def _():
        o_ref[...] = x_ref[...] + mod_ref[gate_idx:gate_idx + 1, :] * acc_ref[...]


def _resid_proj(a, w, x, mod, gate_idx, tk):
    tm, tn = 1024, 1024
    kdim = a.shape[1]
    return pl.pallas_call(
        functools.partial(_resid_proj_kernel, gate_idx=gate_idx),
        out_shape=jax.ShapeDtypeStruct((N_TOK, D_MODEL), F32),
        grid=(N_TOK // tm, D_MODEL // tn, kdim // tk),
        in_specs=[pl.BlockSpec((tm, tk), lambda i, j, k: (i, k)),
                  pl.BlockSpec((tk, tn), lambda i, j, k: (k, j)),
                  pl.BlockSpec((tm, tn), lambda i, j, k: (i, j)),
                  pl.BlockSpec((None, 6, tn), lambda i, j, k: (_mod_row(i, tm), 0, j))],
        out_specs=pl.BlockSpec((tm, tn), lambda i, j, k: (i, j)),
        scratch_shapes=[pltpu.VMEM((tm, tn), F32)],
        compiler_params=_cparams(("parallel", "parallel", "arbitrary")),
    )(a, w, x, mod)


def _merge_kernel(ya_ref, yb_ref, wa_ref, wb_ref, gla_ref, glb_ref, o_ref):
    pa = _dot(ya_ref[...], wa_ref[...])
    pb = _dot(yb_ref[...], wb_ref[...])
    o_ref[...] = (jax.nn.sigmoid(gla_ref[...]) * pa + jax.nn.sigmoid(glb_ref[...]) * pb).astype(BF)


def _merge(y_a, y_b, w_oa, w_ob, proj):
    tm, tn = 512, 1024
    return pl.pallas_call(
        _merge_kernel,
        out_shape=jax.ShapeDtypeStruct((N_TOK, D_MODEL), BF),
        grid=(N_TOK // tm, D_MODEL // tn),
        in_specs=[pl.BlockSpec((tm, D_A), lambda i, j: (i, 0)),
                  pl.BlockSpec((tm, D_A), lambda i, j: (i, 0)),
                  pl.BlockSpec((D_A, tn), lambda i, j: (0, j)),
                  pl.BlockSpec((D_A, tn), lambda i, j: (0, j)),
                  pl.BlockSpec((tm, tn), lambda i, j: (i, C_GLA // tn + j)),
                  pl.BlockSpec((tm, tn), lambda i, j: (i, C_GLB // tn + j))],
        out_specs=pl.BlockSpec((tm, tn), lambda i, j: (i, j)),
        compiler_params=_cparams(("parallel", "arbitrary")),
    )(y_a, y_b, w_oa, w_ob, proj, proj)


def _rope(x, cos_ref, sin_ref):
    swapped = pltpu.roll(x, ROPE_DIM // 2, axis=1) + pltpu.roll(x, LANES - ROPE_DIM // 2, axis=1)
    return x * cos_ref[...] + swapped * sin_ref[...]


def _rope_block(i, tm):
    n_prompt_tiles = N_PROMPT // tm
    tiles_per_seq = DEC_SEQ // tm
    return jnp.where(i < n_prompt_tiles, tiles_per_seq, (i - n_prompt_tiles) % tiles_per_seq)


def _q_kernel(ql_ref, nw_ref, w_ref, hw_ref, cos_ref, sin_ref, o_ref):
    x = ql_ref[...]
    y = x * lax.rsqrt(jnp.mean(x * x, axis=-1, keepdims=True) + NORM_EPS) * nw_ref[...]
    q = _dot(y.astype(BF), w_ref[...])
    hw = hw_ref[...]
    for h in range(H_B):
        qh = q[:, h * QK_PAD:(h + 1) * QK_PAD]
        inv = lax.rsqrt(jnp.sum(qh * qh, axis=-1, keepdims=True) * (1.0 / QK_DIM) + NORM_EPS)
        qn = qh * inv * hw
        o_ref[:, h * QK_PAD:h * QK_PAD + NOPE_DIM] = qn[:, :NOPE_DIM].astype(BF)
        o_ref[:, h * QK_PAD + NOPE_DIM:(h + 1) * QK_PAD] = _rope(qn[:, NOPE_DIM:], cos_ref, sin_ref).astype(BF)


def _q_proj(proj, q_norm_w, w_uq_p, q_head_w_p, cos_t, sin_t):
    tm = 256
    return pl.pallas_call(
        _q_kernel,
        out_shape=jax.ShapeDtypeStruct((N_TOK, H_B * QK_PAD), BF),
        grid=(N_TOK // tm,),
        in_specs=[pl.BlockSpec((tm, Q_LORA), lambda i: (i, C_QLAT // Q_LORA)),
                  pl.BlockSpec((1, Q_LORA), lambda i: (0, 0)),
                  pl.BlockSpec((Q_LORA, H_B * QK_PAD), lambda i: (0, 0)),
                  pl.BlockSpec((1, QK_PAD), lambda i: (0, 0)),
                  pl.BlockSpec((tm, LANES), lambda i: (_rope_block(i, tm), 0)),
                  pl.BlockSpec((tm, LANES), lambda i: (_rope_block(i, tm), 0))],
        out_specs=pl.BlockSpec((tm, H_B * QK_PAD), lambda i: (i, 0)),
        compiler_params=_cparams(("parallel",)),
    )(proj, q_norm_w.reshape(1, Q_LORA), w_uq_p, q_head_w_p, cos_t, sin_t)


def _kv_kernel(lat_ref, kpe_ref, nw_ref, w_ref, wkn_ref, wkr_ref, cos_ref, sin_ref, *out_refs, pre_norm):
    if pre_norm:
        k_ref, v_ref, ckv_ref = out_refs
        x = lat_ref[...]
        ckv = x * lax.rsqrt(jnp.mean(x * x, axis=-1, keepdims=True) + NORM_EPS) * nw_ref[...]
        ckv_ref[...] = ckv
    else:
        k_ref, v_ref = out_refs
        ckv = lat_ref[...]
    kv = _dot(ckv.astype(BF), w_ref[...])
    v_ref[...] = kv[:, H_B * NOPE_DIM:].astype(BF)
    kpe = kpe_ref[...]
    kpe_ss = jnp.sum(kpe * kpe, axis=-1, keepdims=True)
    wkn, wkr = wkn_ref[...], wkr_ref[...]
    for h in range(H_B):
        kn = kv[:, h * NOPE_DIM:(h + 1) * NOPE_DIM]
        inv = lax.rsqrt((jnp.sum(kn * kn, axis=-1, keepdims=True) + kpe_ss) * (1.0 / QK_DIM) + NORM_EPS)
        k_ref[:, h * QK_PAD:h * QK_PAD + NOPE_DIM] = (kn * inv * wkn).astype(BF)
        k_ref[:, h * QK_PAD + NOPE_DIM:(h + 1) * QK_PAD] = _rope(kpe * inv * wkr, cos_ref, sin_ref).astype(BF)


def _kv_proj(lat, lat_col, kpe, kpe_col, n_rows, kv_norm_w, w_ukv_p, wkn, wkr, cos_t, sin_t, rope_block, pre_norm):
    tm = 256
    out_shape = [jax.ShapeDtypeStruct((n_rows, H_B * QK_PAD), BF),
                 jax.ShapeDtypeStruct((n_rows, H_B * V_DIM), BF)]
    out_specs = [pl.BlockSpec((tm, H_B * QK_PAD), lambda i: (i, 0)),
                 pl.BlockSpec((tm, H_B * V_DIM), lambda i: (i, 0))]
    if pre_norm:
        out_shape.append(jax.ShapeDtypeStruct((n_rows, KV_LORA), F32))
        out_specs.append(pl.BlockSpec((tm, KV_LORA), lambda i: (i, 0)))
    return pl.pallas_call(
        functools.partial(_kv_kernel, pre_norm=pre_norm),
        out_shape=out_shape,
        grid=(n_rows // tm,),
        in_specs=[pl.BlockSpec((tm, KV_LORA), lambda i: (i, lat_col // KV_LORA)),
                  pl.BlockSpec((tm, LANES), lambda i: (i, kpe_col // LANES)),
                  pl.BlockSpec((1, KV_LORA), lambda i: (0, 0)),
                  pl.BlockSpec((KV_LORA, H_B * (NOPE_DIM + V_DIM)), lambda i: (0, 0)),
                  pl.BlockSpec((1, NOPE_DIM), lambda i: (0, 0)),
                  pl.BlockSpec((1, LANES), lambda i: (0, 0)),
                  pl.BlockSpec((tm, LANES), lambda i: (rope_block(i, tm), 0)),
                  pl.BlockSpec((tm, LANES), lambda i: (rope_block(i, tm), 0))],
        out_specs=out_specs,
        compiler_params=_cparams(("parallel",)),
    )(lat, kpe, kv_norm_w.reshape(1, KV_LORA), w_ukv_p, wkn, wkr, cos_t, sin_t)


_ATTN_SCALE = QK_DIM ** -0.5


def _attn_prompt_kernel(q_ref, k_ref, v_ref, o_ref, *, heads):
    for h in range(heads):
        q = q_ref[:, h * QK_PAD:(h + 1) * QK_PAD]
        k = k_ref[:, h * QK_PAD:(h + 1) * QK_PAD]
        s = _dot_nt(q, k) * _ATTN_SCALE
        e = jnp.exp(s - jnp.max(s, axis=-1, keepdims=True))
        p = e * (1.0 / jnp.sum(e, axis=-1, keepdims=True))
        o_ref[:, h * V_DIM:(h + 1) * V_DIM] = _dot(p.astype(BF), v_ref[:, h * V_DIM:(h + 1) * V_DIM]).astype(BF)


def _attn_prompt(q, k, v):
    heads = 4
    return pl.pallas_call(
        functools.partial(_attn_prompt_kernel, heads=heads),
        out_shape=jax.ShapeDtypeStruct((N_PROMPT, H_B * V_DIM), BF),
        grid=(BATCH, H_B // heads),
        in_specs=[pl.BlockSpec((SEQ, heads * QK_PAD), lambda b, g: (b, g)),
                  pl.BlockSpec((SEQ, heads * QK_PAD), lambda b, g: (b, g)),
                  pl.BlockSpec((SEQ, heads * V_DIM), lambda b, g: (b, g))],
        out_specs=pl.BlockSpec((SEQ, heads * V_DIM), lambda b, g: (b, g)),
        compiler_params=_cparams(("parallel", "parallel")),
    )(q, k, v)


def _attn_sample_kernel(q_ref, kc_ref, vc_ref, ks_ref, vs_ref, o_ref):
    q = q_ref[...]
    s1 = _dot_nt(q, kc_ref[...]) * _ATTN_SCALE
    s2 = _dot_nt(q, ks_ref[...]) * _ATTN_SCALE
    m = jnp.maximum(jnp.max(s1, axis=-1, keepdims=True), jnp.max(s2, axis=-1, keepdims=True))
    e1 = jnp.exp(s1 - m)
    e2 = jnp.exp(s2 - m)
    inv = 1.0 / (jnp.sum(e1, axis=-1, keepdims=True) + jnp.sum(e2, axis=-1, keepdims=True))
    o = _dot((e1 * inv).astype(BF), vc_ref[...]) + _dot((e2 * inv).astype(BF), vs_ref[...])
    o_ref[...] = o.astype(BF)


def _attn_sample(q, k, v, k_ctx, v_ctx):
    tq = 512
    qt = DEC_SEQ // tq
    q0 = N_PROMPT // tq
    s0 = N_PROMPT // DEC_SEQ
    return pl.pallas_call(
        _attn_sample_kernel,
        out_shape=jax.ShapeDtypeStruct((N_SAMPLE, H_B * V_DIM), BF),
        grid=(DEC_BATCH, H_B, qt),
        in_specs=[pl.BlockSpec((tq, QK_PAD), lambda b, h, t: (q0 + b * qt + t, h)),
                  pl.BlockSpec((PAST_LEN, QK_PAD), lambda b, h, t: (b, h)),
                  pl.BlockSpec((PAST_LEN, V_DIM), lambda b, h, t: (b, h)),
                  pl.BlockSpec((DEC_SEQ, QK_PAD), lambda b, h, t: (s0 + b, h)),
                  pl.BlockSpec((DEC_SEQ, V_DIM), lambda b, h, t: (s0 + b, h))],
        out_specs=pl.BlockSpec((tq, V_DIM), lambda b, h, t: (b * qt + t, h)),
        compiler_params=_cparams(("parallel", "parallel", "arbitrary")),
    )(q, k_ctx, v_ctx, k, v)


def _split3(x):
    hi = x.astype(BF)
    r1 = x - hi.astype(F32)
    mid = r1.astype(BF)
    lo = (r1 - mid.astype(F32)).astype(BF)
    return hi, mid, lo


def _head_sum(x, ones_blk):
    hi, mid, lo = _split3(x)
    return _dot(hi, ones_blk) + _dot(mid, ones_blk) + _dot(lo, ones_blk)


def _wkv_chunk(d, r, v, kk, lw, b, kd, s, consts):
    C = WKV_CHUNK
    tri, strict, incl, lane_lo = consts
    lw_hi = lw.astype(BF)
    lw_lo = (lw - lw_hi.astype(F32)).astype(BF)
    c = _dot(tri, lw_hi) + _dot(tri, lw_lo)
    cl = c[C - 1:C, :] if d == 0 else c[0:1, :]
    e_inc = jnp.exp(c)
    e_neg = jnp.exp(-c)
    e_exc = jnp.exp(c - lw)
    e_end = jnp.exp(cl - c)

    def stack(x):
        return jnp.concatenate([jnp.where(lane_lo, x, 0.0), jnp.where(lane_lo, 0.0, x)], axis=0)

    a_t = stack(-kk * e_exc)
    r_t = stack(r * e_inc)
    b_t = stack(b * e_neg).astype(BF)
    k_t = stack(kd * e_neg).astype(BF)
    b_h = stack(b * e_end).astype(BF)
    k_h = stack(kd * e_end).astype(BF)
    v_s = stack(v).astype(BF)

    sc = _dot_nt(jnp.concatenate([a_t, r_t], axis=0).astype(BF), jnp.concatenate([b_t, k_t], axis=0))
    n2 = 2 * C
    l_ab = jnp.where(strict, sc[:n2, :n2], 0.0)
    l_ak = jnp.where(strict, sc[:n2, n2:], 0.0)
    a_rb = jnp.where(incl, sc[n2:, :n2], 0.0).astype(BF)
    a_rk = jnp.where(incl, sc[n2:, n2:], 0.0).astype(BF)

    x = jnp.concatenate([a_t, _dot(l_ak.astype(BF), v_s)], axis=1)
    lp = l_ab.astype(BF)
    span = 1
    while span < C:
        x = x + _dot(lp, x.astype(BF))
        span *= 2
        if span < C:
            lp = _dot(lp, lp).astype(BF)
    w = x[:, :PAIR].astype(BF)
    u = x[:, PAIR:].astype(BF)
    uv = jnp.concatenate([u, v_s], axis=0)

    q = r_t + _dot(a_rb, w)
    y_s = _dot_nt(q.astype(BF), s.astype(BF)) + _dot(jnp.concatenate([a_rb, a_rk], axis=1), uv)
    y = y_s[:C] + y_s[C:]

    m = _dot_tn(w, b_h)
    n = _dot_tn(uv, jnp.concatenate([b_h, k_h], axis=0))
    s_new = s * jnp.exp(cl) + _dot(s.astype(BF), m.astype(BF)) + n
    return y, s_new


def _wkv_kernel(*refs, T, has_s0, out_state):
    (r_ref, k_ref, v_ref, lora_ref, lg_ref, w0_ref, wl_ref, a0_ref, wa_ref, wg_ref,
     kkw_ref, ka_ref, rk_ref, lnw_ref, lnb_ref) = refs[:15]
    pos = 15
    if has_s0:
        s0_ref = refs[pos]
        pos += 1
    y_ref = refs[pos]
    pos += 1
    if out_state:
        so_ref = refs[pos]
        pos += 1
    lw_s, b_s, kd_s, kk_s, yf_s, yb_s = refs[pos:]

    C = WKV_CHUNK
    nc = T // C
    lane = lax.broadcasted_iota(jnp.int32, (PAIR, PAIR), 1)
    row = lax.broadcasted_iota(jnp.int32, (PAIR, PAIR), 0)
    ones_blk = jnp.where((lane // HEAD_A) == (row // HEAD_A), 1.0, 0.0).astype(BF)
    lane_lo = lax.broadcasted_iota(jnp.int32, (C, PAIR), 1) < HEAD_A
    ci = lax.broadcasted_iota(jnp.int32, (C, C), 0)
    cj = lax.broadcasted_iota(jnp.int32, (C, C), 1)

    k = k_ref[...]
    kk = k * kkw_ref[...]
    kk = kk * lax.rsqrt(_head_sum(kk * kk, ones_blk) + 1e-12)
    kk_s[...] = kk
    for d in range(2):
        lw_in = lora_ref[:, d * R_LORA:(d + 1) * R_LORA]
        la_in = lora_ref[:, (2 + d) * R_LORA:(3 + d) * R_LORA]
        z = -(w0_ref[d:d + 1, :] + _dot(jnp.tanh(lw_in).astype(BF), wl_ref[d].astype(BF)))
        softplus = jnp.maximum(z, 0.0) + jnp.log1p(jnp.exp(-jnp.abs(z)))
        lw_s[d] = -jnp.exp(-softplus - 0.5)
        a = jax.nn.sigmoid(a0_ref[d:d + 1, :] + _dot(la_in.astype(BF), wa_ref[d].astype(BF)))
        kd_s[d] = k * (1.0 + (a - 1.0) * ka_ref[...])
        b_s[d] = kk * a

    def consts(d):
        if d == 0:
            return (jnp.where(cj <= ci, 1.0, 0.0).astype(BF), lane < row, lane <= row, lane_lo)
        return (jnp.where(cj >= ci, 1.0, 0.0).astype(BF), lane > row, lane >= row, lane_lo)

    cf, cb = consts(0), consts(1)

    def body(i, carry):
        s_f, s_b = carry
        of = pl.multiple_of(i * C, C)
        ob = pl.multiple_of((nc - 1 - i) * C, C)
        y_f, s_f = _wkv_chunk(0, r_ref[pl.ds(of, C), :], v_ref[pl.ds(of, C), :], kk_s[pl.ds(of, C), :],
                              lw_s[0, pl.ds(of, C), :], b_s[0, pl.ds(of, C), :], kd_s[0, pl.ds(of, C), :], s_f, cf)
        yf_s[pl.ds(of, C), :] = y_f
        y_b, s_b = _wkv_chunk(1, r_ref[pl.ds(ob, C), :], v_ref[pl.ds(ob, C), :], kk_s[pl.ds(ob, C), :],
                              lw_s[1, pl.ds(ob, C), :], b_s[1, pl.ds(ob, C), :], kd_s[1, pl.ds(ob, C), :], s_b, cb)
        yb_s[pl.ds(ob, C), :] = y_b
        return s_f, s_b

    if has_s0:
        init = (s0_ref[0], s0_ref[1])
    else:
        init = (jnp.zeros((PAIR, PAIR), F32), jnp.zeros((PAIR, PAIR), F32))
    s_f, s_b = lax.fori_loop(0, nc, body, init)
    if out_state:
        so_ref[0] = s_f
        so_ref[1] = s_b

    y = yf_s[...] + yb_s[...]
    inv_n = 1.0 / HEAD_A
    mu = _head_sum(y, ones_blk) * inv_n
    yc = y - mu
    var = _head_sum(yc * yc, ones_blk) * inv_n
    y = yc * lax.rsqrt(var + LN_X_EPS) * lnw_ref[...] + lnb_ref[...]
    r = r_ref[...]
    rr = r * rk_ref[...]
    bonus = (_head_sum(rr * kd_s[0], ones_blk) + _head_sum(rr * kd_s[1], ones_blk)) * v_ref[...]
    g = _dot(jax.nn.sigmoid(lg_ref[...]).astype(BF), wg_ref[...].astype(BF))
    y_ref[...] = ((y + bonus) * g).astype(BF)


def _wkv(proj, row0, n_seq, T, wts, s0):
    (w0, wl, a0, wa, wg, kkw, ka, rk, lnw, lnb) = wts
    sb = row0 // T
    cb = lambda col: col // PAIR
    vec = pl.BlockSpec((1, PAIR), lambda s, p: (0, p))
    in_specs = [pl.BlockSpec((T, PAIR), lambda s, p: (sb + s, cb(C_R) + p)),
                pl.BlockSpec((T, PAIR), lambda s, p: (sb + s, cb(C_K) + p)),
                pl.BlockSpec((T, PAIR), lambda s, p: (sb + s, cb(C_V) + p)),
                pl.BlockSpec((T, 4 * R_LORA), lambda s, p: (sb + s, C_LORA // (4 * R_LORA))),
                pl.BlockSpec((T, R_G_PAD), lambda s, p: (sb + s, C_LG // R_G_PAD)),
                pl.BlockSpec((2, PAIR), lambda s, p: (0, p)),
                pl.BlockSpec((2, R_LORA, PAIR), lambda s, p: (0, 0, p)),
                pl.BlockSpec((2, PAIR), lambda s, p: (0, p)),
                pl.BlockSpec((2, R_LORA, PAIR), lambda s, p: (0, 0, p)),
                pl.BlockSpec((R_G_PAD, PAIR), lambda s, p: (0, p)),
                vec, vec, vec, vec, vec]
    args = [proj, proj, proj, proj, proj, w0, wl, a0, wa, wg, kkw, ka, rk, lnw, lnb]
    has_s0 = s0 is not None
    if has_s0:
        in_specs.append(pl.BlockSpec((None, 2, None, PAIR, PAIR), lambda s, p: (s, 0, p, 0, 0)))
        args.append(s0)
    out_shape = [jax.ShapeDtypeStruct((n_seq * T, D_A), BF)]
    out_specs = [pl.BlockSpec((T, PAIR), lambda s, p: (s, p))]
    out_state = not has_s0
    if out_state:
        out_shape.append(jax.ShapeDtypeStruct((n_seq, 2, N_PAIR, PAIR, PAIR), F32))
        out_specs.append(pl.BlockSpec((None, 2, None, PAIR, PAIR), lambda s, p: (s, 0, p, 0, 0)))
    return pl.pallas_call(
        functools.partial(_wkv_kernel, T=T, has_s0=has_s0, out_state=out_state),
        out_shape=out_shape,
        grid=(n_seq, N_PAIR),
        in_specs=in_specs,
        out_specs=out_specs,
        scratch_shapes=[pltpu.VMEM((2, T, PAIR), F32), pltpu.VMEM((2, T, PAIR), F32),
                        pltpu.VMEM((2, T, PAIR), F32), pltpu.VMEM((T, PAIR), F32),
                        pltpu.VMEM((T, PAIR), F32), pltpu.VMEM((T, PAIR), F32)],
        compiler_params=_cparams(("parallel", "arbitrary")),
    )(*args)


def _rope_tables():
    rows = DEC_SEQ // GRID_W
    row = jnp.repeat(jnp.arange(rows), GRID_W).astype(F32)
    col = jnp.tile(jnp.arange(GRID_W), rows).astype(F32)
    n_freq = ROPE_DIM // 4
    inv = ROPE_THETA ** (-jnp.arange(n_freq, dtype=F32) / n_freq)
    ang = jnp.concatenate([row[:, None] * inv, col[:, None] * inv], axis=-1)
    cos, sin = jnp.cos(ang), jnp.sin(ang)
    pad = jnp.zeros((DEC_SEQ, LANES - ROPE_DIM), F32)
    cos_t = jnp.concatenate([cos, cos, pad], axis=-1)
    sin_t = jnp.concatenate([-sin, sin, pad], axis=-1)
    ident = 256
    cos_t = jnp.concatenate([cos_t, jnp.ones((ident, LANES), F32)], axis=0)
    sin_t = jnp.concatenate([sin_t, jnp.zeros((ident, LANES), F32)], axis=0)
    return cos_t, sin_t


def _pair_state_in(state):
    b = state.shape[0]
    s = state.reshape(b, 2, N_PAIR, 2, HEAD_A, HEAD_A)
    z = jnp.zeros_like(s[:, :, :, 0])
    top = jnp.concatenate([s[:, :, :, 0], z], axis=-1)
    bot = jnp.concatenate([z, s[:, :, :, 1]], axis=-1)
    return jnp.concatenate([top, bot], axis=-2)


def _pair_state_out(sp):
    b = sp.shape[0]
    h0 = sp[:, :, :, :HEAD_A, :HEAD_A]
    h1 = sp[:, :, :, HEAD_A:, HEAD_A:]
    return jnp.stack([h0, h1], axis=3).reshape(b, 2, H_A, HEAD_A, HEAD_A)


def kernel(x_prompt, x_sample, c, state_rwkv, cache_mla_ckv, cache_mla_kpe, c_ctx, w_mod, b_mod, norm_mix_w, w_in, rwkv_w0, rwkv_w_lora_b, rwkv_a0, rwkv_a_lora_b, rwkv_g_lora_b, rwkv_k_k, rwkv_k_a, rwkv_r_k, rwkv_ln_w, rwkv_ln_b, mla_q_norm_w, mla_w_uq, mla_kv_norm_w, mla_w_ukv, mla_q_head_norm, mla_k_head_norm, w_o_rwkv, w_o_mla, w_out, norm_ffn_w, w_ffn_in, w_ffn_out):
    l = 0
    x = jnp.concatenate([x_prompt.reshape(N_PROMPT, D_MODEL), x_sample.reshape(N_SAMPLE, D_MODEL)], axis=0)

    wi = w_in[l]
    zc = lambda n: jnp.zeros((D_MODEL, n), F32)
    w_in_p = jnp.concatenate([wi[:, :7136], zc(32), wi[:, 7136:8736], zc(C_GLA - C_KPE - ROPE_DIM), wi[:, 8736:]],
                             axis=1).astype(BF)
    w_uq_p = jnp.pad(mla_w_uq[l].reshape(Q_LORA, H_B, QK_DIM), ((0, 0), (0, 0), (0, QK_PAD - QK_DIM)))
    w_uq_p = w_uq_p.reshape(Q_LORA, H_B * QK_PAD).astype(BF)
    q_head_w_p = jnp.pad(mla_q_head_norm[l], (0, QK_PAD - QK_DIM)).reshape(1, QK_PAD)
    wkv3 = mla_w_ukv[l].reshape(KV_LORA, H_B, NOPE_DIM + V_DIM)
    w_ukv_p = jnp.concatenate([wkv3[:, :, :NOPE_DIM].reshape(KV_LORA, H_B * NOPE_DIM),
                               wkv3[:, :, NOPE_DIM:].reshape(KV_LORA, H_B * V_DIM)], axis=1).astype(BF)
    wkn = mla_k_head_norm[l, :NOPE_DIM].reshape(1, NOPE_DIM)
    wkr = jnp.pad(mla_k_head_norm[l, NOPE_DIM:], (0, LANES - ROPE_DIM)).reshape(1, LANES)
    wf = w_ffn_in[l]
    zf = jnp.zeros((D_MODEL, D_FF_PAD - D_FF), F32)
    w_ffn_in_p = jnp.concatenate([wf[:, :D_FF], zf, wf[:, D_FF:], zf], axis=1).astype(BF)
    w_ffn_out_p = jnp.pad(w_ffn_out[l], ((0, D_FF_PAD - D_FF), (0, 0))).astype(BF)
    w_oa = w_o_rwkv[l].astype(BF)
    w_ob = w_o_mla[l].astype(BF)
    w_out_b = w_out[l].astype(BF)
    wg_p = jnp.pad(rwkv_g_lora_b[l], ((0, R_G_PAD - R_G), (0, 0)))
    row = lambda a: a.reshape(1, D_A)
    wkv_w = (rwkv_w0[l], rwkv_w_lora_b[l], rwkv_a0[l], rwkv_a_lora_b[l], wg_p, row(rwkv_k_k[l]), row(rwkv_k_a[l]),
             row(rwkv_r_k[l]), row(rwkv_ln_w[l]), row(rwkv_ln_b[l]))
    cos_t, sin_t = _rope_tables()

    cond8 = jnp.concatenate([c_ctx[None, :], c, jnp.zeros((8 - 1 - DEC_BATCH, D_MODEL), F32)], axis=0)
    mod = _modulation(cond8, w_mod[l], b_mod[l]).reshape(8, 6, D_MODEL)

    proj = _in_proj(x, norm_mix_w[l], mod, w_in_p)

    ya_p, s_new = _wkv(proj, 0, BATCH, SEQ, wkv_w, None)
    ya_s = _wkv(proj, N_PROMPT, DEC_BATCH, DEC_SEQ, wkv_w, _pair_state_in(state_rwkv[:, l]))[0]
    y_a = jnp.concatenate([ya_p, ya_s], axis=0)

    q = _q_proj(proj, mla_q_norm_w[l], w_uq_p, q_head_w_p, cos_t, sin_t)
    k, v, ckv = _kv_proj(proj, C_KVLAT, proj, C_KPE, N_TOK, mla_kv_norm_w[l], w_ukv_p, wkn, wkr, cos_t, sin_t,
                         _rope_block, True)
    ctx_ckv = cache_mla_ckv[:, l].reshape(DEC_BATCH * PAST_LEN, KV_LORA)
    ctx_kpe = jnp.pad(cache_mla_kpe[:, l].reshape(DEC_BATCH * PAST_LEN, ROPE_DIM), ((0, 0), (0, LANES - ROPE_DIM)))
    k_ctx, v_ctx = _kv_proj(ctx_ckv, 0, ctx_kpe, 0, DEC_BATCH * PAST_LEN, mla_kv_norm_w[l], w_ukv_p, wkn, wkr,
                            cos_t, sin_t, lambda i, tm: DEC_SEQ // tm, False)
    yb_p = _attn_prompt(q, k, v)
    yb_s = _attn_sample(q, k, v, k_ctx, v_ctx)
    y_b = jnp.concatenate([yb_p, yb_s], axis=0)

    mixed = _merge(y_a, y_b, w_oa, w_ob, proj)
    x1 = _resid_proj(mixed, w_out_b, x, mod, 2, 2048)
    act = _ffn_in(x1, norm_ffn_w[l], mod, w_ffn_in_p)
    x2 = _resid_proj(act, w_ffn_out_p, x1, mod, 5, 1408)

    y_p = x2[:N_PROMPT].reshape(BATCH, SEQ, D_MODEL)
    y_s = x2[N_PROMPT:].reshape(DEC_BATCH, DEC_SEQ, D_MODEL)
    new_state = _pair_state_out(s_new)[:, None]
    new_ckv = ckv[:N_PROMPT].reshape(BATCH, 1, SEQ, KV_LORA)
    new_kpe = proj[:N_PROMPT, C_KPE:C_KPE + ROPE_DIM].reshape(BATCH, 1, SEQ, ROPE_DIM)
    return y_p, y_s, new_state, new_ckv, new_kpe
```

```python
import functools

import numpy as np
import jax
import jax.numpy as jnp
from jax import lax
from jax.experimental import pallas as pl
from jax.experimental.pallas import tpu as pltpu

F32 = jnp.float32
BF = jnp.bfloat16

D_MODEL = 4096
BATCH, SEQ = 32, 256
DEC_BATCH, DEC_SEQ, PAST_LEN = 4, 1024, 512
GRID_W = 64
D_A, HEAD_A = 2048, 64
H_A = D_A // HEAD_A
R_LORA, R_G = 128, 480
LN_X_EPS = 64e-5
H_B, Q_LORA, KV_LORA = 16, 1024, 512
NOPE_DIM, ROPE_DIM, V_DIM = 128, 64, 128
QK_DIM = NOPE_DIM + ROPE_DIM
ROPE_THETA = 10000.0
D_FF = 11008
NORM_EPS = 1e-6

N_PROMPT = BATCH * SEQ
N_SAMPLE = DEC_BATCH * DEC_SEQ
N_TOK = N_PROMPT + N_SAMPLE

LANES = 128
R_G_PAD = 512
QK_PAD = 256
D_FF_PAD = 11264
C_R, C_K, C_V = 0, 2048, 4096
C_LORA, C_LG, C_QLAT, C_KVLAT, C_KPE = 6144, 6656, 7168, 8192, 8704
C_GLA, C_GLB, D_IN_PAD = 9216, 13312, 17408
WKV_CHUNK = 64
WKV_UNROLL = 8
PAIR = 2 * HEAD_A
N_PAIR = D_A // PAIR

VMEM_LIMIT = 56 * 2**20


def _cparams(sem):
    return pltpu.CompilerParams(dimension_semantics=sem, vmem_limit_bytes=VMEM_LIMIT)


def _dot(a, b):
    return jnp.dot(a, b, preferred_element_type=F32)


def _dot_nt(a, b):
    return lax.dot_general(a, b, (((1,), (1,)), ((), ())), preferred_element_type=F32)


def _dot_tn(a, b):
    return lax.dot_general(a, b, (((0,), (0,)), ((), ())), preferred_element_type=F32)


def _mod_row(i, tm):
    n_prompt_tiles = N_PROMPT // tm
    tiles_per_seq = DEC_SEQ // tm
    return jnp.where(i < n_prompt_tiles, 0, 1 + (i - n_prompt_tiles) // tiles_per_seq)


def _mod_kernel(c_ref, w_ref, b_ref, o_ref):
    c = c_ref[...]
    s = c * jax.nn.sigmoid(c)
    o_ref[...] = _dot(s.astype(BF), w_ref[...].astype(BF)) + b_ref[...]


def _modulation(cond8, w_mod, b_mod):
    tn = 512
    n = w_mod.shape[1]
    return pl.pallas_call(
        _mod_kernel,
        out_shape=jax.ShapeDtypeStruct((8, n), F32),
        grid=(n // tn,),
        in_specs=[pl.BlockSpec((8, D_MODEL), lambda j: (0, 0)),
                  pl.BlockSpec((D_MODEL, tn), lambda j: (0, j)),
                  pl.BlockSpec((1, tn), lambda j: (0, j))],
        out_specs=pl.BlockSpec((8, tn), lambda j: (0, j)),
        compiler_params=_cparams(("arbitrary",)),
    )(cond8, w_mod, b_mod.reshape(1, n))


def _normmod(x_ref, nw_ref, mod_ref, shift_idx, scale_idx):
    x = x_ref[...]
    y = x * lax.rsqrt(jnp.mean(x * x, axis=-1, keepdims=True) + NORM_EPS)
    h = (y * nw_ref[...]) * (1.0 + mod_ref[scale_idx:scale_idx + 1, :]) + mod_ref[shift_idx:shift_idx + 1, :]
    return h.astype(BF)


def _split_specs(tm, width, col):
    npt = N_PROMPT // tm
    return [pl.BlockSpec((tm, width), lambda i, *_: (jnp.minimum(i, npt - 1), col(*_))),
            pl.BlockSpec((tm, width), lambda i, *_: (jnp.maximum(i - npt, 0), col(*_)))]


def _in_proj_kernel(xp_ref, xs_ref, nw_ref, mod_ref, w_ref, o_ref, h_ref, *, n_prompt_tiles):
    first = pl.program_id(1) == 0
    is_prompt = pl.program_id(0) < n_prompt_tiles

    @pl.when(first & is_prompt)
    def _():
        h_ref[...] = _normmod(xp_ref, nw_ref, mod_ref, 0, 1)

    @pl.when(first & jnp.logical_not(is_prompt))
    def _():
        h_ref[...] = _normmod(xs_ref, nw_ref, mod_ref, 0, 1)

    o_ref[...] = _dot(h_ref[...], w_ref[...])


def _in_proj(xp, xs, norm_w, mod, w_in_p):
    tm, tn = 512, 512
    return pl.pallas_call(
        functools.partial(_in_proj_kernel, n_prompt_tiles=N_PROMPT // tm),
        out_shape=jax.ShapeDtypeStruct((N_TOK, D_IN_PAD), F32),
        grid=(N_TOK // tm, D_IN_PAD // tn),
        in_specs=_split_specs(tm, D_MODEL, lambda j: 0) + [
                  pl.BlockSpec((1, D_MODEL), lambda i, j: (0, 0)),
                  pl.BlockSpec((None, 6, D_MODEL), lambda i, j: (_mod_row(i, tm), 0, 0)),
                  pl.BlockSpec((D_MODEL, tn), lambda i, j: (0, j))],
        out_specs=pl.BlockSpec((tm, tn), lambda i, j: (i, j)),
        scratch_shapes=[pltpu.VMEM((tm, D_MODEL), BF)],
        compiler_params=_cparams(("parallel", "arbitrary")),
    )(xp, xs, norm_w.reshape(1, D_MODEL), mod, w_in_p)


def _ffn_in_kernel(x_ref, nw_ref, mod_ref, wg_ref, wu_ref, o_ref, h_ref):
    @pl.when(pl.program_id(1) == 0)
    def _():
        h_ref[...] = _normmod(x_ref, nw_ref, mod_ref, 3, 4)

    h = h_ref[...]
    g = _dot(h, wg_ref[...])
    u = _dot(h, wu_ref[...])
    o_ref[...] = (g * jax.nn.sigmoid(g) * u).astype(BF)


def _ffn_in(x, norm_w, mod, w_ffn_in_p):
    tm, tn = 512, 512
    nj = D_FF_PAD // tn
    return pl.pallas_call(
        _ffn_in_kernel,
        out_shape=jax.ShapeDtypeStruct((N_TOK, D_FF_PAD), BF),
        grid=(N_TOK // tm, nj),
        in_specs=[pl.BlockSpec((tm, D_MODEL), lambda i, j: (i, 0)),
                  pl.BlockSpec((1, D_MODEL), lambda i, j: (0, 0)),
                  pl.BlockSpec((None, 6, D_MODEL), lambda i, j: (_mod_row(i, tm), 0, 0)),
                  pl.BlockSpec((D_MODEL, tn), lambda i, j: (0, j)),
                  pl.BlockSpec((D_MODEL, tn), lambda i, j: (0, j + nj))],
        out_specs=pl.BlockSpec((tm, tn), lambda i, j: (i, j)),
        scratch_shapes=[pltpu.VMEM((tm, D_MODEL), BF)],
        compiler_params=_cparams(("parallel", "arbitrary")),
    )(x, norm_w.reshape(1, D_MODEL), mod, w_ffn_in_p, w_ffn_in_p)


def _resid_proj_kernel(a_ref, w_ref, *refs, gate_idx, n_prompt_tiles):
    *x_refs, mod_ref, o_ref, acc_ref = refs
    k = pl.program_id(2)
    last = k == pl.num_programs(2) - 1

    @pl.when(k == 0)
    def _():
        acc_ref[...] = jnp.zeros_like(acc_ref)

    acc_ref[...] += _dot(a_ref[...], w_ref[...])

    def finish(x_ref):
        o_ref[...] = x_ref[...] + mod_ref[gate_idx:gate_idx + 1, :] * acc_ref[...]

    if len(x_refs) == 1:
        pl.when(last)(lambda: finish(x_refs[0]))
    else:
        is_prompt = pl.program_id(0) < n_prompt_tiles
        pl.when(last & is_prompt)(lambda: finish(x_refs[0]))
        pl.when(last & jnp.logical_not(is_prompt))(lambda: finish(x_refs[1]))


def _resid_proj(a, w, xs, mod, gate_idx, tk, row0, n_rows):
    tm, tn = 1024, 1024
    kdim = a.shape[1]
    r0 = row0 // tm
    if len(xs) == 1:
        x_specs = [pl.BlockSpec((tm, tn), lambda i, j, k: (r0 + i, j))]
    else:
        x_specs = _split_specs(tm, tn, lambda j, k: j)
    return pl.pallas_call(
        functools.partial(_resid_proj_kernel, gate_idx=gate_idx, n_prompt_tiles=N_PROMPT // tm),
        out_shape=jax.ShapeDtypeStruct((n_rows, D_MODEL), F32),
        grid=(n_rows // tm, D_MODEL // tn, kdim // tk),
        in_specs=[pl.BlockSpec((tm, tk), lambda i, j, k: (r0 + i, k)),
                  pl.BlockSpec((tk, tn), lambda i, j, k: (k, j))] + x_specs + [
                  pl.BlockSpec((None, 6, tn), lambda i, j, k: (_mod_row(r0 + i, tm), 0, j))],
        out_specs=pl.BlockSpec((tm, tn), lambda i, j, k: (i, j)),
        scratch_shapes=[pltpu.VMEM((tm, tn), F32)],
        compiler_params=_cparams(("parallel", "parallel", "arbitrary")),
    )(a, w, *xs, mod)


def _merge_kernel(yap_ref, yas_ref, ybp_ref, ybs_ref, wa_ref, wb_ref, gla_ref, glb_ref, o_ref, *, n_prompt_tiles):
    is_prompt = pl.program_id(0) < n_prompt_tiles
    ya = jnp.where(is_prompt, yap_ref[...], yas_ref[...])
    yb = jnp.where(is_prompt, ybp_ref[...], ybs_ref[...])
    pa = _dot(ya, wa_ref[...])
    pb = _dot(yb, wb_ref[...])
    o_ref[...] = (jax.nn.sigmoid(gla_ref[...]) * pa + jax.nn.sigmoid(glb_ref[...]) * pb).astype(BF)


def _merge(y_a, y_b, w_oa, w_ob, proj):
    tm, tn = 512, 1024
    return pl.pallas_call(
        functools.partial(_merge_kernel, n_prompt_tiles=N_PROMPT // tm),
        out_shape=jax.ShapeDtypeStruct((N_TOK, D_MODEL), BF),
        grid=(N_TOK // tm, D_MODEL // tn),
        in_specs=_split_specs(tm, D_A, lambda j: 0) + _split_specs(tm, D_A, lambda j: 0) + [
                  pl.BlockSpec((D_A, tn), lambda i, j: (0, j)),
                  pl.BlockSpec((D_A, tn), lambda i, j: (0, j)),
                  pl.BlockSpec((tm, tn), lambda i, j: (i, C_GLA // tn + j)),
                  pl.BlockSpec((tm, tn), lambda i, j: (i, C_GLB // tn + j))],
        out_specs=pl.BlockSpec((tm, tn), lambda i, j: (i, j)),
        compiler_params=_cparams(("parallel", "arbitrary")),
    )(*y_a, *y_b, w_oa, w_ob, proj, proj)


def _rope(x, cos_ref, sin_ref):
    swapped = pltpu.roll(x, ROPE_DIM // 2, axis=1) + pltpu.roll(x, LANES - ROPE_DIM // 2, axis=1)
    return x * cos_ref[...] + swapped * sin_ref[...]


def _rope_block(i, tm):
    n_prompt_tiles = N_PROMPT // tm
    tiles_per_seq = DEC_SEQ // tm
    return jnp.where(i < n_prompt_tiles, tiles_per_seq, (i - n_prompt_tiles) % tiles_per_seq)


def _q_kernel(ql_ref, nw_ref, w_ref, hw_ref, cos_ref, sin_ref, o_ref):
    x = ql_ref[...]
    y = x * lax.rsqrt(jnp.mean(x * x, axis=-1, keepdims=True) + NORM_EPS) * nw_ref[...]
    q = _dot(y.astype(BF), w_ref[...])
    hw = hw_ref[...]
    for h in range(H_B):
        qh = q[:, h * QK_PAD:(h + 1) * QK_PAD]
        inv = lax.rsqrt(jnp.sum(qh * qh, axis=-1, keepdims=True) * (1.0 / QK_DIM) + NORM_EPS)
        qn = qh * inv * hw
        o_ref[:, h * QK_PAD:h * QK_PAD + NOPE_DIM] = qn[:, :NOPE_DIM].astype(BF)
        o_ref[:, h * QK_PAD + NOPE_DIM:(h + 1) * QK_PAD] = _rope(qn[:, NOPE_DIM:], cos_ref, sin_ref).astype(BF)


def _q_proj(proj, q_norm_w, w_uq_p, q_head_w_p, cos_t, sin_t):
    tm = 256
    return pl.pallas_call(
        _q_kernel,
        out_shape=jax.ShapeDtypeStruct((N_TOK, H_B * QK_PAD), BF),
        grid=(N_TOK // tm,),
        in_specs=[pl.BlockSpec((tm, Q_LORA), lambda i: (i, C_QLAT // Q_LORA)),
                  pl.BlockSpec((1, Q_LORA), lambda i: (0, 0)),
                  pl.BlockSpec((Q_LORA, H_B * QK_PAD), lambda i: (0, 0)),
                  pl.BlockSpec((1, QK_PAD), lambda i: (0, 0)),
                  pl.BlockSpec((tm, LANES), lambda i: (_rope_block(i, tm), 0)),
                  pl.BlockSpec((tm, LANES), lambda i: (_rope_block(i, tm), 0))],
        out_specs=pl.BlockSpec((tm, H_B * QK_PAD), lambda i: (i, 0)),
        compiler_params=_cparams(("parallel",)),
    )(proj, q_norm_w.reshape(1, Q_LORA), w_uq_p, q_head_w_p, cos_t, sin_t)


def _kv_kernel(lat_ref, kpe_ref, nw_ref, w_ref, wkn_ref, wkr_ref, cos_ref, sin_ref, *out_refs, pre_norm):
    if pre_norm:
        k_ref, v_ref, ckv_ref = out_refs
        x = lat_ref[...]
        ckv = x * lax.rsqrt(jnp.mean(x * x, axis=-1, keepdims=True) + NORM_EPS) * nw_ref[...]
        ckv_ref[...] = ckv
    else:
        k_ref, v_ref = out_refs
        ckv = lat_ref[...]
    kv = _dot(ckv.astype(BF), w_ref[...])
    v_ref[...] = kv[:, H_B * NOPE_DIM:].astype(BF)
    kpe = kpe_ref[...]
    kpe_ss = jnp.sum(kpe * kpe, axis=-1, keepdims=True)
    wkn, wkr = wkn_ref[...], wkr_ref[...]
    for h in range(H_B):
        kn = kv[:, h * NOPE_DIM:(h + 1) * NOPE_DIM]
        inv = lax.rsqrt((jnp.sum(kn * kn, axis=-1, keepdims=True) + kpe_ss) * (1.0 / QK_DIM) + NORM_EPS)
        k_ref[:, h * QK_PAD:h * QK_PAD + NOPE_DIM] = (kn * inv * wkn).astype(BF)
        k_ref[:, h * QK_PAD + NOPE_DIM:(h + 1) * QK_PAD] = _rope(kpe * inv * wkr, cos_ref, sin_ref).astype(BF)


def _kv_proj(lat, lat_col, kpe, kpe_col, n_rows, kv_norm_w, w_ukv_p, wkn, wkr, cos_t, sin_t, rope_block, pre_norm):
    tm = 256
    out_shape = [jax.ShapeDtypeStruct((n_rows, H_B * QK_PAD), BF),
                 jax.ShapeDtypeStruct((n_rows, H_B * V_DIM), BF)]
    out_specs = [pl.BlockSpec((tm, H_B * QK_PAD), lambda i: (i, 0)),
                 pl.BlockSpec((tm, H_B * V_DIM), lambda i: (i, 0))]
    if pre_norm:
        out_shape.append(jax.ShapeDtypeStruct((n_rows, KV_LORA), F32))
        out_specs.append(pl.BlockSpec((tm, KV_LORA), lambda i: (i, 0)))
    return pl.pallas_call(
        functools.partial(_kv_kernel, pre_norm=pre_norm),
        out_shape=out_shape,
        grid=(n_rows // tm,),
        in_specs=[pl.BlockSpec((tm, KV_LORA), lambda i: (i, lat_col // KV_LORA)),
                  pl.BlockSpec((tm, LANES), lambda i: (i, kpe_col // LANES)),
                  pl.BlockSpec((1, KV_LORA), lambda i: (0, 0)),
                  pl.BlockSpec((KV_LORA, H_B * (NOPE_DIM + V_DIM)), lambda i: (0, 0)),
                  pl.BlockSpec((1, NOPE_DIM), lambda i: (0, 0)),
                  pl.BlockSpec((1, LANES), lambda i: (0, 0)),
                  pl.BlockSpec((tm, LANES), lambda i: (rope_block(i, tm), 0)),
                  pl.BlockSpec((tm, LANES), lambda i: (rope_block(i, tm), 0))],
        out_specs=out_specs,
        compiler_params=_cparams(("parallel",)),
    )(lat, kpe, kv_norm_w.reshape(1, KV_LORA), w_ukv_p, wkn, wkr, cos_t, sin_t)


_ATTN_SCALE = QK_DIM ** -0.5


def _attn_prompt_kernel(q_ref, k_ref, v_ref, o_ref, *, heads):
    for h in range(heads):
        q = q_ref[:, h * QK_PAD:(h + 1) * QK_PAD]
        k = k_ref[:, h * QK_PAD:(h + 1) * QK_PAD]
        s = _dot_nt(q, k) * _ATTN_SCALE
        e = jnp.exp(s - jnp.max(s, axis=-1, keepdims=True))
        p = e * (1.0 / jnp.sum(e, axis=-1, keepdims=True))
        o_ref[:, h * V_DIM:(h + 1) * V_DIM] = _dot(p.astype(BF), v_ref[:, h * V_DIM:(h + 1) * V_DIM]).astype(BF)


def _attn_prompt(q, k, v):
    heads = 4
    return pl.pallas_call(
        functools.partial(_attn_prompt_kernel, heads=heads),
        out_shape=jax.ShapeDtypeStruct((N_PROMPT, H_B * V_DIM), BF),
        grid=(BATCH, H_B // heads),
        in_specs=[pl.BlockSpec((SEQ, heads * QK_PAD), lambda b, g: (b, g)),
                  pl.BlockSpec((SEQ, heads * QK_PAD), lambda b, g: (b, g)),
                  pl.BlockSpec((SEQ, heads * V_DIM), lambda b, g: (b, g))],
        out_specs=pl.BlockSpec((SEQ, heads * V_DIM), lambda b, g: (b, g)),
        compiler_params=_cparams(("parallel", "parallel")),
    )(q, k, v)


def _attn_sample_kernel(q_ref, kc_ref, vc_ref, ks_ref, vs_ref, o_ref):
    q = q_ref[...]
    s1 = _dot_nt(q, kc_ref[...]) * _ATTN_SCALE
    s2 = _dot_nt(q, ks_ref[...]) * _ATTN_SCALE
    m = jnp.maximum(jnp.max(s1, axis=-1, keepdims=True), jnp.max(s2, axis=-1, keepdims=True))
    e1 = jnp.exp(s1 - m)
    e2 = jnp.exp(s2 - m)
    inv = 1.0 / (jnp.sum(e1, axis=-1, keepdims=True) + jnp.sum(e2, axis=-1, keepdims=True))
    o = _dot((e1 * inv).astype(BF), vc_ref[...]) + _dot((e2 * inv).astype(BF), vs_ref[...])
    o_ref[...] = o.astype(BF)


def _attn_sample(q, k, v, k_ctx, v_ctx):
    tq = 512
    qt = DEC_SEQ // tq
    q0 = N_PROMPT // tq
    s0 = N_PROMPT // DEC_SEQ
    return pl.pallas_call(
        _attn_sample_kernel,
        out_shape=jax.ShapeDtypeStruct((N_SAMPLE, H_B * V_DIM), BF),
        grid=(DEC_BATCH, H_B, qt),
        in_specs=[pl.BlockSpec((tq, QK_PAD), lambda b, h, t: (q0 + b * qt + t, h)),
                  pl.BlockSpec((PAST_LEN, QK_PAD), lambda b, h, t: (b, h)),
                  pl.BlockSpec((PAST_LEN, V_DIM), lambda b, h, t: (b, h)),
                  pl.BlockSpec((DEC_SEQ, QK_PAD), lambda b, h, t: (s0 + b, h)),
                  pl.BlockSpec((DEC_SEQ, V_DIM), lambda b, h, t: (s0 + b, h))],
        out_specs=pl.BlockSpec((tq, V_DIM), lambda b, h, t: (b * qt + t, h)),
        compiler_params=_cparams(("parallel", "parallel", "arbitrary")),
    )(q, k_ctx, v_ctx, k, v)


def _split3(x):
    hi = x.astype(BF)
    r1 = x - hi.astype(F32)
    mid = r1.astype(BF)
    lo = (r1 - mid.astype(F32)).astype(BF)
    return hi, mid, lo


def _head_sum(x, ones_blk):
    hi, mid, lo = _split3(x)
    return _dot(hi, ones_blk) + _dot(mid, ones_blk) + _dot(lo, ones_blk)


def _wkv_chunks_a(items, tri, strict, incl, lane_lo):
    C = WKV_CHUNK
    n2 = 2 * C

    def stack(x):
        return jnp.concatenate([jnp.where(lane_lo, x, 0.0), jnp.where(lane_lo, 0.0, x)], axis=0)

    def cumsum(it):
        lw = it["lw"]
        lw_hi = lw.astype(BF)
        lw_lo = (lw - lw_hi.astype(F32)).astype(BF)
        return _dot(tri[it["d"]], lw_hi) + _dot(tri[it["d"]], lw_lo)

    cs = [cumsum(it) for it in items]

    def operands(it, c):
        cl = c[C - 1:C, :] if it["d"] == 0 else c[0:1, :]
        e_inc = jnp.exp(c)
        e_neg = jnp.exp(-c)
        e_exc = jnp.exp(c - it["lw"])
        e_end = jnp.exp(cl - c)
        b, kd = it["b"], it["kd"]
        return dict(d=it["d"], e=jnp.exp(cl),
                    a_t=stack(-it["kk"] * e_exc), r_t=stack(it["r"] * e_inc),
                    b_t=stack(b * e_neg).astype(BF), k_t=stack(kd * e_neg).astype(BF),
                    b_h=stack(b * e_end).astype(BF), k_h=stack(kd * e_end).astype(BF),
                    v_s=stack(it["v"]).astype(BF))

    ops = [operands(it, c) for it, c in zip(items, cs)]
    scs = [_dot_nt(jnp.concatenate([o["a_t"], o["r_t"]], axis=0).astype(BF),
                   jnp.concatenate([o["b_t"], o["k_t"]], axis=0)) for o in ops]
    for o, sc in zip(ops, scs):
        d = o["d"]
        o["lp"] = jnp.where(strict[d], sc[:n2, :n2], 0.0).astype(BF)
        o["l_ak"] = jnp.where(strict[d], sc[:n2, n2:], 0.0).astype(BF)
        o["a_r"] = jnp.concatenate([jnp.where(incl[d], sc[n2:, :n2], 0.0),
                                    jnp.where(incl[d], sc[n2:, n2:], 0.0)], axis=1).astype(BF)

    xs = [jnp.concatenate([o["a_t"], _dot(o["l_ak"], o["v_s"])], axis=1) for o in ops]
    lps = [o["lp"] for o in ops]
    span = 1
    while span < C:
        xs = [x + _dot(lp, x.astype(BF)) for x, lp in zip(xs, lps)]
        span *= 2
        if span < C:
            lps = [_dot(lp, lp).astype(BF) for lp in lps]

    outs = []
    for o, x in zip(ops, xs):
        w = x[:, :PAIR].astype(BF)
        uv = jnp.concatenate([x[:, PAIR:].astype(BF), o["v_s"]], axis=0)
        q = (o["r_t"] + _dot(o["a_r"][:, :n2], w)).astype(BF)
        y0 = _dot(o["a_r"], uv)
        m = _dot_tn(w, o["b_h"]).astype(BF)
        n = _dot_tn(uv, jnp.concatenate([o["b_h"], o["k_h"]], axis=0))
        outs.append((q, y0[:C] + y0[C:], m, n, o["e"]))
    return outs


def _wkv_kernel(*refs, T, has_s0, out_state):
    (r_ref, k_ref, v_ref, lora_ref, lg_ref, w0_ref, wl_ref, a0_ref, wa_ref, wg_ref,
     kkw_ref, ka_ref, rk_ref, lnw_ref, lnb_ref) = refs[:15]
    pos = 15
    if has_s0:
        s0_ref = refs[pos]
        pos += 1
    y_ref = refs[pos]
    pos += 1
    if out_state:
        so_ref = refs[pos]
        pos += 1
    lw_s, b_s, kd_s, kk_s, y_s, q_s, m_s, n_s, e_s = refs[pos:]

    C = WKV_CHUNK
    nc = T // C
    lane = lax.broadcasted_iota(jnp.int32, (PAIR, PAIR), 1)
    row = lax.broadcasted_iota(jnp.int32, (PAIR, PAIR), 0)
    ones_blk = jnp.where((lane // HEAD_A) == (row // HEAD_A), 1.0, 0.0).astype(BF)
    lane_lo = lax.broadcasted_iota(jnp.int32, (C, PAIR), 1) < HEAD_A
    ci = lax.broadcasted_iota(jnp.int32, (C, C), 0)
    cj = lax.broadcasted_iota(jnp.int32, (C, C), 1)

    k = k_ref[...]
    kk = k * kkw_ref[...]
    kk = kk * lax.rsqrt(_head_sum(kk * kk, ones_blk) + 1e-12)
    kk_s[...] = kk
    for d in range(2):
        lw_in = lora_ref[:, d * R_LORA:(d + 1) * R_LORA]
        la_in = lora_ref[:, (2 + d) * R_LORA:(3 + d) * R_LORA]
        z = -(w0_ref[d:d + 1, :] + _dot(jnp.tanh(lw_in).astype(BF), wl_ref[d].astype(BF)))
        softplus = jnp.maximum(z, 0.0) + jnp.log1p(jnp.exp(-jnp.abs(z)))
        lw_s[d] = -jnp.exp(-softplus - 0.5)
        a = jax.nn.sigmoid(a0_ref[d:d + 1, :] + _dot(la_in.astype(BF), wa_ref[d].astype(BF)))
        kd_s[d] = k * (1.0 + (a - 1.0) * ka_ref[...])
        b_s[d] = kk * a

    tri = (jnp.where(cj <= ci, 1.0, 0.0).astype(BF), jnp.where(cj >= ci, 1.0, 0.0).astype(BF))
    strict = (lane < row, lane > row)
    incl = (lane <= row, lane >= row)

    def chunk_rows(idx):
        return pl.ds(idx * C, C) if isinstance(idx, int) else pl.ds(pl.multiple_of(idx * C, C), C)

    per_iter = WKV_UNROLL // 2

    def phase_a(base):
        keys = [(d, base + t) for t in range(per_iter) for d in range(2)]
        items = []
        for d, idx in keys:
            sl = chunk_rows(idx)
            items.append(dict(d=d, r=r_ref[sl, :], v=v_ref[sl, :], kk=kk_s[sl, :],
                              lw=lw_s[d, sl, :], b=b_s[d, sl, :], kd=kd_s[d, sl, :]))
        outs = _wkv_chunks_a(items, tri, strict, incl, lane_lo)
        for (d, idx), (q, y0, m, n, e) in zip(keys, outs):
            q_s[d, idx] = q
            y_s[d, chunk_rows(idx), :] = y0
            m_s[d, idx] = m
            n_s[d, idx] = n
            e_s[d, idx] = jnp.broadcast_to(e, (8, PAIR))

    if nc == per_iter:
        phase_a(0)
    else:
        def body_a(j, carry):
            phase_a(j * per_iter)
            return carry

        lax.fori_loop(0, nc // per_iter, body_a, 0)

    def body_b(i, carry):
        nxt = []
        for d, s, idx in ((0, carry[0], i), (1, carry[1], nc - 1 - i)):
            sl = chunk_rows(idx)
            sb = s.astype(BF)
            yq = _dot_nt(q_s[d, idx], sb)
            y_s[d, sl, :] += yq[:C] + yq[C:]
            nxt.append(s * e_s[d, idx][0:1, :] + _dot(sb, m_s[d, idx]) + n_s[d, idx])
        return tuple(nxt)

    if has_s0:
        init = (s0_ref[0], s0_ref[1])
    else:
        init = (jnp.zeros((PAIR, PAIR), F32), jnp.zeros((PAIR, PAIR), F32))
    s_fin = lax.fori_loop(0, nc, body_b, init)
    if out_state:
        for d in range(2):
            so_ref[d, 0] = s_fin[d][:HEAD_A, :HEAD_A]
            so_ref[d, 1] = s_fin[d][HEAD_A:, HEAD_A:]

    y = y_s[0] + y_s[1]
    inv_n = 1.0 / HEAD_A
    mu = _head_sum(y, ones_blk) * inv_n
    yc = y - mu
    var = _head_sum(yc * yc, ones_blk) * inv_n
    y = yc * lax.rsqrt(var + LN_X_EPS) * lnw_ref[...] + lnb_ref[...]
    r = r_ref[...]
    rr = r * rk_ref[...]
    bonus = (_head_sum(rr * kd_s[0], ones_blk) + _head_sum(rr * kd_s[1], ones_blk)) * v_ref[...]
    g = _dot(jax.nn.sigmoid(lg_ref[...]).astype(BF), wg_ref[...].astype(BF))
    y_ref[...] = ((y + bonus) * g).astype(BF)


def _wkv(proj, row0, n_seq, T, wts, s0):
    (w0, wl, a0, wa, wg, kkw, ka, rk, lnw, lnb) = wts
    sb = row0 // T
    cb = lambda col: col // PAIR
    vec = pl.BlockSpec((1, PAIR), lambda s, p: (0, p))
    in_specs = [pl.BlockSpec((T, PAIR), lambda s, p: (sb + s, cb(C_R) + p)),
                pl.BlockSpec((T, PAIR), lambda s, p: (sb + s, cb(C_K) + p)),
                pl.BlockSpec((T, PAIR), lambda s, p: (sb + s, cb(C_V) + p)),
                pl.BlockSpec((T, 4 * R_LORA), lambda s, p: (sb + s, C_LORA // (4 * R_LORA))),
                pl.BlockSpec((T, R_G_PAD), lambda s, p: (sb + s, C_LG // R_G_PAD)),
                pl.BlockSpec((2, PAIR), lambda s, p: (0, p)),
                pl.BlockSpec((2, R_LORA, PAIR), lambda s, p: (0, 0, p)),
                pl.BlockSpec((2, PAIR), lambda s, p: (0, p)),
                pl.BlockSpec((2, R_LORA, PAIR), lambda s, p: (0, 0, p)),
                pl.BlockSpec((R_G_PAD, PAIR), lambda s, p: (0, p)),
                vec, vec, vec, vec, vec]
    args = [proj, proj, proj, proj, proj, w0, wl, a0, wa, wg, kkw, ka, rk, lnw, lnb]
    has_s0 = s0 is not None
    if has_s0:
        in_specs.append(pl.BlockSpec((None, 2, None, PAIR, PAIR), lambda s, p: (s, 0, p, 0, 0)))
        args.append(s0)
    out_shape = [jax.ShapeDtypeStruct((n_seq * T, D_A), BF)]
    out_specs = [pl.BlockSpec((T, PAIR), lambda s, p: (s, p))]
    out_state = not has_s0
    if out_state:
        out_shape.append(jax.ShapeDtypeStruct((n_seq, 2, H_A, HEAD_A, HEAD_A), F32))
        out_specs.append(pl.BlockSpec((None, 2, 2, HEAD_A, HEAD_A), lambda s, p: (s, 0, p, 0, 0)))
    nc = T // WKV_CHUNK
    return pl.pallas_call(
        functools.partial(_wkv_kernel, T=T, has_s0=has_s0, out_state=out_state),
        out_shape=out_shape,
        grid=(n_seq, N_PAIR),
        in_specs=in_specs,
        out_specs=out_specs,
        scratch_shapes=[pltpu.VMEM((2, T, PAIR), F32), pltpu.VMEM((2, T, PAIR), F32),
                        pltpu.VMEM((2, T, PAIR), F32), pltpu.VMEM((T, PAIR), F32),
                        pltpu.VMEM((2, T, PAIR), F32),
                        pltpu.VMEM((2, nc, PAIR, PAIR), BF), pltpu.VMEM((2, nc, PAIR, PAIR), BF),
                        pltpu.VMEM((2, nc, PAIR, PAIR), F32), pltpu.VMEM((2, nc, 8, PAIR), F32)],
        compiler_params=_cparams(("parallel", "arbitrary")),
    )(*args)


def _rope_tables():
    rows = DEC_SEQ // GRID_W
    row = jnp.repeat(jnp.arange(rows), GRID_W).astype(F32)
    col = jnp.tile(jnp.arange(GRID_W), rows).astype(F32)
    n_freq = ROPE_DIM // 4
    inv = ROPE_THETA ** (-jnp.arange(n_freq, dtype=F32) / n_freq)
    ang = jnp.concatenate([row[:, None] * inv, col[:, None] * inv], axis=-1)
    cos, sin = jnp.cos(ang), jnp.sin(ang)
    pad = jnp.zeros((DEC_SEQ, LANES - ROPE_DIM), F32)
    cos_t = jnp.concatenate([cos, cos, pad], axis=-1)
    sin_t = jnp.concatenate([-sin, sin, pad], axis=-1)
    ident = 256
    cos_t = jnp.concatenate([cos_t, jnp.ones((ident, LANES), F32)], axis=0)
    sin_t = jnp.concatenate([sin_t, jnp.zeros((ident, LANES), F32)], axis=0)
    return cos_t, sin_t


def _pair_state_in(state):
    b = state.shape[0]
    s = state.reshape(b, 2, N_PAIR, 2, HEAD_A, HEAD_A)
    z = jnp.zeros_like(s[:, :, :, 0])
    top = jnp.concatenate([s[:, :, :, 0], z], axis=-1)
    bot = jnp.concatenate([z, s[:, :, :, 1]], axis=-1)
    return jnp.concatenate([top, bot], axis=-2)


def kernel(x_prompt, x_sample, c, state_rwkv, cache_mla_ckv, cache_mla_kpe, c_ctx, w_mod, b_mod, norm_mix_w, w_in, rwkv_w0, rwkv_w_lora_b, rwkv_a0, rwkv_a_lora_b, rwkv_g_lora_b, rwkv_k_k, rwkv_k_a, rwkv_r_k, rwkv_ln_w, rwkv_ln_b, mla_q_norm_w, mla_w_uq, mla_kv_norm_w, mla_w_ukv, mla_q_head_norm, mla_k_head_norm, w_o_rwkv, w_o_mla, w_out, norm_ffn_w, w_ffn_in, w_ffn_out):
    l = 0
    xp = x_prompt.reshape(N_PROMPT, D_MODEL)
    xs = x_sample.reshape(N_SAMPLE, D_MODEL)

    wi = w_in[l]
    zc = lambda n: jnp.zeros((D_MODEL, n), F32)
    w_in_p = jnp.concatenate([wi[:, :7136], zc(32), wi[:, 7136:8736], zc(C_GLA - C_KPE - ROPE_DIM), wi[:, 8736:]],
                             axis=1).astype(BF)
    w_uq_p = jnp.pad(mla_w_uq[l].reshape(Q_LORA, H_B, QK_DIM), ((0, 0), (0, 0), (0, QK_PAD - QK_DIM)))
    w_uq_p = w_uq_p.reshape(Q_LORA, H_B * QK_PAD).astype(BF)
    q_head_w_p = jnp.pad(mla_q_head_norm[l], (0, QK_PAD - QK_DIM)).reshape(1, QK_PAD)
    wkv3 = mla_w_ukv[l].reshape(KV_LORA, H_B, NOPE_DIM + V_DIM)
    w_ukv_p = jnp.concatenate([wkv3[:, :, :NOPE_DIM].reshape(KV_LORA, H_B * NOPE_DIM),
                               wkv3[:, :, NOPE_DIM:].reshape(KV_LORA, H_B * V_DIM)], axis=1).astype(BF)
    wkn = mla_k_head_norm[l, :NOPE_DIM].reshape(1, NOPE_DIM)
    wkr = jnp.pad(mla_k_head_norm[l, NOPE_DIM:], (0, LANES - ROPE_DIM)).reshape(1, LANES)
    wf = w_ffn_in[l]
    zf = jnp.zeros((D_MODEL, D_FF_PAD - D_FF), F32)
    w_ffn_in_p = jnp.concatenate([wf[:, :D_FF], zf, wf[:, D_FF:], zf], axis=1).astype(BF)
    w_ffn_out_p = jnp.pad(w_ffn_out[l], ((0, D_FF_PAD - D_FF), (0, 0))).astype(BF)
    w_oa = w_o_rwkv[l].astype(BF)
    w_ob = w_o_mla[l].astype(BF)
    w_out_b = w_out[l].astype(BF)
    wg_p = jnp.pad(rwkv_g_lora_b[l], ((0, R_G_PAD - R_G), (0, 0)))
    row = lambda a: a.reshape(1, D_A)
    wkv_w = (rwkv_w0[l], rwkv_w_lora_b[l], rwkv_a0[l], rwkv_a_lora_b[l], wg_p, row(rwkv_k_k[l]), row(rwkv_k_a[l]),
             row(rwkv_r_k[l]), row(rwkv_ln_w[l]), row(rwkv_ln_b[l]))
    cos_t, sin_t = _rope_tables()

    cond8 = jnp.concatenate([c_ctx[None, :], c, jnp.zeros((8 - 1 - DEC_BATCH, D_MODEL), F32)], axis=0)
    mod = _modulation(cond8, w_mod[l], b_mod[l]).reshape(8, 6, D_MODEL)

    proj = _in_proj(xp, xs, norm_mix_w[l], mod, w_in_p)

    ya_p, s_new = _wkv(proj, 0, BATCH, SEQ, wkv_w, None)
    ya_s = _wkv(proj, N_PROMPT, DEC_BATCH, DEC_SEQ, wkv_w, _pair_state_in(state_rwkv[:, l]))[0]

    q = _q_proj(proj, mla_q_norm_w[l], w_uq_p, q_head_w_p, cos_t, sin_t)
    k, v, ckv = _kv_proj(proj, C_KVLAT, proj, C_KPE, N_TOK, mla_kv_norm_w[l], w_ukv_p, wkn, wkr, cos_t, sin_t,
                         _rope_block, True)
    ctx_ckv = cache_mla_ckv[:, l].reshape(DEC_BATCH * PAST_LEN, KV_LORA)
    ctx_kpe = jnp.pad(cache_mla_kpe[:, l].reshape(DEC_BATCH * PAST_LEN, ROPE_DIM), ((0, 0), (0, LANES - ROPE_DIM)))
    k_ctx, v_ctx = _kv_proj(ctx_ckv, 0, ctx_kpe, 0, DEC_BATCH * PAST_LEN, mla_kv_norm_w[l], w_ukv_p, wkn, wkr,
                            cos_t, sin_t, lambda i, tm: DEC_SEQ // tm, False)
    yb_p = _attn_prompt(q, k, v)
    yb_s = _attn_sample(q, k, v, k_ctx, v_ctx)

    mixed = _merge((ya_p, ya_s), (yb_p, yb_s), w_oa, w_ob, proj)
    x1 = _resid_proj(mixed, w_out_b, (xp, xs), mod, 2, 2048, 0, N_TOK)
    act = _ffn_in(x1, norm_ffn_w[l], mod, w_ffn_in_p)
    y_p = _resid_proj(act, w_ffn_out_p, (x1,), mod, 5, 1408, 0, N_PROMPT).reshape(BATCH, SEQ, D_MODEL)
    y_s = _resid_proj(act, w_ffn_out_p, (x1,), mod, 5, 1408, N_PROMPT, N_SAMPLE).reshape(DEC_BATCH, DEC_SEQ, D_MODEL)
    new_state = s_new[:, None]
    new_ckv = ckv[:N_PROMPT].reshape(BATCH, 1, SEQ, KV_LORA)
    new_kpe = proj[:N_PROMPT, C_KPE:C_KPE + ROPE_DIM].reshape(BATCH, 1, SEQ, ROPE_DIM)
    return y_p, y_s, new_state, new_ckv, new_kpe
```

```python
import functools

import numpy as np
import jax
import jax.numpy as jnp
from jax import lax
from jax.experimental import pallas as pl
from jax.experimental.pallas import tpu as pltpu

F32 = jnp.float32
BF = jnp.bfloat16

D_MODEL = 4096
BATCH, SEQ = 32, 256
DEC_BATCH, DEC_SEQ, PAST_LEN = 4, 1024, 512
GRID_W = 64
D_A, HEAD_A = 2048, 64
H_A = D_A // HEAD_A
R_LORA, R_G = 128, 480
LN_X_EPS = 64e-5
H_B, Q_LORA, KV_LORA = 16, 1024, 512
NOPE_DIM, ROPE_DIM, V_DIM = 128, 64, 128
QK_DIM = NOPE_DIM + ROPE_DIM
ROPE_THETA = 10000.0
D_FF = 11008
NORM_EPS = 1e-6

N_PROMPT = BATCH * SEQ
N_SAMPLE = DEC_BATCH * DEC_SEQ
N_TOK = N_PROMPT + N_SAMPLE

LANES = 128
R_G_PAD = 512
QK_PAD = 256
D_FF_PAD = 11264
FF_BLK = D_FF_PAD - D_FF
C_R, C_K, C_V = 0, 2048, 4096
C_LORA, C_LG, C_QLAT, C_KVLAT, C_KPE = 6144, 6656, 7168, 8192, 8704
C_GLA, C_GLB, D_IN_PAD = 9216, 13312, 17408
WKV_CHUNK = 64
WKV_UNROLL = 8
WKV_PAIRS = 2
PAIR = 2 * HEAD_A
N_PAIR = D_A // PAIR

VMEM_LIMIT = 56 * 2**20


def _cparams(sem):
    return pltpu.CompilerParams(dimension_semantics=sem, vmem_limit_bytes=VMEM_LIMIT)


def _dot(a, b):
    return jnp.dot(a, b, preferred_element_type=F32)


def _dot_nt(a, b):
    return lax.dot_general(a, b, (((1,), (1,)), ((), ())), preferred_element_type=F32)


def _dot_tn(a, b):
    return lax.dot_general(a, b, (((0,), (0,)), ((), ())), preferred_element_type=F32)


def _mod_row(i, tm):
    n_prompt_tiles = N_PROMPT // tm
    tiles_per_seq = DEC_SEQ // tm
    return jnp.where(i < n_prompt_tiles, 0, 1 + (i - n_prompt_tiles) // tiles_per_seq)


def _mod_kernel(c_ref, w_ref, b_ref, o_ref):
    c = c_ref[...]
    s = c * jax.nn.sigmoid(c)
    o_ref[...] = _dot(s.astype(BF), w_ref[...].astype(BF)) + b_ref[...]


def _modulation(cond8, w_mod, b_mod):
    tn = 512
    n = w_mod.shape[1]
    return pl.pallas_call(
        _mod_kernel,
        out_shape=jax.ShapeDtypeStruct((8, n), F32),
        grid=(n // tn,),
        in_specs=[pl.BlockSpec((8, D_MODEL), lambda j: (0, 0)),
                  pl.BlockSpec((D_MODEL, tn), lambda j: (0, j)),
                  pl.BlockSpec((1, tn), lambda j: (0, j))],
        out_specs=pl.BlockSpec((8, tn), lambda j: (0, j)),
        compiler_params=_cparams(("arbitrary",)),
    )(cond8, w_mod, b_mod.reshape(1, n))


def _normmod(x_ref, nw_ref, mod_ref, shift_idx, scale_idx):
    x = x_ref[...]
    y = x * lax.rsqrt(jnp.mean(x * x, axis=-1, keepdims=True) + NORM_EPS)
    h = (y * nw_ref[...]) * (1.0 + mod_ref[scale_idx:scale_idx + 1, :]) + mod_ref[shift_idx:shift_idx + 1, :]
    return h.astype(BF)


def _split_specs(tm, width, col):
    npt = N_PROMPT // tm
    return [pl.BlockSpec((tm, width), lambda i, *_: (jnp.minimum(i, npt - 1), col(*_))),
            pl.BlockSpec((tm, width), lambda i, *_: (jnp.maximum(i - npt, 0), col(*_)))]


def _in_proj_kernel(xp_ref, xs_ref, nw_ref, mod_ref, w_ref, o_ref, h_ref, *, n_prompt_tiles):
    first = pl.program_id(1) == 0
    is_prompt = pl.program_id(0) < n_prompt_tiles

    @pl.when(first & is_prompt)
    def _():
        h_ref[...] = _normmod(xp_ref, nw_ref, mod_ref, 0, 1)

    @pl.when(first & jnp.logical_not(is_prompt))
    def _():
        h_ref[...] = _normmod(xs_ref, nw_ref, mod_ref, 0, 1)

    o_ref[...] = _dot(h_ref[...], w_ref[...])


def _in_proj(xp, xs, norm_w, mod, w_in_p):
    tm, tn = 512, 512
    return pl.pallas_call(
        functools.partial(_in_proj_kernel, n_prompt_tiles=N_PROMPT // tm),
        out_shape=jax.ShapeDtypeStruct((N_TOK, D_IN_PAD), F32),
        grid=(N_TOK // tm, D_IN_PAD // tn),
        in_specs=_split_specs(tm, D_MODEL, lambda j: 0) + [
                  pl.BlockSpec((1, D_MODEL), lambda i, j: (0, 0)),
                  pl.BlockSpec((None, 6, D_MODEL), lambda i, j: (_mod_row(i, tm), 0, 0)),
                  pl.BlockSpec((D_MODEL, tn), lambda i, j: (0, j))],
        out_specs=pl.BlockSpec((tm, tn), lambda i, j: (i, j)),
        scratch_shapes=[pltpu.VMEM((tm, D_MODEL), BF)],
        compiler_params=_cparams(("parallel", "arbitrary")),
    )(xp, xs, norm_w.reshape(1, D_MODEL), mod, w_in_p)


def _ffn_in_kernel(x_ref, nw_ref, mod_ref, wg_ref, wu_ref, o_ref, h_ref):
    @pl.when(pl.program_id(1) == 0)
    def _():
        h_ref[...] = _normmod(x_ref, nw_ref, mod_ref, 3, 4)

    h = h_ref[...]
    g = _dot(h, wg_ref[...])
    u = _dot(h, wu_ref[...])
    o_ref[...] = (g * jax.nn.sigmoid(g) * u).astype(BF)


def _ffn_in(x, norm_w, mod, w_ffn_in_p):
    tm, tn = 512, 512
    nj = D_FF_PAD // tn
    return pl.pallas_call(
        _ffn_in_kernel,
        out_shape=jax.ShapeDtypeStruct((N_TOK, D_FF_PAD), BF),
        grid=(N_TOK // tm, nj),
        in_specs=[pl.BlockSpec((tm, D_MODEL), lambda i, j: (i, 0)),
                  pl.BlockSpec((1, D_MODEL), lambda i, j: (0, 0)),
                  pl.BlockSpec((None, 6, D_MODEL), lambda i, j: (_mod_row(i, tm), 0, 0)),
                  pl.BlockSpec((D_MODEL, tn), lambda i, j: (0, j)),
                  pl.BlockSpec((D_MODEL, tn), lambda i, j: (0, j + nj))],
        out_specs=pl.BlockSpec((tm, tn), lambda i, j: (i, j)),
        scratch_shapes=[pltpu.VMEM((tm, D_MODEL), BF)],
        compiler_params=_cparams(("parallel", "arbitrary")),
    )(x, norm_w.reshape(1, D_MODEL), mod, w_ffn_in_p, w_ffn_in_p)


def _resid_proj_kernel(a_ref, w_ref, *refs, gate_idx, n_prompt_tiles):
    *x_refs, mod_ref, o_ref, acc_ref = refs
    k = pl.program_id(2)
    last = k == pl.num_programs(2) - 1

    @pl.when(k == 0)
    def _():
        acc_ref[...] = jnp.zeros_like(acc_ref)

    acc_ref[...] += _dot(a_ref[...], w_ref[...])

    def finish(x_ref):
        o_ref[...] = x_ref[...] + mod_ref[gate_idx:gate_idx + 1, :] * acc_ref[...]

    if len(x_refs) == 1:
        pl.when(last)(lambda: finish(x_refs[0]))
    else:
        is_prompt = pl.program_id(0) < n_prompt_tiles
        pl.when(last & is_prompt)(lambda: finish(x_refs[0]))
        pl.when(last & jnp.logical_not(is_prompt))(lambda: finish(x_refs[1]))


def _resid_proj(a, w, xs, mod, gate_idx, tk, row0, n_rows):
    tm, tn = 1024, 1024
    kdim = a.shape[1]
    r0 = row0 // tm
    if len(xs) == 1:
        x_specs = [pl.BlockSpec((tm, tn), lambda i, j, k: (r0 + i, j))]
    else:
        x_specs = _split_specs(tm, tn, lambda j, k: j)
    return pl.pallas_call(
        functools.partial(_resid_proj_kernel, gate_idx=gate_idx, n_prompt_tiles=N_PROMPT // tm),
        out_shape=jax.ShapeDtypeStruct((n_rows, D_MODEL), F32),
        grid=(n_rows // tm, D_MODEL // tn, kdim // tk),
        in_specs=[pl.BlockSpec((tm, tk), lambda i, j, k: (r0 + i, k)),
                  pl.BlockSpec((tk, tn), lambda i, j, k: (k, j))] + x_specs + [
                  pl.BlockSpec((None, 6, tn), lambda i, j, k: (_mod_row(r0 + i, tm), 0, j))],
        out_specs=pl.BlockSpec((tm, tn), lambda i, j, k: (i, j)),
        scratch_shapes=[pltpu.VMEM((tm, tn), F32)],
        compiler_params=_cparams(("parallel", "parallel", "arbitrary")),
    )(a, w, *xs, mod)


def _merge_kernel(yap_ref, yas_ref, ybp_ref, ybs_ref, wa_ref, wb_ref, gla_ref, glb_ref, o_ref, *, n_prompt_tiles):
    is_prompt = pl.program_id(0) < n_prompt_tiles
    ya = jnp.where(is_prompt, yap_ref[...], yas_ref[...])
    yb = jnp.where(is_prompt, ybp_ref[...], ybs_ref[...])
    pa = _dot(ya, wa_ref[...])
    pb = _dot(yb, wb_ref[...])
    o_ref[...] = (jax.nn.sigmoid(gla_ref[...]) * pa + jax.nn.sigmoid(glb_ref[...]) * pb).astype(BF)


def _merge(y_a, y_b, w_oa, w_ob, proj):
    tm, tn = 512, 1024
    return pl.pallas_call(
        functools.partial(_merge_kernel, n_prompt_tiles=N_PROMPT // tm),
        out_shape=jax.ShapeDtypeStruct((N_TOK, D_MODEL), BF),
        grid=(N_TOK // tm, D_MODEL // tn),
        in_specs=_split_specs(tm, D_A, lambda j: 0) + _split_specs(tm, D_A, lambda j: 0) + [
                  pl.BlockSpec((D_A, tn), lambda i, j: (0, j)),
                  pl.BlockSpec((D_A, tn), lambda i, j: (0, j)),
                  pl.BlockSpec((tm, tn), lambda i, j: (i, C_GLA // tn + j)),
                  pl.BlockSpec((tm, tn), lambda i, j: (i, C_GLB // tn + j))],
        out_specs=pl.BlockSpec((tm, tn), lambda i, j: (i, j)),
        compiler_params=_cparams(("parallel", "arbitrary")),
    )(*y_a, *y_b, w_oa, w_ob, proj, proj)


def _rope(x, cos_ref, sin_ref):
    swapped = pltpu.roll(x, ROPE_DIM // 2, axis=1) + pltpu.roll(x, LANES - ROPE_DIM // 2, axis=1)
    return x * cos_ref[...] + swapped * sin_ref[...]


def _rope_block(i, tm):
    n_prompt_tiles = N_PROMPT // tm
    tiles_per_seq = DEC_SEQ // tm
    return jnp.where(i < n_prompt_tiles, tiles_per_seq, (i - n_prompt_tiles) % tiles_per_seq)


def _q_kernel(ql_ref, nw_ref, w_ref, hw_ref, cos_ref, sin_ref, o_ref):
    x = ql_ref[...]
    y = x * lax.rsqrt(jnp.mean(x * x, axis=-1, keepdims=True) + NORM_EPS) * nw_ref[...]
    q = _dot(y.astype(BF), w_ref[...])
    hw = hw_ref[...]
    for h in range(H_B):
        qh = q[:, h * QK_PAD:(h + 1) * QK_PAD]
        inv = lax.rsqrt(jnp.sum(qh * qh, axis=-1, keepdims=True) * (1.0 / QK_DIM) + NORM_EPS)
        qn = qh * inv * hw
        o_ref[:, h * QK_PAD:h * QK_PAD + NOPE_DIM] = qn[:, :NOPE_DIM].astype(BF)
        o_ref[:, h * QK_PAD + NOPE_DIM:(h + 1) * QK_PAD] = _rope(qn[:, NOPE_DIM:], cos_ref, sin_ref).astype(BF)


def _q_proj(proj, q_norm_w, w_uq_p, q_head_w_p, cos_t, sin_t):
    tm = 256
    return pl.pallas_call(
        _q_kernel,
        out_shape=jax.ShapeDtypeStruct((N_TOK, H_B * QK_PAD), BF),
        grid=(N_TOK // tm,),
        in_specs=[pl.BlockSpec((tm, Q_LORA), lambda i: (i, C_QLAT // Q_LORA)),
                  pl.BlockSpec((1, Q_LORA), lambda i: (0, 0)),
                  pl.BlockSpec((Q_LORA, H_B * QK_PAD), lambda i: (0, 0)),
                  pl.BlockSpec((1, QK_PAD), lambda i: (0, 0)),
                  pl.BlockSpec((tm, LANES), lambda i: (_rope_block(i, tm), 0)),
                  pl.BlockSpec((tm, LANES), lambda i: (_rope_block(i, tm), 0))],
        out_specs=pl.BlockSpec((tm, H_B * QK_PAD), lambda i: (i, 0)),
        compiler_params=_cparams(("parallel",)),
    )(proj, q_norm_w.reshape(1, Q_LORA), w_uq_p, q_head_w_p, cos_t, sin_t)


def _kv_kernel(lat_ref, kpe_ref, nw_ref, w_ref, wkn_ref, wkr_ref, cos_ref, sin_ref, *out_refs, pre_norm):
    if pre_norm:
        k_ref, v_ref, ckv_ref = out_refs
        x = lat_ref[...]
        ckv = x * lax.rsqrt(jnp.mean(x * x, axis=-1, keepdims=True) + NORM_EPS) * nw_ref[...]
        ckv_ref[...] = ckv
    else:
        k_ref, v_ref = out_refs
        ckv = lat_ref[...]
    kv = _dot(ckv.astype(BF), w_ref[...])
    v_ref[...] = kv[:, H_B * NOPE_DIM:].astype(BF)
    kpe = kpe_ref[...]
    kpe_ss = jnp.sum(kpe * kpe, axis=-1, keepdims=True)
    wkn, wkr = wkn_ref[...], wkr_ref[...]
    for h in range(H_B):
        kn = kv[:, h * NOPE_DIM:(h + 1) * NOPE_DIM]
        inv = lax.rsqrt((jnp.sum(kn * kn, axis=-1, keepdims=True) + kpe_ss) * (1.0 / QK_DIM) + NORM_EPS)
        k_ref[:, h * QK_PAD:h * QK_PAD + NOPE_DIM] = (kn * inv * wkn).astype(BF)
        k_ref[:, h * QK_PAD + NOPE_DIM:(h + 1) * QK_PAD] = _rope(kpe * inv * wkr, cos_ref, sin_ref).astype(BF)


def _kv_proj(lat, lat_col, kpe, kpe_col, n_rows, kv_norm_w, w_ukv_p, wkn, wkr, cos_t, sin_t, rope_block, pre_norm):
    tm = 256
    out_shape = [jax.ShapeDtypeStruct((n_rows, H_B * QK_PAD), BF),
                 jax.ShapeDtypeStruct((n_rows, H_B * V_DIM), BF)]
    out_specs = [pl.BlockSpec((tm, H_B * QK_PAD), lambda i: (i, 0)),
                 pl.BlockSpec((tm, H_B * V_DIM), lambda i: (i, 0))]
    if pre_norm:
        out_shape.append(jax.ShapeDtypeStruct((n_rows, KV_LORA), F32))
        out_specs.append(pl.BlockSpec((tm, KV_LORA), lambda i: (i, 0)))
    return pl.pallas_call(
        functools.partial(_kv_kernel, pre_norm=pre_norm),
        out_shape=out_shape,
        grid=(n_rows // tm,),
        in_specs=[pl.BlockSpec((tm, KV_LORA), lambda i: (i, lat_col // KV_LORA)),
                  pl.BlockSpec((tm, LANES), lambda i: (i, kpe_col // LANES)),
                  pl.BlockSpec((1, KV_LORA), lambda i: (0, 0)),
                  pl.BlockSpec((KV_LORA, H_B * (NOPE_DIM + V_DIM)), lambda i: (0, 0)),
                  pl.BlockSpec((1, NOPE_DIM), lambda i: (0, 0)),
                  pl.BlockSpec((1, LANES), lambda i: (0, 0)),
                  pl.BlockSpec((tm, LANES), lambda i: (rope_block(i, tm), 0)),
                  pl.BlockSpec((tm, LANES), lambda i: (rope_block(i, tm), 0))],
        out_specs=out_specs,
        compiler_params=_cparams(("parallel",)),
    )(lat, kpe, kv_norm_w.reshape(1, KV_LORA), w_ukv_p, wkn, wkr, cos_t, sin_t)


_ATTN_SCALE = QK_DIM ** -0.5


def _attn_prompt_kernel(q_ref, k_ref, v_ref, o_ref, *, heads):
    for h in range(heads):
        q = q_ref[:, h * QK_PAD:(h + 1) * QK_PAD]
        k = k_ref[:, h * QK_PAD:(h + 1) * QK_PAD]
        s = _dot_nt(q, k) * _ATTN_SCALE
        e = jnp.exp(s - jnp.max(s, axis=-1, keepdims=True))
        p = e * (1.0 / jnp.sum(e, axis=-1, keepdims=True))
        o_ref[:, h * V_DIM:(h + 1) * V_DIM] = _dot(p.astype(BF), v_ref[:, h * V_DIM:(h + 1) * V_DIM]).astype(BF)


def _attn_prompt(q, k, v):
    heads = 4
    return pl.pallas_call(
        functools.partial(_attn_prompt_kernel, heads=heads),
        out_shape=jax.ShapeDtypeStruct((N_PROMPT, H_B * V_DIM), BF),
        grid=(BATCH, H_B // heads),
        in_specs=[pl.BlockSpec((SEQ, heads * QK_PAD), lambda b, g: (b, g)),
                  pl.BlockSpec((SEQ, heads * QK_PAD), lambda b, g: (b, g)),
                  pl.BlockSpec((SEQ, heads * V_DIM), lambda b, g: (b, g))],
        out_specs=pl.BlockSpec((SEQ, heads * V_DIM), lambda b, g: (b, g)),
        compiler_params=_cparams(("parallel", "parallel")),
    )(q, k, v)


def _attn_sample_kernel(q_ref, kc_ref, vc_ref, ks_ref, vs_ref, o_ref):
    q = q_ref[...]
    s1 = _dot_nt(q, kc_ref[...]) * _ATTN_SCALE
    s2 = _dot_nt(q, ks_ref[...]) * _ATTN_SCALE
    m = jnp.maximum(jnp.max(s1, axis=-1, keepdims=True), jnp.max(s2, axis=-1, keepdims=True))
    e1 = jnp.exp(s1 - m)
    e2 = jnp.exp(s2 - m)
    inv = 1.0 / (jnp.sum(e1, axis=-1, keepdims=True) + jnp.sum(e2, axis=-1, keepdims=True))
    o = _dot((e1 * inv).astype(BF), vc_ref[...]) + _dot((e2 * inv).astype(BF), vs_ref[...])
    o_ref[...] = o.astype(BF)


def _attn_sample(q, k, v, k_ctx, v_ctx):
    tq = 512
    qt = DEC_SEQ // tq
    q0 = N_PROMPT // tq
    s0 = N_PROMPT // DEC_SEQ
    return pl.pallas_call(
        _attn_sample_kernel,
        out_shape=jax.ShapeDtypeStruct((N_SAMPLE, H_B * V_DIM), BF),
        grid=(DEC_BATCH, H_B, qt),
        in_specs=[pl.BlockSpec((tq, QK_PAD), lambda b, h, t: (q0 + b * qt + t, h)),
                  pl.BlockSpec((PAST_LEN, QK_PAD), lambda b, h, t: (b, h)),
                  pl.BlockSpec((PAST_LEN, V_DIM), lambda b, h, t: (b, h)),
                  pl.BlockSpec((DEC_SEQ, QK_PAD), lambda b, h, t: (s0 + b, h)),
                  pl.BlockSpec((DEC_SEQ, V_DIM), lambda b, h, t: (s0 + b, h))],
        out_specs=pl.BlockSpec((tq, V_DIM), lambda b, h, t: (b * qt + t, h)),
        compiler_params=_cparams(("parallel", "parallel", "arbitrary")),
    )(q, k_ctx, v_ctx, k, v)


def _head_sum(x, ones_blk):
    hi = x.astype(BF)
    lo = (x - hi.astype(F32)).astype(BF)
    return _dot(hi, ones_blk) + _dot(lo, ones_blk)


def _wkv_chunks_a(items, tri, strict, incl, lane_lo, eye):
    C = WKV_CHUNK
    n2 = 2 * C

    def stack(x):
        return jnp.concatenate([jnp.where(lane_lo, x, 0.0), jnp.where(lane_lo, 0.0, x)], axis=0)

    def cumsum(it):
        lw = it["lw"]
        lw_hi = lw.astype(BF)
        lw_lo = (lw - lw_hi.astype(F32)).astype(BF)
        return _dot(tri[it["d"]], lw_hi) + _dot(tri[it["d"]], lw_lo)

    cs = [cumsum(it) for it in items]

    def operands(it, c):
        cl = c[C - 1:C, :] if it["d"] == 0 else c[0:1, :]
        e_inc = jnp.exp(c)
        e_neg = jnp.exp(-c)
        e_exc = jnp.exp(c - it["lw"])
        e_end = jnp.exp(cl - c)
        b, kd = it["b"], it["kd"]
        return dict(d=it["d"], e=jnp.exp(cl),
                    a_t=stack(-it["kk"] * e_exc).astype(BF), r_t=stack(it["r"] * e_inc),
                    b_t=stack(b * e_neg).astype(BF), k_t=stack(kd * e_neg).astype(BF),
                    bk_h=jnp.concatenate([stack(b * e_end), stack(kd * e_end)], axis=0).astype(BF),
                    v_s=stack(it["v"]).astype(BF))

    ops = [operands(it, c) for it, c in zip(items, cs)]
    scs = [_dot_nt(jnp.concatenate([o["a_t"], o["r_t"].astype(BF)], axis=0),
                   jnp.concatenate([o["b_t"], o["k_t"]], axis=0)) for o in ops]
    for o, sc in zip(ops, scs):
        d = o["d"]
        o["lp"] = jnp.where(strict[d], sc[:n2, :n2], 0.0).astype(BF)
        o["l_ak"] = jnp.where(strict[d], sc[:n2, n2:], 0.0).astype(BF)
        o["a_r"] = jnp.concatenate([jnp.where(incl[d], sc[n2:, :n2], 0.0),
                                    jnp.where(incl[d], sc[n2:, n2:], 0.0)], axis=1).astype(BF)

    ts = [eye + o["lp"].astype(F32) for o in ops]
    lps = [_dot(o["lp"], o["lp"]).astype(BF) for o in ops]
    span = 2
    while 2 * span < C:
        rs = [_dot(lp, jnp.concatenate([t.astype(BF), lp], axis=1)) for t, lp in zip(ts, lps)]
        ts = [t + r[:, :n2] for t, r in zip(ts, rs)]
        lps = [r[:, n2:].astype(BF) for r in rs]
        span *= 2
    ts = [t + _dot(lp, t.astype(BF)) for t, lp in zip(ts, lps)]

    lakvs = [_dot(o["l_ak"], o["v_s"]) for o in ops]
    wus = [_dot(t.astype(BF), jnp.concatenate([o["a_t"], lakv.astype(BF)], axis=1))
           for t, o, lakv in zip(ts, ops, lakvs)]
    zero = jnp.zeros((n2, n2), BF)
    outs = []
    for o, wu in zip(ops, wus):
        rhs = jnp.concatenate([wu.astype(BF), jnp.concatenate([zero, o["v_s"]], axis=1)], axis=0)
        qy = _dot(o["a_r"], rhs)
        mn = _dot_tn(o["bk_h"], rhs)
        q = (o["r_t"] + qy[:, :n2]).astype(BF)
        y0 = qy[:C, n2:] + qy[C:, n2:]
        outs.append((q, y0, mn[:, :n2].astype(BF), mn[:, n2:].T, o["e"]))
    return outs


def _wkv_kernel(*refs, T, has_s0, out_state):
    (r_ref, k_ref, v_ref, lora_ref, lg_ref, w0_ref, wl_ref, a0_ref, wa_ref, wg_ref,
     kkw_ref, ka_ref, rk_ref, lnw_ref, lnb_ref) = refs[:15]
    pos = 15
    if has_s0:
        s0_ref = refs[pos]
        pos += 1
    y_ref = refs[pos]
    pos += 1
    if out_state:
        so_ref = refs[pos]
        pos += 1
    lw_s, b_s, kd_s, kk_s, y_s, q_s, m_s, n_s, e_s = refs[pos:]

    C = WKV_CHUNK
    nc = T // C
    lane = lax.broadcasted_iota(jnp.int32, (PAIR, PAIR), 1)
    row = lax.broadcasted_iota(jnp.int32, (PAIR, PAIR), 0)
    ones_blk = jnp.where((lane // HEAD_A) == (row // HEAD_A), 1.0, 0.0).astype(BF)
    eye = jnp.where(lane == row, 1.0, 0.0)
    lane_lo = lax.broadcasted_iota(jnp.int32, (C, PAIR), 1) < HEAD_A
    ci = lax.broadcasted_iota(jnp.int32, (C, C), 0)
    cj = lax.broadcasted_iota(jnp.int32, (C, C), 1)
    tri = (jnp.where(cj <= ci, 1.0, 0.0).astype(BF), jnp.where(cj >= ci, 1.0, 0.0).astype(BF))
    strict = (lane < row, lane > row)
    incl = (lane <= row, lane >= row)
    pairs = range(WKV_PAIRS)

    def cols(pp):
        return slice(pp * PAIR, (pp + 1) * PAIR)

    for pp in pairs:
        k = k_ref[:, cols(pp)]
        kk = k * kkw_ref[:, cols(pp)]
        kk = kk * lax.rsqrt(_head_sum(kk * kk, ones_blk) + 1e-12)
        kk_s[pp] = kk
        for d in range(2):
            lw_in = lora_ref[:, d * R_LORA:(d + 1) * R_LORA]
            la_in = lora_ref[:, (2 + d) * R_LORA:(3 + d) * R_LORA]
            z = -(w0_ref[d:d + 1, cols(pp)] + _dot(jnp.tanh(lw_in).astype(BF), wl_ref[d, :, cols(pp)].astype(BF)))
            softplus = jnp.maximum(z, 0.0) + jnp.log1p(jnp.exp(-jnp.abs(z)))
            lw_s[2 * pp + d] = -jnp.exp(-softplus - 0.5)
            a = jax.nn.sigmoid(a0_ref[d:d + 1, cols(pp)] + _dot(la_in.astype(BF), wa_ref[d, :, cols(pp)].astype(BF)))
            kd_s[2 * pp + d] = k * (1.0 + (a - 1.0) * ka_ref[:, cols(pp)])
            b_s[2 * pp + d] = kk * a

    def chunk_rows(idx):
        return pl.ds(idx * C, C) if isinstance(idx, int) else pl.ds(pl.multiple_of(idx * C, C), C)

    per_iter = WKV_UNROLL // 2

    def phase_a(pp, base):
        keys = [(d, base + t) for t in range(per_iter) for d in range(2)]
        items = []
        for d, idx in keys:
            sl = chunk_rows(idx)
            items.append(dict(d=d, r=r_ref[sl, cols(pp)], v=v_ref[sl, cols(pp)], kk=kk_s[pp, sl, :],
                              lw=lw_s[2 * pp + d, sl, :], b=b_s[2 * pp + d, sl, :], kd=kd_s[2 * pp + d, sl, :]))
        outs = _wkv_chunks_a(items, tri, strict, incl, lane_lo, eye)
        for (d, idx), (q, y0, mt, n, e) in zip(keys, outs):
            q_s[2 * pp + d, idx] = q
            y_s[2 * pp + d, chunk_rows(idx), :] = y0
            m_s[2 * pp + d, idx] = mt
            n_s[2 * pp + d, idx] = n
            e_s[2 * pp + d, idx] = jnp.broadcast_to(e, (8, PAIR))

    for pp in pairs:
        if nc == per_iter:
            phase_a(pp, 0)
        else:
            def body_a(j, carry, pp=pp):
                phase_a(pp, j * per_iter)
                return carry

            lax.fori_loop(0, nc // per_iter, body_a, 0)

    chains = [(pp, d) for pp in pairs for d in range(2)]

    def body_b(i, carry):
        nxt = []
        for (pp, d), s in zip(chains, carry):
            idx = i if d == 0 else nc - 1 - i
            sl = chunk_rows(idx)
            sb = s.astype(BF)
            yq = _dot_nt(q_s[2 * pp + d, idx], sb)
            y_s[2 * pp + d, sl, :] += yq[:C] + yq[C:]
            nxt.append(s * e_s[2 * pp + d, idx][0:1, :] + _dot_nt(sb, m_s[2 * pp + d, idx]) + n_s[2 * pp + d, idx])
        return tuple(nxt)

    if has_s0:
        init = tuple(s0_ref[d, pp] for pp, d in chains)
    else:
        init = tuple(jnp.zeros((PAIR, PAIR), F32) for _ in chains)
    s_fin = lax.fori_loop(0, nc, body_b, init)
    if out_state:
        for (pp, d), s in zip(chains, s_fin):
            so_ref[d, 2 * pp] = s[:HEAD_A, :HEAD_A]
            so_ref[d, 2 * pp + 1] = s[HEAD_A:, HEAD_A:]

    sig_lg = jax.nn.sigmoid(lg_ref[...]).astype(BF)
    inv_n = 1.0 / HEAD_A
    for pp in pairs:
        y = y_s[2 * pp] + y_s[2 * pp + 1]
        mu = _head_sum(y, ones_blk) * inv_n
        yc = y - mu
        var = _head_sum(yc * yc, ones_blk) * inv_n
        y = yc * lax.rsqrt(var + LN_X_EPS) * lnw_ref[:, cols(pp)] + lnb_ref[:, cols(pp)]
        rr = r_ref[:, cols(pp)] * rk_ref[:, cols(pp)]
        bonus = _head_sum(rr * (kd_s[2 * pp] + kd_s[2 * pp + 1]), ones_blk) * v_ref[:, cols(pp)]
        g = _dot(sig_lg, wg_ref[:, cols(pp)].astype(BF))
        y_ref[:, cols(pp)] = ((y + bonus) * g).astype(BF)


def _wkv(proj, row0, n_seq, T, wts, s0):
    (w0, wl, a0, wa, wg, kkw, ka, rk, lnw, lnb) = wts
    sb = row0 // T
    nc = T // WKV_CHUNK
    gw = WKV_PAIRS * PAIR
    cb = lambda col: col // gw
    vec = pl.BlockSpec((1, gw), lambda s, p: (0, p))
    in_specs = [pl.BlockSpec((T, gw), lambda s, p: (sb + s, cb(C_R) + p)),
                pl.BlockSpec((T, gw), lambda s, p: (sb + s, cb(C_K) + p)),
                pl.BlockSpec((T, gw), lambda s, p: (sb + s, cb(C_V) + p)),
                pl.BlockSpec((T, 4 * R_LORA), lambda s, p: (sb + s, C_LORA // (4 * R_LORA))),
                pl.BlockSpec((T, R_G_PAD), lambda s, p: (sb + s, C_LG // R_G_PAD)),
                pl.BlockSpec((2, gw), lambda s, p: (0, p)),
                pl.BlockSpec((2, R_LORA, gw), lambda s, p: (0, 0, p)),
                pl.BlockSpec((2, gw), lambda s, p: (0, p)),
                pl.BlockSpec((2, R_LORA, gw), lambda s, p: (0, 0, p)),
                pl.BlockSpec((R_G_PAD, gw), lambda s, p: (0, p)),
                vec, vec, vec, vec, vec]
    args = [proj, proj, proj, proj, proj, w0, wl, a0, wa, wg, kkw, ka, rk, lnw, lnb]
    has_s0 = s0 is not None
    if has_s0:
        in_specs.append(pl.BlockSpec((None, 2, WKV_PAIRS, PAIR, PAIR), lambda s, p: (s, 0, p, 0, 0)))
        args.append(s0)
    out_shape = [jax.ShapeDtypeStruct((n_seq * T, D_A), BF)]
    out_specs = [pl.BlockSpec((T, gw), lambda s, p: (s, p))]
    out_state = not has_s0
    if out_state:
        out_shape.append(jax.ShapeDtypeStruct((n_seq, 2, H_A, HEAD_A, HEAD_A), F32))
        out_specs.append(pl.BlockSpec((None, 2, 2 * WKV_PAIRS, HEAD_A, HEAD_A), lambda s, p: (s, 0, p, 0, 0)))
    nch = 2 * WKV_PAIRS
    return pl.pallas_call(
        functools.partial(_wkv_kernel, T=T, has_s0=has_s0, out_state=out_state),
        out_shape=out_shape,
        grid=(n_seq, N_PAIR // WKV_PAIRS),
        in_specs=in_specs,
        out_specs=out_specs,
        scratch_shapes=[pltpu.VMEM((nch, T, PAIR), F32), pltpu.VMEM((nch, T, PAIR), F32),
                        pltpu.VMEM((nch, T, PAIR), F32), pltpu.VMEM((WKV_PAIRS, T, PAIR), F32),
                        pltpu.VMEM((nch, T, PAIR), F32),
                        pltpu.VMEM((nch, nc, PAIR, PAIR), BF), pltpu.VMEM((nch, nc, PAIR, PAIR), BF),
                        pltpu.VMEM((nch, nc, PAIR, PAIR), F32), pltpu.VMEM((nch, nc, 8, PAIR), F32)],
        compiler_params=_cparams(("parallel", "arbitrary")),
    )(*args)


W_IN_BLK = 512
W_IN_B0 = C_QLAT // W_IN_BLK
W_IN_C0 = C_GLA // W_IN_BLK
W_IN_SHIFT = R_G_PAD - R_G


def _w_in_relayout_kernel(main_ref, side_ref, o_ref):
    j = pl.program_id(1)
    col = lax.broadcasted_iota(jnp.int32, o_ref.shape, 1)
    sh = W_IN_SHIFT

    @pl.when(j < W_IN_B0)
    def _():
        valid = jnp.where(j == W_IN_B0 - 1, R_G, W_IN_BLK)
        o_ref[...] = jnp.where(col < valid, main_ref[...], 0.0).astype(BF)

    @pl.when((j >= W_IN_B0) & (j < W_IN_C0))
    def _():
        valid = jnp.where(j == W_IN_C0 - 1, ROPE_DIM, W_IN_BLK)
        x = jnp.concatenate([side_ref[:, LANES - sh:], main_ref[:, :W_IN_BLK - sh]], axis=1)
        o_ref[...] = jnp.where(col < valid, x, 0.0).astype(BF)

    @pl.when(j >= W_IN_C0)
    def _():
        x = jnp.concatenate([main_ref[:, sh:], side_ref[:, :sh]], axis=1)
        o_ref[...] = x.astype(BF)


def _w_in_relayout(w_in):
    tk = 1024
    per = W_IN_BLK // LANES

    def main_idx(i, j):
        return i, jnp.where(j >= W_IN_C0, j - 1, j)

    def side_idx(i, j):
        return i, jnp.where(j >= W_IN_C0, per * j, per * jnp.maximum(j, W_IN_B0) - 1)

    return pl.pallas_call(
        _w_in_relayout_kernel,
        out_shape=jax.ShapeDtypeStruct((D_MODEL, D_IN_PAD), BF),
        grid=(D_MODEL // tk, D_IN_PAD // W_IN_BLK),
        in_specs=[pl.BlockSpec((tk, W_IN_BLK), main_idx), pl.BlockSpec((tk, LANES), side_idx)],
        out_specs=pl.BlockSpec((tk, W_IN_BLK), lambda i, j: (i, j)),
        compiler_params=_cparams(("parallel", "arbitrary")),
    )(w_in, w_in)


def _cast_pad_kernel(x_ref, o_ref, *, axis, n_real):
    is_pad = pl.program_id(axis) % (n_real + 1) == n_real

    @pl.when(is_pad)
    def _():
        o_ref[...] = jnp.zeros_like(o_ref)

    @pl.when(jnp.logical_not(is_pad))
    def _():
        o_ref[...] = x_ref[...].astype(BF)


def _cast_pad_cols(w, blk, n_real, n_seg):
    rows = w.shape[0]
    tr = 2048

    def in_idx(i, j):
        return i, (j // (n_real + 1)) * n_real + jnp.minimum(j % (n_real + 1), n_real - 1)

    return pl.pallas_call(
        functools.partial(_cast_pad_kernel, axis=1, n_real=n_real),
        out_shape=jax.ShapeDtypeStruct((rows, n_seg * (n_real + 1) * blk), BF),
        grid=(rows // tr, n_seg * (n_real + 1)),
        in_specs=[pl.BlockSpec((tr, blk), in_idx)],
        out_specs=pl.BlockSpec((tr, blk), lambda i, j: (i, j)),
        compiler_params=_cparams(("parallel", "arbitrary")),
    )(w)


def _cast_pad_rows(w, blk, n_real):
    cols = w.shape[1]
    return pl.pallas_call(
        functools.partial(_cast_pad_kernel, axis=0, n_real=n_real),
        out_shape=jax.ShapeDtypeStruct(((n_real + 1) * blk, cols), BF),
        grid=(n_real + 1,),
        in_specs=[pl.BlockSpec((blk, cols), lambda i: (jnp.minimum(i, n_real - 1), 0))],
        out_specs=pl.BlockSpec((blk, cols), lambda i: (i, 0)),
        compiler_params=_cparams(("arbitrary",)),
    )(w)


def _rope_tables():
    rows = DEC_SEQ // GRID_W
    row = jnp.repeat(jnp.arange(rows), GRID_W).astype(F32)
    col = jnp.tile(jnp.arange(GRID_W), rows).astype(F32)
    n_freq = ROPE_DIM // 4
    inv = ROPE_THETA ** (-jnp.arange(n_freq, dtype=F32) / n_freq)
    ang = jnp.concatenate([row[:, None] * inv, col[:, None] * inv], axis=-1)
    cos, sin = jnp.cos(ang), jnp.sin(ang)
    pad = jnp.zeros((DEC_SEQ, LANES - ROPE_DIM), F32)
    cos_t = jnp.concatenate([cos, cos, pad], axis=-1)
    sin_t = jnp.concatenate([-sin, sin, pad], axis=-1)
    ident = 256
    cos_t = jnp.concatenate([cos_t, jnp.ones((ident, LANES), F32)], axis=0)
    sin_t = jnp.concatenate([sin_t, jnp.zeros((ident, LANES), F32)], axis=0)
    return cos_t, sin_t


def _pair_state_in(state):
    b = state.shape[0]
    s = state.reshape(b, 2, N_PAIR, 2, HEAD_A, HEAD_A)
    z = jnp.zeros_like(s[:, :, :, 0])
    top = jnp.concatenate([s[:, :, :, 0], z], axis=-1)
    bot = jnp.concatenate([z, s[:, :, :, 1]], axis=-1)
    return jnp.concatenate([top, bot], axis=-2)


def kernel(x_prompt, x_sample, c, state_rwkv, cache_mla_ckv, cache_mla_kpe, c_ctx, w_mod, b_mod, norm_mix_w, w_in, rwkv_w0, rwkv_w_lora_b, rwkv_a0, rwkv_a_lora_b, rwkv_g_lora_b, rwkv_k_k, rwkv_k_a, rwkv_r_k, rwkv_ln_w, rwkv_ln_b, mla_q_norm_w, mla_w_uq, mla_kv_norm_w, mla_w_ukv, mla_q_head_norm, mla_k_head_norm, w_o_rwkv, w_o_mla, w_out, norm_ffn_w, w_ffn_in, w_ffn_out):
    l = 0
    xp = x_prompt.reshape(N_PROMPT, D_MODEL)
    xs = x_sample.reshape(N_SAMPLE, D_MODEL)

    w_in_p = _w_in_relayout(w_in[l])
    w_uq_p = jnp.pad(mla_w_uq[l].reshape(Q_LORA, H_B, QK_DIM), ((0, 0), (0, 0), (0, QK_PAD - QK_DIM)))
    w_uq_p = w_uq_p.reshape(Q_LORA, H_B * QK_PAD).astype(BF)
    q_head_w_p = jnp.pad(mla_q_head_norm[l], (0, QK_PAD - QK_DIM)).reshape(1, QK_PAD)
    wkv3 = mla_w_ukv[l].reshape(KV_LORA, H_B, NOPE_DIM + V_DIM)
    w_ukv_p = jnp.concatenate([wkv3[:, :, :NOPE_DIM].reshape(KV_LORA, H_B * NOPE_DIM),
                               wkv3[:, :, NOPE_DIM:].reshape(KV_LORA, H_B * V_DIM)], axis=1).astype(BF)
    wkn = mla_k_head_norm[l, :NOPE_DIM].reshape(1, NOPE_DIM)
    wkr = jnp.pad(mla_k_head_norm[l, NOPE_DIM:], (0, LANES - ROPE_DIM)).reshape(1, LANES)
    w_ffn_in_p = _cast_pad_cols(w_ffn_in[l], FF_BLK, D_FF // FF_BLK, 2)
    w_ffn_out_p = _cast_pad_rows(w_ffn_out[l], FF_BLK, D_FF // FF_BLK)
    w_oa = w_o_rwkv[l].astype(BF)
    w_ob = w_o_mla[l].astype(BF)
    w_out_b = w_out[l].astype(BF)
    wg_p = jnp.pad(rwkv_g_lora_b[l], ((0, R_G_PAD - R_G), (0, 0)))
    row = lambda a: a.reshape(1, D_A)
    wkv_w = (rwkv_w0[l], rwkv_w_lora_b[l], rwkv_a0[l], rwkv_a_lora_b[l], wg_p, row(rwkv_k_k[l]), row(rwkv_k_a[l]),
             row(rwkv_r_k[l]), row(rwkv_ln_w[l]), row(rwkv_ln_b[l]))
    cos_t, sin_t = _rope_tables()

    cond8 = jnp.concatenate([c_ctx[None, :], c, jnp.zeros((8 - 1 - DEC_BATCH, D_MODEL), F32)], axis=0)
    mod = _modulation(cond8, w_mod[l], b_mod[l]).reshape(8, 6, D_MODEL)

    proj = _in_proj(xp, xs, norm_mix_w[l], mod, w_in_p)

    ya_p, s_new = _wkv(proj, 0, BATCH, SEQ, wkv_w, None)
    ya_s = _wkv(proj, N_PROMPT, DEC_BATCH, DEC_SEQ, wkv_w, _pair_state_in(state_rwkv[:, l]))[0]

    q = _q_proj(proj, mla_q_norm_w[l], w_uq_p, q_head_w_p, cos_t, sin_t)
    k, v, ckv = _kv_proj(proj, C_KVLAT, proj, C_KPE, N_TOK, mla_kv_norm_w[l], w_ukv_p, wkn, wkr, cos_t, sin_t,
                         _rope_block, True)
    ctx_ckv = cache_mla_ckv[:, l].reshape(DEC_BATCH * PAST_LEN, KV_LORA)
    ctx_kpe = jnp.pad(cache_mla_kpe[:, l].reshape(DEC_BATCH * PAST_LEN, ROPE_DIM), ((0, 0), (0, LANES - ROPE_DIM)))
    k_ctx, v_ctx = _kv_proj(ctx_ckv, 0, ctx_kpe, 0, DEC_BATCH * PAST_LEN, mla_kv_norm_w[l], w_ukv_p, wkn, wkr,
                            cos_t, sin_t, lambda i, tm: DEC_SEQ // tm, False)
    yb_p = _attn_prompt(q, k, v)
    yb_s = _attn_sample(q, k, v, k_ctx, v_ctx)

    mixed = _merge((ya_p, ya_s), (yb_p, yb_s), w_oa, w_ob, proj)
    x1 = _resid_proj(mixed, w_out_b, (xp, xs), mod, 2, 2048, 0, N_TOK)
    act = _ffn_in(x1, norm_ffn_w[l], mod, w_ffn_in_p)
    y_p = _resid_proj(act, w_ffn_out_p, (x1,), mod, 5, 1408, 0, N_PROMPT).reshape(BATCH, SEQ, D_MODEL)
    y_s = _resid_proj(act, w_ffn_out_p, (x1,), mod, 5, 1408, N_PROMPT, N_SAMPLE).reshape(DEC_BATCH, DEC_SEQ, D_MODEL)
    new_state = s_new[:, None]
    new_ckv = ckv[:N_PROMPT].reshape(BATCH, 1, SEQ, KV_LORA)
    new_kpe = proj[:N_PROMPT, C_KPE:C_KPE + ROPE_DIM].reshape(BATCH, 1, SEQ, ROPE_DIM)
    return y_p, y_s, new_state, new_ckv, new_kpe
```

```python
import functools

import numpy as np
import jax
import jax.numpy as jnp
from jax import lax
from jax.experimental import pallas as pl
from jax.experimental.pallas import tpu as pltpu

F32 = jnp.float32
BF = jnp.bfloat16

D_MODEL = 4096
BATCH, SEQ = 32, 256
DEC_BATCH, DEC_SEQ, PAST_LEN = 4, 1024, 512
GRID_W = 64
D_A, HEAD_A = 2048, 64
H_A = D_A // HEAD_A
R_LORA, R_G = 128, 480
LN_X_EPS = 64e-5
H_B, Q_LORA, KV_LORA = 16, 1024, 512
NOPE_DIM, ROPE_DIM, V_DIM = 128, 64, 128
QK_DIM = NOPE_DIM + ROPE_DIM
ROPE_THETA = 10000.0
D_FF = 11008
NORM_EPS = 1e-6

N_PROMPT = BATCH * SEQ
N_SAMPLE = DEC_BATCH * DEC_SEQ
N_TOK = N_PROMPT + N_SAMPLE

LANES = 128
R_G_PAD = 512
QK_PAD = 256
D_FF_PAD = 11264
FF_BLK = D_FF_PAD - D_FF
C_R, C_K, C_V = 0, 2048, 4096
C_LORA, C_LG, C_QLAT, C_KVLAT, C_KPE = 6144, 6656, 7168, 8192, 8704
C_GLA, C_GLB, D_IN_PAD = 9216, 13312, 17408
WKV_CHUNK = 64
WKV_UNROLL = 8
WKV_PAIRS = 2
PAIR = 2 * HEAD_A
N_PAIR = D_A // PAIR

VMEM_LIMIT = 56 * 2**20


def _cparams(sem):
    return pltpu.CompilerParams(dimension_semantics=sem, vmem_limit_bytes=VMEM_LIMIT)


def _dot(a, b):
    return jnp.dot(a, b, preferred_element_type=F32)


def _dot_nt(a, b):
    return lax.dot_general(a, b, (((1,), (1,)), ((), ())), preferred_element_type=F32)


def _dot_tn(a, b):
    return lax.dot_general(a, b, (((0,), (0,)), ((), ())), preferred_element_type=F32)


def _mod_row(i, tm):
    n_prompt_tiles = N_PROMPT // tm
    tiles_per_seq = DEC_SEQ // tm
    return jnp.where(i < n_prompt_tiles, 0, 1 + (i - n_prompt_tiles) // tiles_per_seq)


def _mod_kernel(c_ref, w_ref, b_ref, o_ref):
    c = c_ref[...]
    s = c * jax.nn.sigmoid(c)
    o_ref[...] = _dot(s.astype(BF), w_ref[...].astype(BF)) + b_ref[...]


def _modulation(cond8, w_mod, b_mod):
    tn = 512
    n = w_mod.shape[1]
    return pl.pallas_call(
        _mod_kernel,
        out_shape=jax.ShapeDtypeStruct((8, n), F32),
        grid=(n // tn,),
        in_specs=[pl.BlockSpec((8, D_MODEL), lambda j: (0, 0)),
                  pl.BlockSpec((D_MODEL, tn), lambda j: (0, j)),
                  pl.BlockSpec((1, tn), lambda j: (0, j))],
        out_specs=pl.BlockSpec((8, tn), lambda j: (0, j)),
        compiler_params=_cparams(("arbitrary",)),
    )(cond8, w_mod, b_mod.reshape(1, n))


NORM_ROWS = 16


def _normmod(x_ref, nw_ref, mod_ref, o_ref, shift_idx, scale_idx):
    gain = nw_ref[...] * (1.0 + mod_ref[scale_idx:scale_idx + 1, :])
    shift = mod_ref[shift_idx:shift_idx + 1, :]

    def body(r, carry):
        rows = pl.ds(pl.multiple_of(r * NORM_ROWS, NORM_ROWS), NORM_ROWS)
        x = x_ref[rows, :]
        inv = lax.rsqrt(jnp.mean(x * x, axis=-1, keepdims=True) + NORM_EPS)
        o_ref[rows, :] = (x * inv * gain + shift).astype(BF)
        return carry

    lax.fori_loop(0, x_ref.shape[0] // NORM_ROWS, body, 0)


def _split_specs(tm, width, col):
    npt = N_PROMPT // tm
    return [pl.BlockSpec((tm, width), lambda i, *_: (jnp.minimum(i, npt - 1), col(*_))),
            pl.BlockSpec((tm, width), lambda i, *_: (jnp.maximum(i - npt, 0), col(*_)))]


def _norm_mod_kernel(*refs, shift_idx, scale_idx, n_prompt_tiles):
    *x_refs, nw_ref, mod_ref, o_ref = refs

    def run(x_ref):
        _normmod(x_ref, nw_ref, mod_ref, o_ref, shift_idx, scale_idx)

    if len(x_refs) == 1:
        run(x_refs[0])
    else:
        is_prompt = pl.program_id(0) < n_prompt_tiles
        pl.when(is_prompt)(lambda: run(x_refs[0]))
        pl.when(jnp.logical_not(is_prompt))(lambda: run(x_refs[1]))


def _norm_mod(xs, norm_w, mod, shift_idx, scale_idx):
    tm = 256
    if len(xs) == 1:
        x_specs = [pl.BlockSpec((tm, D_MODEL), lambda i: (i, 0))]
    else:
        x_specs = _split_specs(tm, D_MODEL, lambda: 0)
    return pl.pallas_call(
        functools.partial(_norm_mod_kernel, shift_idx=shift_idx, scale_idx=scale_idx, n_prompt_tiles=N_PROMPT // tm),
        out_shape=jax.ShapeDtypeStruct((N_TOK, D_MODEL), BF),
        grid=(N_TOK // tm,),
        in_specs=x_specs + [pl.BlockSpec((1, D_MODEL), lambda i: (0, 0)),
                            pl.BlockSpec((None, 6, D_MODEL), lambda i: (_mod_row(i, tm), 0, 0))],
        out_specs=pl.BlockSpec((tm, D_MODEL), lambda i: (i, 0)),
        compiler_params=_cparams(("parallel",)),
    )(*xs, norm_w.reshape(1, D_MODEL), mod)


def _in_proj_kernel(h_ref, wt_ref, o_ref):
    o_ref[...] = _dot_nt(h_ref[...], wt_ref[...])


def _in_proj(h, w_in_t):
    tm, tn = 1024, 1024
    return pl.pallas_call(
        _in_proj_kernel,
        out_shape=jax.ShapeDtypeStruct((N_TOK, D_IN_PAD), F32),
        grid=(N_TOK // tm, D_IN_PAD // tn),
        in_specs=[pl.BlockSpec((tm, D_MODEL), lambda i, j: (i, 0)),
                  pl.BlockSpec((tn, D_MODEL), lambda i, j: (j, 0))],
        out_specs=pl.BlockSpec((tm, tn), lambda i, j: (i, j)),
        compiler_params=_cparams(("parallel", "arbitrary")),
    )(h, w_in_t)


def _ffn_in_kernel(h_ref, wg_ref, wu_ref, o_ref):
    h = h_ref[...]
    g = _dot(h, wg_ref[...])
    u = _dot(h, wu_ref[...])
    o_ref[...] = (g * jax.nn.sigmoid(g) * u).astype(BF)


def _ffn_in(h, w_ffn_in_p):
    tm, tn = 1024, 512
    nj = D_FF_PAD // tn
    return pl.pallas_call(
        _ffn_in_kernel,
        out_shape=jax.ShapeDtypeStruct((N_TOK, D_FF_PAD), BF),
        grid=(N_TOK // tm, nj),
        in_specs=[pl.BlockSpec((tm, D_MODEL), lambda i, j: (i, 0)),
                  pl.BlockSpec((D_MODEL, tn), lambda i, j: (0, j)),
                  pl.BlockSpec((D_MODEL, tn), lambda i, j: (0, j + nj))],
        out_specs=pl.BlockSpec((tm, tn), lambda i, j: (i, j)),
        compiler_params=_cparams(("parallel", "arbitrary")),
    )(h, w_ffn_in_p, w_ffn_in_p)


def _resid_proj_kernel(a_ref, w_ref, *refs, gate_idx, n_prompt_tiles, nk):
    if nk == 1:
        *x_refs, mod_ref, o_ref = refs
    else:
        *x_refs, mod_ref, o_ref, acc_ref = refs
    part = _dot(a_ref[...], w_ref[...])
    last = True
    if nk > 1:
        k = pl.program_id(2)
        last = k == nk - 1

        @pl.when(k == 0)
        def _():
            acc_ref[...] = part

        @pl.when((k > 0) & jnp.logical_not(last))
        def _():
            acc_ref[...] += part

    def finish(x_ref):
        total = part if nk == 1 else acc_ref[...] + part
        o_ref[...] = x_ref[...] + mod_ref[gate_idx:gate_idx + 1, :] * total

    if len(x_refs) == 1:
        pl.when(jnp.asarray(last))(lambda: finish(x_refs[0]))
    else:
        is_prompt = pl.program_id(0) < n_prompt_tiles
        pl.when(last & is_prompt)(lambda: finish(x_refs[0]))
        pl.when(last & jnp.logical_not(is_prompt))(lambda: finish(x_refs[1]))


def _resid_proj(a, w, xs, mod, gate_idx, tn, tk, row0, n_rows):
    tm = 1024
    kdim = a.shape[1]
    nk = kdim // tk
    r0 = row0 // tm
    if len(xs) == 1:
        x_specs = [pl.BlockSpec((tm, tn), lambda i, j, k: (r0 + i, j))]
    else:
        x_specs = _split_specs(tm, tn, lambda j, k: j)
    return pl.pallas_call(
        functools.partial(_resid_proj_kernel, gate_idx=gate_idx, n_prompt_tiles=N_PROMPT // tm, nk=nk),
        out_shape=jax.ShapeDtypeStruct((n_rows, D_MODEL), F32),
        grid=(n_rows // tm, D_MODEL // tn, nk),
        in_specs=[pl.BlockSpec((tm, tk), lambda i, j, k: (r0 + i, k)),
                  pl.BlockSpec((tk, tn), lambda i, j, k: (k, j))] + x_specs + [
                  pl.BlockSpec((None, 6, tn), lambda i, j, k: (_mod_row(r0 + i, tm), 0, j))],
        out_specs=pl.BlockSpec((tm, tn), lambda i, j, k: (i, j)),
        scratch_shapes=[pltpu.VMEM((tm, tn), F32)] if nk > 1 else [],
        compiler_params=_cparams(("parallel", "parallel", "arbitrary")),
    )(a, w, *xs, mod)


def _merge_kernel(yap_ref, yas_ref, ybp_ref, ybs_ref, wa_ref, wb_ref, gla_ref, glb_ref, o_ref, *, n_prompt_tiles):
    is_prompt = pl.program_id(0) < n_prompt_tiles
    ya = jnp.where(is_prompt, yap_ref[...], yas_ref[...])
    yb = jnp.where(is_prompt, ybp_ref[...], ybs_ref[...])
    pa = _dot(ya, wa_ref[...])
    pb = _dot(yb, wb_ref[...])
    o_ref[...] = (jax.nn.sigmoid(gla_ref[...]) * pa + jax.nn.sigmoid(glb_ref[...]) * pb).astype(BF)


def _merge(y_a, y_b, w_oa, w_ob, proj):
    tm, tn = 512, 1024
    return pl.pallas_call(
        functools.partial(_merge_kernel, n_prompt_tiles=N_PROMPT // tm),
        out_shape=jax.ShapeDtypeStruct((N_TOK, D_MODEL), BF),
        grid=(N_TOK // tm, D_MODEL // tn),
        in_specs=_split_specs(tm, D_A, lambda j: 0) + _split_specs(tm, D_A, lambda j: 0) + [
                  pl.BlockSpec((D_A, tn), lambda i, j: (0, j)),
                  pl.BlockSpec((D_A, tn), lambda i, j: (0, j)),
                  pl.BlockSpec((tm, tn), lambda i, j: (i, C_GLA // tn + j)),
                  pl.BlockSpec((tm, tn), lambda i, j: (i, C_GLB // tn + j))],
        out_specs=pl.BlockSpec((tm, tn), lambda i, j: (i, j)),
        compiler_params=_cparams(("parallel", "arbitrary")),
    )(*y_a, *y_b, w_oa, w_ob, proj, proj)


def _rope(x, cos_ref, sin_ref):
    swapped = pltpu.roll(x, ROPE_DIM // 2, axis=1) + pltpu.roll(x, LANES - ROPE_DIM // 2, axis=1)
    return x * cos_ref[...] + swapped * sin_ref[...]


def _rope_block(i, tm):
    n_prompt_tiles = N_PROMPT // tm
    tiles_per_seq = DEC_SEQ // tm
    return jnp.where(i < n_prompt_tiles, tiles_per_seq, (i - n_prompt_tiles) % tiles_per_seq)


def _q_kernel(ql_ref, nw_ref, w_ref, hw_ref, cos_ref, sin_ref, o_ref):
    x = ql_ref[...]
    y = x * lax.rsqrt(jnp.mean(x * x, axis=-1, keepdims=True) + NORM_EPS) * nw_ref[...]
    q = _dot(y.astype(BF), w_ref[...])
    hw = hw_ref[...]
    for h in range(H_B):
        qh = q[:, h * QK_PAD:(h + 1) * QK_PAD]
        inv = lax.rsqrt(jnp.sum(qh * qh, axis=-1, keepdims=True) * (1.0 / QK_DIM) + NORM_EPS)
        qn = qh * inv * hw
        o_ref[:, h * QK_PAD:h * QK_PAD + NOPE_DIM] = qn[:, :NOPE_DIM].astype(BF)
        o_ref[:, h * QK_PAD + NOPE_DIM:(h + 1) * QK_PAD] = _rope(qn[:, NOPE_DIM:], cos_ref, sin_ref).astype(BF)


def _q_proj(proj, q_norm_w, w_uq_p, q_head_w_p, cos_t, sin_t):
    tm = 256
    return pl.pallas_call(
        _q_kernel,
        out_shape=jax.ShapeDtypeStruct((N_TOK, H_B * QK_PAD), BF),
        grid=(N_TOK // tm,),
        in_specs=[pl.BlockSpec((tm, Q_LORA), lambda i: (i, C_QLAT // Q_LORA)),
                  pl.BlockSpec((1, Q_LORA), lambda i: (0, 0)),
                  pl.BlockSpec((Q_LORA, H_B * QK_PAD), lambda i: (0, 0)),
                  pl.BlockSpec((1, QK_PAD), lambda i: (0, 0)),
                  pl.BlockSpec((tm, LANES), lambda i: (_rope_block(i, tm), 0)),
                  pl.BlockSpec((tm, LANES), lambda i: (_rope_block(i, tm), 0))],
        out_specs=pl.BlockSpec((tm, H_B * QK_PAD), lambda i: (i, 0)),
        compiler_params=_cparams(("parallel",)),
    )(proj, q_norm_w.reshape(1, Q_LORA), w_uq_p, q_head_w_p, cos_t, sin_t)


def _kv_kernel(lat_ref, kpe_ref, nw_ref, w_ref, wkn_ref, wkr_ref, cos_ref, sin_ref, *out_refs, pre_norm):
    if pre_norm:
        k_ref, v_ref, ckv_ref = out_refs
        x = lat_ref[...]
        ckv = x * lax.rsqrt(jnp.mean(x * x, axis=-1, keepdims=True) + NORM_EPS) * nw_ref[...]
        ckv_ref[...] = ckv
    else:
        k_ref, v_ref = out_refs
        ckv = lat_ref[...]
    kv = _dot(ckv.astype(BF), w_ref[...])
    v_ref[...] = kv[:, H_B * NOPE_DIM:].astype(BF)
    kpe = kpe_ref[...]
    kpe_ss = jnp.sum(kpe * kpe, axis=-1, keepdims=True)
    wkn, wkr = wkn_ref[...], wkr_ref[...]
    for h in range(H_B):
        kn = kv[:, h * NOPE_DIM:(h + 1) * NOPE_DIM]
        inv = lax.rsqrt((jnp.sum(kn * kn, axis=-1, keepdims=True) + kpe_ss) * (1.0 / QK_DIM) + NORM_EPS)
        k_ref[:, h * QK_PAD:h * QK_PAD + NOPE_DIM] = (kn * inv * wkn).astype(BF)
        k_ref[:, h * QK_PAD + NOPE_DIM:(h + 1) * QK_PAD] = _rope(kpe * inv * wkr, cos_ref, sin_ref).astype(BF)


def _kv_proj(lat, lat_col, kpe, kpe_col, n_rows, kv_norm_w, w_ukv_p, wkn, wkr, cos_t, sin_t, rope_block, pre_norm):
    tm = 256
    out_shape = [jax.ShapeDtypeStruct((n_rows, H_B * QK_PAD), BF),
                 jax.ShapeDtypeStruct((n_rows, H_B * V_DIM), BF)]
    out_specs = [pl.BlockSpec((tm, H_B * QK_PAD), lambda i: (i, 0)),
                 pl.BlockSpec((tm, H_B * V_DIM), lambda i: (i, 0))]
    if pre_norm:
        out_shape.append(jax.ShapeDtypeStruct((n_rows, KV_LORA), F32))
        out_specs.append(pl.BlockSpec((tm, KV_LORA), lambda i: (i, 0)))
    return pl.pallas_call(
        functools.partial(_kv_kernel, pre_norm=pre_norm),
        out_shape=out_shape,
        grid=(n_rows // tm,),
        in_specs=[pl.BlockSpec((tm, KV_LORA), lambda i: (i, lat_col // KV_LORA)),
                  pl.BlockSpec((tm, LANES), lambda i: (i, kpe_col // LANES)),
                  pl.BlockSpec((1, KV_LORA), lambda i: (0, 0)),
                  pl.BlockSpec((KV_LORA, H_B * (NOPE_DIM + V_DIM)), lambda i: (0, 0)),
                  pl.BlockSpec((1, NOPE_DIM), lambda i: (0, 0)),
                  pl.BlockSpec((1, LANES), lambda i: (0, 0)),
                  pl.BlockSpec((tm, LANES), lambda i: (rope_block(i, tm), 0)),
                  pl.BlockSpec((tm, LANES), lambda i: (rope_block(i, tm), 0))],
        out_specs=out_specs,
        compiler_params=_cparams(("parallel",)),
    )(lat, kpe, kv_norm_w.reshape(1, KV_LORA), w_ukv_p, wkn, wkr, cos_t, sin_t)


_ATTN_SCALE = QK_DIM ** -0.5


def _attn_prompt_kernel(q_ref, k_ref, v_ref, o_ref, *, heads):
    for h in range(heads):
        q = q_ref[:, h * QK_PAD:(h + 1) * QK_PAD]
        k = k_ref[:, h * QK_PAD:(h + 1) * QK_PAD]
        s = _dot_nt(q, k) * _ATTN_SCALE
        e = jnp.exp(s - jnp.max(s, axis=-1, keepdims=True))
        p = e * (1.0 / jnp.sum(e, axis=-1, keepdims=True))
        o_ref[:, h * V_DIM:(h + 1) * V_DIM] = _dot(p.astype(BF), v_ref[:, h * V_DIM:(h + 1) * V_DIM]).astype(BF)


def _attn_prompt(q, k, v):
    heads = 4
    return pl.pallas_call(
        functools.partial(_attn_prompt_kernel, heads=heads),
        out_shape=jax.ShapeDtypeStruct((N_PROMPT, H_B * V_DIM), BF),
        grid=(BATCH, H_B // heads),
        in_specs=[pl.BlockSpec((SEQ, heads * QK_PAD), lambda b, g: (b, g)),
                  pl.BlockSpec((SEQ, heads * QK_PAD), lambda b, g: (b, g)),
                  pl.BlockSpec((SEQ, heads * V_DIM), lambda b, g: (b, g))],
        out_specs=pl.BlockSpec((SEQ, heads * V_DIM), lambda b, g: (b, g)),
        compiler_params=_cparams(("parallel", "parallel")),
    )(q, k, v)


def _attn_sample_kernel(q_ref, kc_ref, vc_ref, ks_ref, vs_ref, o_ref, *, heads):
    for h in range(heads):
        qk = slice(h * QK_PAD, (h + 1) * QK_PAD)
        vv = slice(h * V_DIM, (h + 1) * V_DIM)
        q = q_ref[:, qk]
        s1 = _dot_nt(q, kc_ref[:, qk]) * _ATTN_SCALE
        s2 = _dot_nt(q, ks_ref[:, qk]) * _ATTN_SCALE
        m = jnp.maximum(jnp.max(s1, axis=-1, keepdims=True), jnp.max(s2, axis=-1, keepdims=True))
        e1 = jnp.exp(s1 - m)
        e2 = jnp.exp(s2 - m)
        inv = 1.0 / (jnp.sum(e1, axis=-1, keepdims=True) + jnp.sum(e2, axis=-1, keepdims=True))
        o = _dot((e1 * inv).astype(BF), vc_ref[:, vv]) + _dot((e2 * inv).astype(BF), vs_ref[:, vv])
        o_ref[:, vv] = o.astype(BF)


def _attn_sample(q, k, v, k_ctx, v_ctx):
    tq = 512
    heads = 2
    qt = DEC_SEQ // tq
    q0 = N_PROMPT // tq
    s0 = N_PROMPT // DEC_SEQ
    return pl.pallas_call(
        functools.partial(_attn_sample_kernel, heads=heads),
        out_shape=jax.ShapeDtypeStruct((N_SAMPLE, H_B * V_DIM), BF),
        grid=(DEC_BATCH, H_B // heads, qt),
        in_specs=[pl.BlockSpec((tq, heads * QK_PAD), lambda b, g, t: (q0 + b * qt + t, g)),
                  pl.BlockSpec((PAST_LEN, heads * QK_PAD), lambda b, g, t: (b, g)),
                  pl.BlockSpec((PAST_LEN, heads * V_DIM), lambda b, g, t: (b, g)),
                  pl.BlockSpec((DEC_SEQ, heads * QK_PAD), lambda b, g, t: (s0 + b, g)),
                  pl.BlockSpec((DEC_SEQ, heads * V_DIM), lambda b, g, t: (s0 + b, g))],
        out_specs=pl.BlockSpec((tq, heads * V_DIM), lambda b, g, t: (b * qt + t, g)),
        compiler_params=_cparams(("parallel", "parallel", "arbitrary")),
    )(q, k_ctx, v_ctx, k, v)


def _head_sum(x, ones_blk):
    hi = x.astype(BF)
    lo = (x - hi.astype(F32)).astype(BF)
    return _dot(hi, ones_blk) + _dot(lo, ones_blk)


def _wkv_chunks_a(items, tri, strict, incl, lane_lo, eye):
    C = WKV_CHUNK
    n2 = 2 * C

    def stack(x):
        return jnp.concatenate([jnp.where(lane_lo, x, 0.0), jnp.where(lane_lo, 0.0, x)], axis=0)

    def cumsum(it):
        lw = it["lw"]
        lw_hi = lw.astype(BF)
        lw_lo = (lw - lw_hi.astype(F32)).astype(BF)
        return _dot(tri[it["d"]], lw_hi) + _dot(tri[it["d"]], lw_lo)

    cs = [cumsum(it) for it in items]

    def operands(it, c):
        cl = c[C - 1:C, :] if it["d"] == 0 else c[0:1, :]
        e_inc = jnp.exp(c)
        e_neg = jnp.exp(-c)
        e_exc = jnp.exp(c - it["lw"])
        e_end = jnp.exp(cl - c)
        b, kd = it["b"], it["kd"]
        return dict(d=it["d"], e=jnp.exp(cl),
                    a_t=stack(-it["kk"] * e_exc).astype(BF), r_t=stack(it["r"] * e_inc),
                    b_t=stack(b * e_neg).astype(BF), k_t=stack(kd * e_neg).astype(BF),
                    bk_h=jnp.concatenate([stack(b * e_end), stack(kd * e_end)], axis=0).astype(BF),
                    v_s=stack(it["v"]).astype(BF))

    ops = [operands(it, c) for it, c in zip(items, cs)]
    scs = [_dot_nt(jnp.concatenate([o["a_t"], o["r_t"].astype(BF)], axis=0),
                   jnp.concatenate([o["b_t"], o["k_t"]], axis=0)) for o in ops]
    for o, sc in zip(ops, scs):
        d = o["d"]
        o["lp"] = jnp.where(strict[d], sc[:n2, :n2], 0.0).astype(BF)
        o["l_ak"] = jnp.where(strict[d], sc[:n2, n2:], 0.0).astype(BF)
        o["a_r"] = jnp.concatenate([jnp.where(incl[d], sc[n2:, :n2], 0.0),
                                    jnp.where(incl[d], sc[n2:, n2:], 0.0)], axis=1).astype(BF)

    ts = [eye + o["lp"].astype(F32) for o in ops]
    lps = [_dot(o["lp"], o["lp"]).astype(BF) for o in ops]
    span = 2
    while 2 * span < C:
        rs = [_dot(lp, jnp.concatenate([t.astype(BF), lp], axis=1)) for t, lp in zip(ts, lps)]
        ts = [t + r[:, :n2] for t, r in zip(ts, rs)]
        lps = [r[:, n2:].astype(BF) for r in rs]
        span *= 2
    ts = [t + _dot(lp, t.astype(BF)) for t, lp in zip(ts, lps)]

    lakvs = [_dot(o["l_ak"], o["v_s"]) for o in ops]
    wus = [_dot(t.astype(BF), jnp.concatenate([o["a_t"], lakv.astype(BF)], axis=1))
           for t, o, lakv in zip(ts, ops, lakvs)]
    zero = jnp.zeros((n2, n2), BF)
    outs = []
    for o, wu in zip(ops, wus):
        rhs = jnp.concatenate([wu.astype(BF), jnp.concatenate([zero, o["v_s"]], axis=1)], axis=0)
        qy = _dot(o["a_r"], rhs)
        mn = _dot_tn(o["bk_h"], rhs)
        q = (o["r_t"] + qy[:, :n2]).astype(BF)
        y0 = qy[:C, n2:] + qy[C:, n2:]
        outs.append((q, y0, mn[:, :n2].astype(BF), mn[:, n2:].T, o["e"]))
    return outs


def _wkv_kernel(*refs, T, has_s0, out_state):
    (r_ref, k_ref, v_ref, lora_ref, lg_ref, w0_ref, wl_ref, a0_ref, wa_ref, wg_ref,
     kkw_ref, ka_ref, rk_ref, lnw_ref, lnb_ref) = refs[:15]
    pos = 15
    if has_s0:
        s0_ref = refs[pos]
        pos += 1
    y_ref = refs[pos]
    pos += 1
    if out_state:
        so_ref = refs[pos]
        pos += 1
    lw_s, b_s, kd_s, kk_s, y_s, q_s, m_s, n_s, e_s = refs[pos:]

    C = WKV_CHUNK
    nc = T // C
    lane = lax.broadcasted_iota(jnp.int32, (PAIR, PAIR), 1)
    row = lax.broadcasted_iota(jnp.int32, (PAIR, PAIR), 0)
    ones_blk = jnp.where((lane // HEAD_A) == (row // HEAD_A), 1.0, 0.0).astype(BF)
    eye = jnp.where(lane == row, 1.0, 0.0)
    lane_lo = lax.broadcasted_iota(jnp.int32, (C, PAIR), 1) < HEAD_A
    ci = lax.broadcasted_iota(jnp.int32, (C, C), 0)
    cj = lax.broadcasted_iota(jnp.int32, (C, C), 1)
    tri = (jnp.where(cj <= ci, 1.0, 0.0).astype(BF), jnp.where(cj >= ci, 1.0, 0.0).astype(BF))
    strict = (lane < row, lane > row)
    incl = (lane <= row, lane >= row)
    pairs = range(WKV_PAIRS)

    def cols(pp):
        return slice(pp * PAIR, (pp + 1) * PAIR)

    for pp in pairs:
        k = k_ref[:, cols(pp)]
        kk = k * kkw_ref[:, cols(pp)]
        kk = kk * lax.rsqrt(_head_sum(kk * kk, ones_blk) + 1e-12)
        kk_s[pp] = kk
        for d in range(2):
            lw_in = lora_ref[:, d * R_LORA:(d + 1) * R_LORA]
            la_in = lora_ref[:, (2 + d) * R_LORA:(3 + d) * R_LORA]
            z = -(w0_ref[d:d + 1, cols(pp)] + _dot(jnp.tanh(lw_in).astype(BF), wl_ref[d, :, cols(pp)].astype(BF)))
            softplus = jnp.maximum(z, 0.0) + jnp.log1p(jnp.exp(-jnp.abs(z)))
            lw_s[2 * pp + d] = -jnp.exp(-softplus - 0.5)
            a = jax.nn.sigmoid(a0_ref[d:d + 1, cols(pp)] + _dot(la_in.astype(BF), wa_ref[d, :, cols(pp)].astype(BF)))
            kd_s[2 * pp + d] = k * (1.0 + (a - 1.0) * ka_ref[:, cols(pp)])
            b_s[2 * pp + d] = kk * a

    def chunk_rows(idx):
        return pl.ds(idx * C, C) if isinstance(idx, int) else pl.ds(pl.multiple_of(idx * C, C), C)

    per_iter = WKV_UNROLL // 2

    def phase_a(pp, base):
        keys = [(d, base + t) for t in range(per_iter) for d in range(2)]
        items = []
        for d, idx in keys:
            sl = chunk_rows(idx)
            items.append(dict(d=d, r=r_ref[sl, cols(pp)], v=v_ref[sl, cols(pp)], kk=kk_s[pp, sl, :],
                              lw=lw_s[2 * pp + d, sl, :], b=b_s[2 * pp + d, sl, :], kd=kd_s[2 * pp + d, sl, :]))
        outs = _wkv_chunks_a(items, tri, strict, incl, lane_lo, eye)
        for (d, idx), (q, y0, mt, n, e) in zip(keys, outs):
            q_s[2 * pp + d, idx] = q
            y_s[2 * pp + d, chunk_rows(idx), :] = y0
            m_s[2 * pp + d, idx] = mt
            n_s[2 * pp + d, idx] = n
            e_s[2 * pp + d, idx] = jnp.broadcast_to(e, (8, PAIR))

    for pp in pairs:
        if nc == per_iter:
            phase_a(pp, 0)
        else:
            def body_a(j, carry, pp=pp):
                phase_a(pp, j * per_iter)
                return carry

            lax.fori_loop(0, nc // per_iter, body_a, 0)

    chains = [(pp, d) for pp in pairs for d in range(2)]

    def body_b(i, carry):
        nxt = []
        for (pp, d), s in zip(chains, carry):
            idx = i if d == 0 else nc - 1 - i
            sl = chunk_rows(idx)
            sb = s.astype(BF)
            yq = _dot_nt(q_s[2 * pp + d, idx], sb)
            y_s[2 * pp + d, sl, :] += yq[:C] + yq[C:]
            nxt.append(s * e_s[2 * pp + d, idx][0:1, :] + _dot_nt(sb, m_s[2 * pp + d, idx]) + n_s[2 * pp + d, idx])
        return tuple(nxt)

    if has_s0:
        init = tuple(s0_ref[d, pp] for pp, d in chains)
    else:
        init = tuple(jnp.zeros((PAIR, PAIR), F32) for _ in chains)
    s_fin = lax.fori_loop(0, nc, body_b, init)
    if out_state:
        for (pp, d), s in zip(chains, s_fin):
            so_ref[d, 2 * pp] = s[:HEAD_A, :HEAD_A]
            so_ref[d, 2 * pp + 1] = s[HEAD_A:, HEAD_A:]

    sig_lg = jax.nn.sigmoid(lg_ref[...]).astype(BF)
    inv_n = 1.0 / HEAD_A
    for pp in pairs:
        y = y_s[2 * pp] + y_s[2 * pp + 1]
        mu = _head_sum(y, ones_blk) * inv_n
        yc = y - mu
        var = _head_sum(yc * yc, ones_blk) * inv_n
        y = yc * lax.rsqrt(var + LN_X_EPS) * lnw_ref[:, cols(pp)] + lnb_ref[:, cols(pp)]
        rr = r_ref[:, cols(pp)] * rk_ref[:, cols(pp)]
        bonus = _head_sum(rr * (kd_s[2 * pp] + kd_s[2 * pp + 1]), ones_blk) * v_ref[:, cols(pp)]
        g = _dot(sig_lg, wg_ref[:, cols(pp)].astype(BF))
        y_ref[:, cols(pp)] = ((y + bonus) * g).astype(BF)


def _wkv(proj, row0, n_seq, T, wts, s0):
    (w0, wl, a0, wa, wg, kkw, ka, rk, lnw, lnb) = wts
    sb = row0 // T
    nc = T // WKV_CHUNK
    gw = WKV_PAIRS * PAIR
    cb = lambda col: col // gw
    vec = pl.BlockSpec((1, gw), lambda s, p: (0, p))
    in_specs = [pl.BlockSpec((T, gw), lambda s, p: (sb + s, cb(C_R) + p)),
                pl.BlockSpec((T, gw), lambda s, p: (sb + s, cb(C_K) + p)),
                pl.BlockSpec((T, gw), lambda s, p: (sb + s, cb(C_V) + p)),
                pl.BlockSpec((T, 4 * R_LORA), lambda s, p: (sb + s, C_LORA // (4 * R_LORA))),
                pl.BlockSpec((T, R_G_PAD), lambda s, p: (sb + s, C_LG // R_G_PAD)),
                pl.BlockSpec((2, gw), lambda s, p: (0, p)),
                pl.BlockSpec((2, R_LORA, gw), lambda s, p: (0, 0, p)),
                pl.BlockSpec((2, gw), lambda s, p: (0, p)),
                pl.BlockSpec((2, R_LORA, gw), lambda s, p: (0, 0, p)),
                pl.BlockSpec((R_G_PAD, gw), lambda s, p: (0, p)),
                vec, vec, vec, vec, vec]
    args = [proj, proj, proj, proj, proj, w0, wl, a0, wa, wg, kkw, ka, rk, lnw, lnb]
    has_s0 = s0 is not None
    if has_s0:
        in_specs.append(pl.BlockSpec((None, 2, WKV_PAIRS, PAIR, PAIR), lambda s, p: (s, 0, p, 0, 0)))
        args.append(s0)
    out_shape = [jax.ShapeDtypeStruct((n_seq * T, D_A), BF)]
    out_specs = [pl.BlockSpec((T, gw), lambda s, p: (s, p))]
    out_state = not has_s0
    if out_state:
        out_shape.append(jax.ShapeDtypeStruct((n_seq, 2, H_A, HEAD_A, HEAD_A), F32))
        out_specs.append(pl.BlockSpec((None, 2, 2 * WKV_PAIRS, HEAD_A, HEAD_A), lambda s, p: (s, 0, p, 0, 0)))
    nch = 2 * WKV_PAIRS
    return pl.pallas_call(
        functools.partial(_wkv_kernel, T=T, has_s0=has_s0, out_state=out_state),
        out_shape=out_shape,
        grid=(n_seq, N_PAIR // WKV_PAIRS),
        in_specs=in_specs,
        out_specs=out_specs,
        scratch_shapes=[pltpu.VMEM((nch, T, PAIR), F32), pltpu.VMEM((nch, T, PAIR), F32),
                        pltpu.VMEM((nch, T, PAIR), F32), pltpu.VMEM((WKV_PAIRS, T, PAIR), F32),
                        pltpu.VMEM((nch, T, PAIR), F32),
                        pltpu.VMEM((nch, nc, PAIR, PAIR), BF), pltpu.VMEM((nch, nc, PAIR, PAIR), BF),
                        pltpu.VMEM((nch, nc, PAIR, PAIR), F32), pltpu.VMEM((nch, nc, 8, PAIR), F32)],
        compiler_params=_cparams(("parallel", "arbitrary")),
    )(*args)


W_IN_BLK = 512
W_IN_SIDE = 128
W_IN_B0 = C_QLAT // W_IN_BLK
W_IN_C0 = C_GLA // W_IN_BLK
W_IN_SHIFT = R_G_PAD - R_G


def _w_in_relayout_kernel(main_ref, side_ref, o_ref):
    j = pl.program_id(0)
    row = lax.broadcasted_iota(jnp.int32, o_ref.shape, 0)
    sh = W_IN_SHIFT

    @pl.when(j < W_IN_B0)
    def _():
        valid = jnp.where(j == W_IN_B0 - 1, R_G, W_IN_BLK)
        o_ref[...] = jnp.where(row < valid, main_ref[...], 0.0).astype(BF)

    @pl.when((j >= W_IN_B0) & (j < W_IN_C0))
    def _():
        valid = jnp.where(j == W_IN_C0 - 1, ROPE_DIM, W_IN_BLK)
        x = jnp.concatenate([side_ref[W_IN_SIDE - sh:, :], main_ref[:W_IN_BLK - sh, :]], axis=0)
        o_ref[...] = jnp.where(row < valid, x, 0.0).astype(BF)

    @pl.when(j >= W_IN_C0)
    def _():
        x = jnp.concatenate([main_ref[sh:, :], side_ref[:sh, :]], axis=0)
        o_ref[...] = x.astype(BF)


def _w_in_relayout(w_in_t):
    tc = 2048
    per = W_IN_BLK // W_IN_SIDE

    def main_idx(j, c):
        return jnp.where(j >= W_IN_C0, j - 1, j), c

    def side_idx(j, c):
        return jnp.where(j >= W_IN_C0, per * j, per * jnp.maximum(j, W_IN_B0) - 1), c

    return pl.pallas_call(
        _w_in_relayout_kernel,
        out_shape=jax.ShapeDtypeStruct((D_IN_PAD, D_MODEL), BF),
        grid=(D_IN_PAD // W_IN_BLK, D_MODEL // tc),
        in_specs=[pl.BlockSpec((W_IN_BLK, tc), main_idx), pl.BlockSpec((W_IN_SIDE, tc), side_idx)],
        out_specs=pl.BlockSpec((W_IN_BLK, tc), lambda j, c: (j, c)),
        compiler_params=_cparams(("arbitrary", "arbitrary")),
    )(w_in_t, w_in_t)


def _cast_pad_kernel(x_ref, o_ref, *, axis, n_real):
    is_pad = pl.program_id(axis) % (n_real + 1) == n_real

    @pl.when(is_pad)
    def _():
        o_ref[...] = jnp.zeros_like(o_ref)

    @pl.when(jnp.logical_not(is_pad))
    def _():
        o_ref[...] = x_ref[...].astype(BF)


def _cast_pad_cols(w, blk, n_real, n_seg):
    rows = w.shape[0]
    tr = 2048

    def in_idx(i, j):
        return i, (j // (n_real + 1)) * n_real + jnp.minimum(j % (n_real + 1), n_real - 1)

    return pl.pallas_call(
        functools.partial(_cast_pad_kernel, axis=1, n_real=n_real),
        out_shape=jax.ShapeDtypeStruct((rows, n_seg * (n_real + 1) * blk), BF),
        grid=(rows // tr, n_seg * (n_real + 1)),
        in_specs=[pl.BlockSpec((tr, blk), in_idx)],
        out_specs=pl.BlockSpec((tr, blk), lambda i, j: (i, j)),
        compiler_params=_cparams(("parallel", "arbitrary")),
    )(w)


def _cast_pad_rows(w, blk, n_real):
    cols = w.shape[1]
    return pl.pallas_call(
        functools.partial(_cast_pad_kernel, axis=0, n_real=n_real),
        out_shape=jax.ShapeDtypeStruct(((n_real + 1) * blk, cols), BF),
        grid=(n_real + 1,),
        in_specs=[pl.BlockSpec((blk, cols), lambda i: (jnp.minimum(i, n_real - 1), 0))],
        out_specs=pl.BlockSpec((blk, cols), lambda i: (i, 0)),
        compiler_params=_cparams(("arbitrary",)),
    )(w)


def _rope_tables():
    rows = DEC_SEQ // GRID_W
    row = jnp.repeat(jnp.arange(rows), GRID_W).astype(F32)
    col = jnp.tile(jnp.arange(GRID_W), rows).astype(F32)
    n_freq = ROPE_DIM // 4
    inv = ROPE_THETA ** (-jnp.arange(n_freq, dtype=F32) / n_freq)
    ang = jnp.concatenate([row[:, None] * inv, col[:, None] * inv], axis=-1)
    cos, sin = jnp.cos(ang), jnp.sin(ang)
    pad = jnp.zeros((DEC_SEQ, LANES - ROPE_DIM), F32)
    cos_t = jnp.concatenate([cos, cos, pad], axis=-1)
    sin_t = jnp.concatenate([-sin, sin, pad], axis=-1)
    ident = 256
    cos_t = jnp.concatenate([cos_t, jnp.ones((ident, LANES), F32)], axis=0)
    sin_t = jnp.concatenate([sin_t, jnp.zeros((ident, LANES), F32)], axis=0)
    return cos_t, sin_t


def _pair_state_in(state):
    b = state.shape[0]
    s = state.reshape(b, 2, N_PAIR, 2, HEAD_A, HEAD_A)
    z = jnp.zeros_like(s[:, :, :, 0])
    top = jnp.concatenate([s[:, :, :, 0], z], axis=-1)
    bot = jnp.concatenate([z, s[:, :, :, 1]], axis=-1)
    return jnp.concatenate([top, bot], axis=-2)


def kernel(x_prompt, x_sample, c, state_rwkv, cache_mla_ckv, cache_mla_kpe, c_ctx, w_mod, b_mod, norm_mix_w, w_in, rwkv_w0, rwkv_w_lora_b, rwkv_a0, rwkv_a_lora_b, rwkv_g_lora_b, rwkv_k_k, rwkv_k_a, rwkv_r_k, rwkv_ln_w, rwkv_ln_b, mla_q_norm_w, mla_w_uq, mla_kv_norm_w, mla_w_ukv, mla_q_head_norm, mla_k_head_norm, w_o_rwkv, w_o_mla, w_out, norm_ffn_w, w_ffn_in, w_ffn_out):
    l = 0
    xp = x_prompt.reshape(N_PROMPT, D_MODEL)
    xs = x_sample.reshape(N_SAMPLE, D_MODEL)

    w_in_t = _w_in_relayout(jnp.swapaxes(w_in[l], 0, 1))
    w_uq_p = jnp.pad(mla_w_uq[l].reshape(Q_LORA, H_B, QK_DIM), ((0, 0), (0, 0), (0, QK_PAD - QK_DIM)))
    w_uq_p = w_uq_p.reshape(Q_LORA, H_B * QK_PAD).astype(BF)
    q_head_w_p = jnp.pad(mla_q_head_norm[l], (0, QK_PAD - QK_DIM)).reshape(1, QK_PAD)
    wkv3 = mla_w_ukv[l].reshape(KV_LORA, H_B, NOPE_DIM + V_DIM)
    w_ukv_p = jnp.concatenate([wkv3[:, :, :NOPE_DIM].reshape(KV_LORA, H_B * NOPE_DIM),
                               wkv3[:, :, NOPE_DIM:].reshape(KV_LORA, H_B * V_DIM)], axis=1).astype(BF)
    wkn = mla_k_head_norm[l, :NOPE_DIM].reshape(1, NOPE_DIM)
    wkr = jnp.pad(mla_k_head_norm[l, NOPE_DIM:], (0, LANES - ROPE_DIM)).reshape(1, LANES)
    w_ffn_in_p = _cast_pad_cols(w_ffn_in[l], FF_BLK, D_FF // FF_BLK, 2)
    w_ffn_out_p = _cast_pad_rows(w_ffn_out[l], FF_BLK, D_FF // FF_BLK)
    w_oa = w_o_rwkv[l].astype(BF)
    w_ob = w_o_mla[l].astype(BF)
    w_out_b = w_out[l].astype(BF)
    wg_p = jnp.pad(rwkv_g_lora_b[l], ((0, R_G_PAD - R_G), (0, 0)))
    row = lambda a: a.reshape(1, D_A)
    wkv_w = (rwkv_w0[l], rwkv_w_lora_b[l], rwkv_a0[l], rwkv_a_lora_b[l], wg_p, row(rwkv_k_k[l]), row(rwkv_k_a[l]),
             row(rwkv_r_k[l]), row(rwkv_ln_w[l]), row(rwkv_ln_b[l]))
    cos_t, sin_t = _rope_tables()

    cond8 = jnp.concatenate([c_ctx[None, :], c, jnp.zeros((8 - 1 - DEC_BATCH, D_MODEL), F32)], axis=0)
    mod = _modulation(cond8, w_mod[l], b_mod[l]).reshape(8, 6, D_MODEL)

    proj = _in_proj(_norm_mod((xp, xs), norm_mix_w[l], mod, 0, 1), w_in_t)

    ya_p, s_new = _wkv(proj, 0, BATCH, SEQ, wkv_w, None)
    ya_s = _wkv(proj, N_PROMPT, DEC_BATCH, DEC_SEQ, wkv_w, _pair_state_in(state_rwkv[:, l]))[0]

    q = _q_proj(proj, mla_q_norm_w[l], w_uq_p, q_head_w_p, cos_t, sin_t)
    k, v, ckv = _kv_proj(proj, C_KVLAT, proj, C_KPE, N_TOK, mla_kv_norm_w[l], w_ukv_p, wkn, wkr, cos_t, sin_t,
                         _rope_block, True)
    ctx_ckv = cache_mla_ckv[:, l].reshape(DEC_BATCH * PAST_LEN, KV_LORA)
    ctx_kpe = jnp.pad(cache_mla_kpe[:, l].reshape(DEC_BATCH * PAST_LEN, ROPE_DIM), ((0, 0), (0, LANES - ROPE_DIM)))
    k_ctx, v_ctx = _kv_proj(ctx_ckv, 0, ctx_kpe, 0, DEC_BATCH * PAST_LEN, mla_kv_norm_w[l], w_ukv_p, wkn, wkr,
                            cos_t, sin_t, lambda i, tm: DEC_SEQ // tm, False)
    yb_p = _attn_prompt(q, k, v)
    yb_s = _attn_sample(q, k, v, k_ctx, v_ctx)

    mixed = _merge((ya_p, ya_s), (yb_p, yb_s), w_oa, w_ob, proj)
    x1 = _resid_proj(mixed, w_out_b, (xp, xs), mod, 2, 512, D_MODEL, 0, N_TOK)
    act = _ffn_in(_norm_mod((x1,), norm_ffn_w[l], mod, 3, 4), w_ffn_in_p)
    tk_ffn = D_FF_PAD // 4
    y_p = _resid_proj(act, w_ffn_out_p, (x1,), mod, 5, 1024, tk_ffn, 0, N_PROMPT).reshape(BATCH, SEQ, D_MODEL)
    y_s = _resid_proj(act, w_ffn_out_p, (x1,), mod, 5, 1024, tk_ffn, N_PROMPT, N_SAMPLE)
    y_s = y_s.reshape(DEC_BATCH, DEC_SEQ, D_MODEL)
    new_state = s_new[:, None]
    new_ckv = ckv[:N_PROMPT].reshape(BATCH, 1, SEQ, KV_LORA)
    new_kpe = proj[:N_PROMPT, C_KPE:C_KPE + ROPE_DIM].reshape(BATCH, 1, SEQ, ROPE_DIM)
    return y_p, y_s, new_state, new_ckv, new_kpe
```

```python
import functools

import numpy as np
import jax
import jax.numpy as jnp
from jax import lax
from jax.experimental import pallas as pl
from jax.experimental.pallas import tpu as pltpu

F32 = jnp.float32
BF = jnp.bfloat16

D_MODEL = 4096
BATCH, SEQ = 32, 256
DEC_BATCH, DEC_SEQ, PAST_LEN = 4, 1024, 512
GRID_W = 64
D_A, HEAD_A = 2048, 64
H_A = D_A // HEAD_A
R_LORA, R_G = 128, 480
LN_X_EPS = 64e-5
H_B, Q_LORA, KV_LORA = 16, 1024, 512
NOPE_DIM, ROPE_DIM, V_DIM = 128, 64, 128
QK_DIM = NOPE_DIM + ROPE_DIM
ROPE_THETA = 10000.0
D_FF = 11008
NORM_EPS = 1e-6

N_PROMPT = BATCH * SEQ
N_SAMPLE = DEC_BATCH * DEC_SEQ
N_TOK = N_PROMPT + N_SAMPLE

LANES = 128
R_G_PAD = 512
QK_PAD = 256
D_FF_PAD = 11264
FF_BLK = D_FF_PAD - D_FF
C_R, C_K, C_V = 0, 2048, 4096
C_LORA, C_LG, C_QLAT, C_KVLAT, C_KPE = 6144, 6656, 7168, 8192, 8704
C_GLA, C_GLB, D_IN_PAD = 9216, 13312, 17408
WKV_CHUNK = 64
WKV_UNROLL = 8
WKV_PAIRS_PROMPT = 4
WKV_PAIRS_SAMPLE = 2
DECAY_SCALE = float(np.exp(-0.5))
PAIR = 2 * HEAD_A
N_PAIR = D_A // PAIR

VMEM_LIMIT = 56 * 2**20


def _cparams(sem):
    return pltpu.CompilerParams(dimension_semantics=sem, vmem_limit_bytes=VMEM_LIMIT)


def _dot(a, b):
    return jnp.dot(a, b, preferred_element_type=F32)


def _dot_nt(a, b):
    return lax.dot_general(a, b, (((1,), (1,)), ((), ())), preferred_element_type=F32)


def _dot_tn(a, b):
    return lax.dot_general(a, b, (((0,), (0,)), ((), ())), preferred_element_type=F32)


def _mod_row(i, tm):
    n_prompt_tiles = N_PROMPT // tm
    tiles_per_seq = DEC_SEQ // tm
    return jnp.where(i < n_prompt_tiles, 0, 1 + (i - n_prompt_tiles) // tiles_per_seq)


def _mod_kernel(c_ref, w_ref, b_ref, o_ref):
    c = c_ref[...]
    s = c * jax.nn.sigmoid(c)
    o_ref[...] = _dot(s.astype(BF), w_ref[...].astype(BF)) + b_ref[...]


def _modulation(cond8, w_mod, b_mod):
    tn = 512
    n = w_mod.shape[1]
    return pl.pallas_call(
        _mod_kernel,
        out_shape=jax.ShapeDtypeStruct((8, n), F32),
        grid=(n // tn,),
        in_specs=[pl.BlockSpec((8, D_MODEL), lambda j: (0, 0)),
                  pl.BlockSpec((D_MODEL, tn), lambda j: (0, j)),
                  pl.BlockSpec((1, tn), lambda j: (0, j))],
        out_specs=pl.BlockSpec((8, tn), lambda j: (0, j)),
        compiler_params=_cparams(("arbitrary",)),
    )(cond8, w_mod, b_mod.reshape(1, n))


NORM_ROWS = 16


def _normmod(x_ref, nw_ref, mod_ref, o_ref, shift_idx, scale_idx):
    gain = nw_ref[...] * (1.0 + mod_ref[scale_idx:scale_idx + 1, :])
    shift = mod_ref[shift_idx:shift_idx + 1, :]

    def body(r, carry):
        rows = pl.ds(pl.multiple_of(r * NORM_ROWS, NORM_ROWS), NORM_ROWS)
        x = x_ref[rows, :]
        inv = lax.rsqrt(jnp.mean(x * x, axis=-1, keepdims=True) + NORM_EPS)
        o_ref[rows, :] = (x * inv * gain + shift).astype(BF)
        return carry

    lax.fori_loop(0, x_ref.shape[0] // NORM_ROWS, body, 0)


def _split_specs(tm, width, col):
    npt = N_PROMPT // tm
    return [pl.BlockSpec((tm, width), lambda i, *_: (jnp.minimum(i, npt - 1), col(*_))),
            pl.BlockSpec((tm, width), lambda i, *_: (jnp.maximum(i - npt, 0), col(*_)))]


def _norm_mod_kernel(*refs, shift_idx, scale_idx, n_prompt_tiles):
    *x_refs, nw_ref, mod_ref, o_ref = refs

    def run(x_ref):
        _normmod(x_ref, nw_ref, mod_ref, o_ref, shift_idx, scale_idx)

    if len(x_refs) == 1:
        run(x_refs[0])
    else:
        is_prompt = pl.program_id(0) < n_prompt_tiles
        pl.when(is_prompt)(lambda: run(x_refs[0]))
        pl.when(jnp.logical_not(is_prompt))(lambda: run(x_refs[1]))


def _norm_mod(xs, norm_w, mod, shift_idx, scale_idx):
    tm = 256
    if len(xs) == 1:
        x_specs = [pl.BlockSpec((tm, D_MODEL), lambda i: (i, 0))]
    else:
        x_specs = _split_specs(tm, D_MODEL, lambda: 0)
    return pl.pallas_call(
        functools.partial(_norm_mod_kernel, shift_idx=shift_idx, scale_idx=scale_idx, n_prompt_tiles=N_PROMPT // tm),
        out_shape=jax.ShapeDtypeStruct((N_TOK, D_MODEL), BF),
        grid=(N_TOK // tm,),
        in_specs=x_specs + [pl.BlockSpec((1, D_MODEL), lambda i: (0, 0)),
                            pl.BlockSpec((None, 6, D_MODEL), lambda i: (_mod_row(i, tm), 0, 0))],
        out_specs=pl.BlockSpec((tm, D_MODEL), lambda i: (i, 0)),
        compiler_params=_cparams(("parallel",)),
    )(*xs, norm_w.reshape(1, D_MODEL), mod)


CAST_SPLIT = 2


class _CastJob:
    def __init__(self, w, axis, blk, n_real, n_seg, nj):
        self.w, self.axis, self.blk, self.n_real, self.nj = w, axis, blk, n_real, nj
        self.n_parts = n_seg * (n_real + 1) * CAST_SPLIT
        other = w.shape[1 - axis]
        part = other // CAST_SPLIT
        self.block = (blk, part) if axis == 0 else (part, blk)
        padded = n_seg * (n_real + 1) * blk
        self.out_shape = jax.ShapeDtypeStruct((padded, other) if axis == 0 else (other, padded), BF)

    def _place(self, b, part):
        return (b, part) if self.axis == 0 else (part, b)

    def _block_of(self, i, j):
        t = jnp.minimum(i * self.nj + j, self.n_parts - 1)
        return t // CAST_SPLIT, t % CAST_SPLIT

    def in_spec(self):
        def idx(i, j):
            b, part = self._block_of(i, j)
            seg, jj = b // (self.n_real + 1), b % (self.n_real + 1)
            return self._place(seg * self.n_real + jnp.minimum(jj, self.n_real - 1), part)
        return pl.BlockSpec(self.block, idx)

    def out_spec(self):
        return pl.BlockSpec(self.block, lambda i, j: self._place(*self._block_of(i, j)))

    def step(self, x_ref, o_ref):
        t = pl.program_id(0) * self.nj + pl.program_id(1)
        is_pad = (t // CAST_SPLIT) % (self.n_real + 1) == self.n_real
        active = t < self.n_parts

        @pl.when(active & is_pad)
        def _():
            o_ref[...] = jnp.zeros_like(o_ref)

        @pl.when(active & jnp.logical_not(is_pad))
        def _():
            o_ref[...] = x_ref[...].astype(BF)


def _in_proj_kernel(h_ref, wt_ref, wc_ref, o_ref, oc_ref, *, job):
    o_ref[...] = _dot_nt(h_ref[...], wt_ref[...])
    job.step(wc_ref, oc_ref)


def _in_proj(h, w_in_t, w_ffn_in):
    tm, tn = 1024, 1024
    nj = D_IN_PAD // tn
    job = _CastJob(w_ffn_in, 1, FF_BLK, D_FF // FF_BLK, 2, nj)
    return pl.pallas_call(
        functools.partial(_in_proj_kernel, job=job),
        out_shape=[jax.ShapeDtypeStruct((N_TOK, D_IN_PAD), F32), job.out_shape],
        grid=(N_TOK // tm, nj),
        in_specs=[pl.BlockSpec((tm, D_MODEL), lambda i, j: (i, 0)),
                  pl.BlockSpec((tn, D_MODEL), lambda i, j: (j, 0)),
                  job.in_spec()],
        out_specs=[pl.BlockSpec((tm, tn), lambda i, j: (i, j)), job.out_spec()],
        compiler_params=_cparams(("arbitrary", "arbitrary")),
    )(h, w_in_t, w_ffn_in)


def _ffn_in_kernel(h_ref, wg_ref, wu_ref, wc_ref, o_ref, oc_ref, *, job):
    h = h_ref[...]
    g = _dot(h, wg_ref[...])
    u = _dot(h, wu_ref[...])
    o_ref[...] = (g * jax.nn.sigmoid(g) * u).astype(BF)
    job.step(wc_ref, oc_ref)


def _ffn_in(h, w_ffn_in_p, w_ffn_out):
    tm, tn = 1024, 512
    nj = D_FF_PAD // tn
    job = _CastJob(w_ffn_out, 0, FF_BLK, D_FF // FF_BLK, 1, nj)
    return pl.pallas_call(
        functools.partial(_ffn_in_kernel, job=job),
        out_shape=[jax.ShapeDtypeStruct((N_TOK, D_FF_PAD), BF), job.out_shape],
        grid=(N_TOK // tm, nj),
        in_specs=[pl.BlockSpec((tm, D_MODEL), lambda i, j: (i, 0)),
                  pl.BlockSpec((D_MODEL, tn), lambda i, j: (0, j)),
                  pl.BlockSpec((D_MODEL, tn), lambda i, j: (0, j + nj)),
                  job.in_spec()],
        out_specs=[pl.BlockSpec((tm, tn), lambda i, j: (i, j)), job.out_spec()],
        compiler_params=_cparams(("arbitrary", "arbitrary")),
    )(h, w_ffn_in_p, w_ffn_in_p, w_ffn_out)


def _resid_proj_kernel(a_ref, w_ref, *refs, gate_idx, n_prompt_tiles, nk):
    if nk == 1:
        *x_refs, mod_ref, o_ref = refs
    else:
        *x_refs, mod_ref, o_ref, acc_ref = refs
    part = _dot(a_ref[...], w_ref[...])
    last = True
    if nk > 1:
        k = pl.program_id(2)
        last = k == nk - 1

        @pl.when(k == 0)
        def _():
            acc_ref[...] = part

        @pl.when((k > 0) & jnp.logical_not(last))
        def _():
            acc_ref[...] += part

    def finish(x_ref):
        total = part if nk == 1 else acc_ref[...] + part
        o_ref[...] = x_ref[...] + mod_ref[gate_idx:gate_idx + 1, :] * total

    if len(x_refs) == 1:
        pl.when(jnp.asarray(last))(lambda: finish(x_refs[0]))
    else:
        is_prompt = pl.program_id(0) < n_prompt_tiles
        pl.when(last & is_prompt)(lambda: finish(x_refs[0]))
        pl.when(last & jnp.logical_not(is_prompt))(lambda: finish(x_refs[1]))


def _resid_proj(a, w, xs, mod, gate_idx, tn, tk, row0, n_rows):
    tm = 1024
    kdim = a.shape[1]
    nk = kdim // tk
    r0 = row0 // tm
    if len(xs) == 1:
        x_specs = [pl.BlockSpec((tm, tn), lambda i, j, k: (r0 + i, j))]
    else:
        x_specs = _split_specs(tm, tn, lambda j, k: j)
    return pl.pallas_call(
        functools.partial(_resid_proj_kernel, gate_idx=gate_idx, n_prompt_tiles=N_PROMPT // tm, nk=nk),
        out_shape=jax.ShapeDtypeStruct((n_rows, D_MODEL), F32),
        grid=(n_rows // tm, D_MODEL // tn, nk),
        in_specs=[pl.BlockSpec((tm, tk), lambda i, j, k: (r0 + i, k)),
                  pl.BlockSpec((tk, tn), lambda i, j, k: (k, j))] + x_specs + [
                  pl.BlockSpec((None, 6, tn), lambda i, j, k: (_mod_row(r0 + i, tm), 0, j))],
        out_specs=pl.BlockSpec((tm, tn), lambda i, j, k: (i, j)),
        scratch_shapes=[pltpu.VMEM((tm, tn), F32)] if nk > 1 else [],
        compiler_params=_cparams(("parallel", "parallel", "arbitrary")),
    )(a, w, *xs, mod)


def _merge_kernel(yap_ref, yas_ref, ybp_ref, ybs_ref, wa_ref, wb_ref, gla_ref, glb_ref, o_ref, *, n_prompt_tiles):
    is_prompt = pl.program_id(0) < n_prompt_tiles
    ya = jnp.where(is_prompt, yap_ref[...], yas_ref[...])
    yb = jnp.where(is_prompt, ybp_ref[...], ybs_ref[...])
    pa = _dot(ya, wa_ref[...])
    pb = _dot(yb, wb_ref[...])
    o_ref[...] = (jax.nn.sigmoid(gla_ref[...]) * pa + jax.nn.sigmoid(glb_ref[...]) * pb).astype(BF)


def _merge(y_a, y_b, w_oa, w_ob, proj):
    tm, tn = 512, 1024
    return pl.pallas_call(
        functools.partial(_merge_kernel, n_prompt_tiles=N_PROMPT // tm),
        out_shape=jax.ShapeDtypeStruct((N_TOK, D_MODEL), BF),
        grid=(N_TOK // tm, D_MODEL // tn),
        in_specs=_split_specs(tm, D_A, lambda j: 0) + _split_specs(tm, D_A, lambda j: 0) + [
                  pl.BlockSpec((D_A, tn), lambda i, j: (0, j)),
                  pl.BlockSpec((D_A, tn), lambda i, j: (0, j)),
                  pl.BlockSpec((tm, tn), lambda i, j: (i, C_GLA // tn + j)),
                  pl.BlockSpec((tm, tn), lambda i, j: (i, C_GLB // tn + j))],
        out_specs=pl.BlockSpec((tm, tn), lambda i, j: (i, j)),
        compiler_params=_cparams(("parallel", "arbitrary")),
    )(*y_a, *y_b, w_oa, w_ob, proj, proj)


def _rope(x, cos_ref, sin_ref):
    swapped = pltpu.roll(x, ROPE_DIM // 2, axis=1) + pltpu.roll(x, LANES - ROPE_DIM // 2, axis=1)
    return x * cos_ref[...] + swapped * sin_ref[...]


def _rope_block(i, tm):
    n_prompt_tiles = N_PROMPT // tm
    tiles_per_seq = DEC_SEQ // tm
    return jnp.where(i < n_prompt_tiles, tiles_per_seq, (i - n_prompt_tiles) % tiles_per_seq)


def _q_kernel(ql_ref, nw_ref, w_ref, hw_ref, cos_ref, sin_ref, o_ref):
    x = ql_ref[...]
    y = x * lax.rsqrt(jnp.mean(x * x, axis=-1, keepdims=True) + NORM_EPS) * nw_ref[...]
    q = _dot(y.astype(BF), w_ref[...])
    hw = hw_ref[...]
    for h in range(H_B):
        qh = q[:, h * QK_PAD:(h + 1) * QK_PAD]
        inv = lax.rsqrt(jnp.sum(qh * qh, axis=-1, keepdims=True) * (1.0 / QK_DIM) + NORM_EPS)
        qn = qh * inv * hw
        o_ref[:, h * QK_PAD:h * QK_PAD + NOPE_DIM] = qn[:, :NOPE_DIM].astype(BF)
        o_ref[:, h * QK_PAD + NOPE_DIM:(h + 1) * QK_PAD] = _rope(qn[:, NOPE_DIM:], cos_ref, sin_ref).astype(BF)


def _q_proj(proj, q_norm_w, w_uq_p, q_head_w_p, cos_t, sin_t):
    tm = 256
    return pl.pallas_call(
        _q_kernel,
        out_shape=jax.ShapeDtypeStruct((N_TOK, H_B * QK_PAD), BF),
        grid=(N_TOK // tm,),
        in_specs=[pl.BlockSpec((tm, Q_LORA), lambda i: (i, C_QLAT // Q_LORA)),
                  pl.BlockSpec((1, Q_LORA), lambda i: (0, 0)),
                  pl.BlockSpec((Q_LORA, H_B * QK_PAD), lambda i: (0, 0)),
                  pl.BlockSpec((1, QK_PAD), lambda i: (0, 0)),
                  pl.BlockSpec((tm, LANES), lambda i: (_rope_block(i, tm), 0)),
                  pl.BlockSpec((tm, LANES), lambda i: (_rope_block(i, tm), 0))],
        out_specs=pl.BlockSpec((tm, H_B * QK_PAD), lambda i: (i, 0)),
        compiler_params=_cparams(("parallel",)),
    )(proj, q_norm_w.reshape(1, Q_LORA), w_uq_p, q_head_w_p, cos_t, sin_t)


def _kv_kernel(lat_ref, kpe_ref, nw_ref, w_ref, wkn_ref, wkr_ref, cos_ref, sin_ref, *out_refs, pre_norm):
    if pre_norm:
        k_ref, v_ref, ckv_ref = out_refs
        x = lat_ref[...]
        ckv = x * lax.rsqrt(jnp.mean(x * x, axis=-1, keepdims=True) + NORM_EPS) * nw_ref[...]
        ckv_ref[...] = ckv
    else:
        k_ref, v_ref = out_refs
        ckv = lat_ref[...]
    kv = _dot(ckv.astype(BF), w_ref[...])
    v_ref[...] = kv[:, H_B * NOPE_DIM:].astype(BF)
    kpe = kpe_ref[...]
    kpe_ss = jnp.sum(kpe * kpe, axis=-1, keepdims=True)
    kpe_rot = _rope(kpe * wkr_ref[...], cos_ref, sin_ref)
    wkn = wkn_ref[...]
    for h in range(H_B):
        kn = kv[:, h * NOPE_DIM:(h + 1) * NOPE_DIM]
        inv = lax.rsqrt((jnp.sum(kn * kn, axis=-1, keepdims=True) + kpe_ss) * (1.0 / QK_DIM) + NORM_EPS)
        k_ref[:, h * QK_PAD:h * QK_PAD + NOPE_DIM] = (kn * inv * wkn).astype(BF)
        k_ref[:, h * QK_PAD + NOPE_DIM:(h + 1) * QK_PAD] = (kpe_rot * inv).astype(BF)


def _kv_proj(lat, lat_col, kpe, kpe_col, n_rows, kv_norm_w, w_ukv_p, wkn, wkr, cos_t, sin_t, rope_block, pre_norm):
    tm = 256
    out_shape = [jax.ShapeDtypeStruct((n_rows, H_B * QK_PAD), BF),
                 jax.ShapeDtypeStruct((n_rows, H_B * V_DIM), BF)]
    out_specs = [pl.BlockSpec((tm, H_B * QK_PAD), lambda i: (i, 0)),
                 pl.BlockSpec((tm, H_B * V_DIM), lambda i: (i, 0))]
    if pre_norm:
        out_shape.append(jax.ShapeDtypeStruct((n_rows, KV_LORA), F32))
        out_specs.append(pl.BlockSpec((tm, KV_LORA), lambda i: (i, 0)))
    return pl.pallas_call(
        functools.partial(_kv_kernel, pre_norm=pre_norm),
        out_shape=out_shape,
        grid=(n_rows // tm,),
        in_specs=[pl.BlockSpec((tm, KV_LORA), lambda i: (i, lat_col // KV_LORA)),
                  pl.BlockSpec((tm, LANES), lambda i: (i, kpe_col // LANES)),
                  pl.BlockSpec((1, KV_LORA), lambda i: (0, 0)),
                  pl.BlockSpec((KV_LORA, H_B * (NOPE_DIM + V_DIM)), lambda i: (0, 0)),
                  pl.BlockSpec((1, NOPE_DIM), lambda i: (0, 0)),
                  pl.BlockSpec((1, LANES), lambda i: (0, 0)),
                  pl.BlockSpec((tm, LANES), lambda i: (rope_block(i, tm), 0)),
                  pl.BlockSpec((tm, LANES), lambda i: (rope_block(i, tm), 0))],
        out_specs=out_specs,
        compiler_params=_cparams(("parallel",)),
    )(lat, kpe, kv_norm_w.reshape(1, KV_LORA), w_ukv_p, wkn, wkr, cos_t, sin_t)


_ATTN_SCALE = QK_DIM ** -0.5


def _attn_prompt_kernel(q_ref, k_ref, v_ref, o_ref, *, heads):
    for h in range(heads):
        q = q_ref[:, h * QK_PAD:(h + 1) * QK_PAD]
        k = k_ref[:, h * QK_PAD:(h + 1) * QK_PAD]
        s = _dot_nt(q, k) * _ATTN_SCALE
        e = jnp.exp(s - jnp.max(s, axis=-1, keepdims=True))
        p = e * (1.0 / jnp.sum(e, axis=-1, keepdims=True))
        o_ref[:, h * V_DIM:(h + 1) * V_DIM] = _dot(p.astype(BF), v_ref[:, h * V_DIM:(h + 1) * V_DIM]).astype(BF)


def _attn_prompt(q, k, v):
    heads = 4
    return pl.pallas_call(
        functools.partial(_attn_prompt_kernel, heads=heads),
        out_shape=jax.ShapeDtypeStruct((N_PROMPT, H_B * V_DIM), BF),
        grid=(BATCH, H_B // heads),
        in_specs=[pl.BlockSpec((SEQ, heads * QK_PAD), lambda b, g: (b, g)),
                  pl.BlockSpec((SEQ, heads * QK_PAD), lambda b, g: (b, g)),
                  pl.BlockSpec((SEQ, heads * V_DIM), lambda b, g: (b, g))],
        out_specs=pl.BlockSpec((SEQ, heads * V_DIM), lambda b, g: (b, g)),
        compiler_params=_cparams(("parallel", "parallel")),
    )(q, k, v)


def _attn_sample_kernel(q_ref, kc_ref, vc_ref, ks_ref, vs_ref, o_ref, *, heads):
    for h in range(heads):
        qk = slice(h * QK_PAD, (h + 1) * QK_PAD)
        vv = slice(h * V_DIM, (h + 1) * V_DIM)
        q = q_ref[:, qk]
        s1 = _dot_nt(q, kc_ref[:, qk]) * _ATTN_SCALE
        s2 = _dot_nt(q, ks_ref[:, qk]) * _ATTN_SCALE
        m = jnp.maximum(jnp.max(s1, axis=-1, keepdims=True), jnp.max(s2, axis=-1, keepdims=True))
        e1 = jnp.exp(s1 - m)
        e2 = jnp.exp(s2 - m)
        inv = 1.0 / (jnp.sum(e1, axis=-1, keepdims=True) + jnp.sum(e2, axis=-1, keepdims=True))
        o = _dot((e1 * inv).astype(BF), vc_ref[:, vv]) + _dot((e2 * inv).astype(BF), vs_ref[:, vv])
        o_ref[:, vv] = o.astype(BF)


def _attn_sample(q, k, v, k_ctx, v_ctx):
    tq = 512
    heads = 2
    qt = DEC_SEQ // tq
    q0 = N_PROMPT // tq
    s0 = N_PROMPT // DEC_SEQ
    return pl.pallas_call(
        functools.partial(_attn_sample_kernel, heads=heads),
        out_shape=jax.ShapeDtypeStruct((N_SAMPLE, H_B * V_DIM), BF),
        grid=(DEC_BATCH, H_B // heads, qt),
        in_specs=[pl.BlockSpec((tq, heads * QK_PAD), lambda b, g, t: (q0 + b * qt + t, g)),
                  pl.BlockSpec((PAST_LEN, heads * QK_PAD), lambda b, g, t: (b, g)),
                  pl.BlockSpec((PAST_LEN, heads * V_DIM), lambda b, g, t: (b, g)),
                  pl.BlockSpec((DEC_SEQ, heads * QK_PAD), lambda b, g, t: (s0 + b, g)),
                  pl.BlockSpec((DEC_SEQ, heads * V_DIM), lambda b, g, t: (s0 + b, g))],
        out_specs=pl.BlockSpec((tq, heads * V_DIM), lambda b, g, t: (b * qt + t, g)),
        compiler_params=_cparams(("parallel", "parallel", "arbitrary")),
    )(q, k_ctx, v_ctx, k, v)


def _head_sum(x, ones_blk):
    hi = x.astype(BF)
    lo = (x - hi.astype(F32)).astype(BF)
    return _dot(hi, ones_blk) + _dot(lo, ones_blk)


def _wkv_chunks_a(items, strict, incl, lane_lo, eye):
    C = WKV_CHUNK
    n2 = 2 * C

    def stack(x):
        return jnp.concatenate([jnp.where(lane_lo, x, 0.0), jnp.where(lane_lo, 0.0, x)], axis=0)

    row_c = lax.broadcasted_iota(jnp.int32, (C, PAIR), 0)

    def cumsum(it):
        x = it["lw"]
        s = 1
        while s < C:
            if it["d"] == 0:
                x = x + jnp.where(row_c >= s, pltpu.roll(x, s, axis=0), 0.0)
            else:
                x = x + jnp.where(row_c < C - s, pltpu.roll(x, C - s, axis=0), 0.0)
            s *= 2
        return x

    cs = [cumsum(it) for it in items]

    def operands(it, c):
        cl = c[C - 1:C, :] if it["d"] == 0 else c[0:1, :]
        e_inc = jnp.exp(c)
        e_neg = jnp.exp(-c)
        e_exc = jnp.exp(c - it["lw"])
        e_end = jnp.exp(cl - c)
        b, kd = it["b"], it["kd"]
        return dict(d=it["d"], e=jnp.exp(cl),
                    a_t=stack(-it["kk"] * e_exc).astype(BF), r_t=stack(it["r"] * e_inc),
                    b_t=stack(b * e_neg).astype(BF), k_t=stack(kd * e_neg).astype(BF),
                    bk_h=jnp.concatenate([stack(b * e_end), stack(kd * e_end)], axis=0).astype(BF),
                    v_s=stack(it["v"]).astype(BF))

    ops = [operands(it, c) for it, c in zip(items, cs)]
    scs = [_dot_nt(jnp.concatenate([o["a_t"], o["r_t"].astype(BF)], axis=0),
                   jnp.concatenate([o["b_t"], o["k_t"]], axis=0)) for o in ops]
    for o, sc in zip(ops, scs):
        d = o["d"]
        o["lp"] = jnp.where(strict[d], sc[:n2, :n2], 0.0).astype(BF)
        o["l_ak"] = jnp.where(strict[d], sc[:n2, n2:], 0.0).astype(BF)
        o["a_r"] = jnp.concatenate([jnp.where(incl[d], sc[n2:, :n2], 0.0),
                                    jnp.where(incl[d], sc[n2:, n2:], 0.0)], axis=1).astype(BF)

    ts = [eye + o["lp"].astype(F32) for o in ops]
    lps = [_dot(o["lp"], o["lp"]).astype(BF) for o in ops]
    span = 2
    while 2 * span < C:
        rs = [_dot(lp, jnp.concatenate([t.astype(BF), lp], axis=1)) for t, lp in zip(ts, lps)]
        ts = [t + r[:, :n2] for t, r in zip(ts, rs)]
        lps = [r[:, n2:].astype(BF) for r in rs]
        span *= 2
    ts = [t + _dot(lp, t.astype(BF)) for t, lp in zip(ts, lps)]

    lakvs = [_dot(o["l_ak"], o["v_s"]) for o in ops]
    wus = [_dot(t.astype(BF), jnp.concatenate([o["a_t"], lakv.astype(BF)], axis=1))
           for t, o, lakv in zip(ts, ops, lakvs)]
    zero = jnp.zeros((n2, n2), BF)
    outs = []
    for o, wu in zip(ops, wus):
        rhs = jnp.concatenate([wu.astype(BF), jnp.concatenate([zero, o["v_s"]], axis=1)], axis=0)
        qy = _dot(o["a_r"], rhs)
        mn = _dot_tn(o["bk_h"], rhs)
        q = (o["r_t"] + qy[:, :n2]).astype(BF)
        y0 = qy[:C, n2:] + qy[C:, n2:]
        outs.append((q, y0, mn[:, :n2].astype(BF), mn[:, n2:].T, o["e"]))
    return outs


def _wkv_kernel(*refs, T, n_pairs, has_s0, out_state):
    (r_ref, k_ref, v_ref, lora_ref, lg_ref, w0_ref, wl_ref, a0_ref, wa_ref, wg_ref,
     kkw_ref, ka_ref, rk_ref, lnw_ref, lnb_ref) = refs[:15]
    pos = 15
    if has_s0:
        s0_ref = refs[pos]
        pos += 1
    y_ref = refs[pos]
    pos += 1
    if out_state:
        so_ref = refs[pos]
        pos += 1
    lw_s, b_s, kd_s, kk_s, y_s, q_s, m_s, n_s, e_s = refs[pos:]

    C = WKV_CHUNK
    nc = T // C
    lane = lax.broadcasted_iota(jnp.int32, (PAIR, PAIR), 1)
    row = lax.broadcasted_iota(jnp.int32, (PAIR, PAIR), 0)
    ones_blk = jnp.where((lane // HEAD_A) == (row // HEAD_A), 1.0, 0.0).astype(BF)
    eye = jnp.where(lane == row, 1.0, 0.0)
    lane_lo = lax.broadcasted_iota(jnp.int32, (C, PAIR), 1) < HEAD_A
    strict = (lane < row, lane > row)
    incl = (lane <= row, lane >= row)
    pairs = range(n_pairs)

    def cols(pp):
        return slice(pp * PAIR, (pp + 1) * PAIR)

    for pp in pairs:
        k = k_ref[:, cols(pp)]
        kk = k * kkw_ref[:, cols(pp)]
        kk = kk * lax.rsqrt(_head_sum(kk * kk, ones_blk) + 1e-12)
        kk_s[pp] = kk
        for d in range(2):
            lw_in = lora_ref[:, d * R_LORA:(d + 1) * R_LORA]
            la_in = lora_ref[:, (2 + d) * R_LORA:(3 + d) * R_LORA]
            x = w0_ref[d:d + 1, cols(pp)] + _dot(jnp.tanh(lw_in).astype(BF), wl_ref[d, :, cols(pp)].astype(BF))
            lw_s[2 * pp + d] = -DECAY_SCALE * jax.nn.sigmoid(x)
            a = jax.nn.sigmoid(a0_ref[d:d + 1, cols(pp)] + _dot(la_in.astype(BF), wa_ref[d, :, cols(pp)].astype(BF)))
            kd_s[2 * pp + d] = k * (1.0 + (a - 1.0) * ka_ref[:, cols(pp)])
            b_s[2 * pp + d] = kk * a

    def chunk_rows(idx):
        return pl.ds(idx * C, C) if isinstance(idx, int) else pl.ds(pl.multiple_of(idx * C, C), C)

    per_iter = WKV_UNROLL // 2

    def phase_a(pp, base):
        keys = [(d, base + t) for t in range(per_iter) for d in range(2)]
        items = []
        for d, idx in keys:
            sl = chunk_rows(idx)
            items.append(dict(d=d, r=r_ref[sl, cols(pp)], v=v_ref[sl, cols(pp)], kk=kk_s[pp, sl, :],
                              lw=lw_s[2 * pp + d, sl, :], b=b_s[2 * pp + d, sl, :], kd=kd_s[2 * pp + d, sl, :]))
        outs = _wkv_chunks_a(items, strict, incl, lane_lo, eye)
        for (d, idx), (q, y0, mt, n, e) in zip(keys, outs):
            q_s[2 * pp + d, idx] = q
            y_s[2 * pp + d, chunk_rows(idx), :] = y0
            m_s[2 * pp + d, idx] = mt
            n_s[2 * pp + d, idx] = n
            e_s[2 * pp + d, idx] = jnp.broadcast_to(e, (8, PAIR))

    for pp in pairs:
        if nc == per_iter:
            phase_a(pp, 0)
        else:
            def body_a(j, carry, pp=pp):
                phase_a(pp, j * per_iter)
                return carry

            lax.fori_loop(0, nc // per_iter, body_a, 0)

    chains = [(pp, d) for pp in pairs for d in range(2)]

    def body_b(i, carry):
        nxt = []
        for (pp, d), s in zip(chains, carry):
            idx = i if d == 0 else nc - 1 - i
            sl = chunk_rows(idx)
            sb = s.astype(BF)
            yq = _dot_nt(q_s[2 * pp + d, idx], sb)
            y_s[2 * pp + d, sl, :] += yq[:C] + yq[C:]
            nxt.append(s * e_s[2 * pp + d, idx][0:1, :] + _dot_nt(sb, m_s[2 * pp + d, idx]) + n_s[2 * pp + d, idx])
        return tuple(nxt)

    if has_s0:
        init = tuple(s0_ref[d, pp] for pp, d in chains)
    else:
        init = tuple(jnp.zeros((PAIR, PAIR), F32) for _ in chains)
    s_fin = lax.fori_loop(0, nc, body_b, init)
    if out_state:
        for (pp, d), s in zip(chains, s_fin):
            so_ref[d, 2 * pp] = s[:HEAD_A, :HEAD_A]
            so_ref[d, 2 * pp + 1] = s[HEAD_A:, HEAD_A:]

    sig_lg = jax.nn.sigmoid(lg_ref[...]).astype(BF)
    inv_n = 1.0 / HEAD_A
    for pp in pairs:
        y = y_s[2 * pp] + y_s[2 * pp + 1]
        mu = _head_sum(y, ones_blk) * inv_n
        yc = y - mu
        var = _head_sum(yc * yc, ones_blk) * inv_n
        y = yc * lax.rsqrt(var + LN_X_EPS) * lnw_ref[:, cols(pp)] + lnb_ref[:, cols(pp)]
        rr = r_ref[:, cols(pp)] * rk_ref[:, cols(pp)]
        bonus = _head_sum(rr * (kd_s[2 * pp] + kd_s[2 * pp + 1]), ones_blk) * v_ref[:, cols(pp)]
        g = _dot(sig_lg, wg_ref[:, cols(pp)].astype(BF))
        y_ref[:, cols(pp)] = ((y + bonus) * g).astype(BF)


def _wkv(proj, row0, n_seq, T, wts, s0, n_pairs):
    (w0, wl, a0, wa, wg, kkw, ka, rk, lnw, lnb) = wts
    sb = row0 // T
    nc = T // WKV_CHUNK
    gw = n_pairs * PAIR
    cb = lambda col: col // gw
    vec = pl.BlockSpec((1, gw), lambda s, p: (0, p))
    in_specs = [pl.BlockSpec((T, gw), lambda s, p: (sb + s, cb(C_R) + p)),
                pl.BlockSpec((T, gw), lambda s, p: (sb + s, cb(C_K) + p)),
                pl.BlockSpec((T, gw), lambda s, p: (sb + s, cb(C_V) + p)),
                pl.BlockSpec((T, 4 * R_LORA), lambda s, p: (sb + s, C_LORA // (4 * R_LORA))),
                pl.BlockSpec((T, R_G_PAD), lambda s, p: (sb + s, C_LG // R_G_PAD)),
                pl.BlockSpec((2, gw), lambda s, p: (0, p)),
                pl.BlockSpec((2, R_LORA, gw), lambda s, p: (0, 0, p)),
                pl.BlockSpec((2, gw), lambda s, p: (0, p)),
                pl.BlockSpec((2, R_LORA, gw), lambda s, p: (0, 0, p)),
                pl.BlockSpec((R_G_PAD, gw), lambda s, p: (0, p)),
                vec, vec, vec, vec, vec]
    args = [proj, proj, proj, proj, proj, w0, wl, a0, wa, wg, kkw, ka, rk, lnw, lnb]
    has_s0 = s0 is not None
    if has_s0:
        in_specs.append(pl.BlockSpec((None, 2, n_pairs, PAIR, PAIR), lambda s, p: (s, 0, p, 0, 0)))
        args.append(s0)
    out_shape = [jax.ShapeDtypeStruct((n_seq * T, D_A), BF)]
    out_specs = [pl.BlockSpec((T, gw), lambda s, p: (s, p))]
    out_state = not has_s0
    if out_state:
        out_shape.append(jax.ShapeDtypeStruct((n_seq, 2, H_A, HEAD_A, HEAD_A), F32))
        out_specs.append(pl.BlockSpec((None, 2, 2 * n_pairs, HEAD_A, HEAD_A), lambda s, p: (s, 0, p, 0, 0)))
    nch = 2 * n_pairs
    return pl.pallas_call(
        functools.partial(_wkv_kernel, T=T, n_pairs=n_pairs, has_s0=has_s0, out_state=out_state),
        out_shape=out_shape,
        grid=(n_seq, N_PAIR // n_pairs),
        in_specs=in_specs,
        out_specs=out_specs,
        scratch_shapes=[pltpu.VMEM((nch, T, PAIR), F32), pltpu.VMEM((nch, T, PAIR), F32),
                        pltpu.VMEM((nch, T, PAIR), F32), pltpu.VMEM((n_pairs, T, PAIR), F32),
                        pltpu.VMEM((nch, T, PAIR), F32),
                        pltpu.VMEM((nch, nc, PAIR, PAIR), BF), pltpu.VMEM((nch, nc, PAIR, PAIR), BF),
                        pltpu.VMEM((nch, nc, PAIR, PAIR), F32), pltpu.VMEM((nch, nc, 8, PAIR), F32)],
        compiler_params=_cparams(("parallel", "arbitrary")),
    )(*args)


W_IN_BLK = 512
W_IN_SIDE = 128
W_IN_B0 = C_QLAT // W_IN_BLK
W_IN_C0 = C_GLA // W_IN_BLK
W_IN_SHIFT = R_G_PAD - R_G


def _w_in_relayout_kernel(main_ref, side_ref, o_ref):
    j = pl.program_id(0)
    row = lax.broadcasted_iota(jnp.int32, o_ref.shape, 0)
    sh = W_IN_SHIFT

    @pl.when(j < W_IN_B0)
    def _():
        valid = jnp.where(j == W_IN_B0 - 1, R_G, W_IN_BLK)
        o_ref[...] = jnp.where(row < valid, main_ref[...], 0.0).astype(BF)

    @pl.when((j >= W_IN_B0) & (j < W_IN_C0))
    def _():
        valid = jnp.where(j == W_IN_C0 - 1, ROPE_DIM, W_IN_BLK)
        x = jnp.concatenate([side_ref[W_IN_SIDE - sh:, :], main_ref[:W_IN_BLK - sh, :]], axis=0)
        o_ref[...] = jnp.where(row < valid, x, 0.0).astype(BF)

    @pl.when(j >= W_IN_C0)
    def _():
        x = jnp.concatenate([main_ref[sh:, :], side_ref[:sh, :]], axis=0)
        o_ref[...] = x.astype(BF)


def _w_in_relayout(w_in_t):
    tc = 2048
    per = W_IN_BLK // W_IN_SIDE

    def main_idx(j, c):
        return jnp.where(j >= W_IN_C0, j - 1, j), c

    def side_idx(j, c):
        return jnp.where(j >= W_IN_C0, per * j, per * jnp.maximum(j, W_IN_B0) - 1), c

    return pl.pallas_call(
        _w_in_relayout_kernel,
        out_shape=jax.ShapeDtypeStruct((D_IN_PAD, D_MODEL), BF),
        grid=(D_IN_PAD // W_IN_BLK, D_MODEL // tc),
        in_specs=[pl.BlockSpec((W_IN_BLK, tc), main_idx), pl.BlockSpec((W_IN_SIDE, tc), side_idx)],
        out_specs=pl.BlockSpec((W_IN_BLK, tc), lambda j, c: (j, c)),
        compiler_params=_cparams(("arbitrary", "arbitrary")),
    )(w_in_t, w_in_t)


def _rope_tables():
    rows = DEC_SEQ // GRID_W
    row = jnp.repeat(jnp.arange(rows), GRID_W).astype(F32)
    col = jnp.tile(jnp.arange(GRID_W), rows).astype(F32)
    n_freq = ROPE_DIM // 4
    inv = ROPE_THETA ** (-jnp.arange(n_freq, dtype=F32) / n_freq)
    ang = jnp.concatenate([row[:, None] * inv, col[:, None] * inv], axis=-1)
    cos, sin = jnp.cos(ang), jnp.sin(ang)
    pad = jnp.zeros((DEC_SEQ, LANES - ROPE_DIM), F32)
    cos_t = jnp.concatenate([cos, cos, pad], axis=-1)
    sin_t = jnp.concatenate([-sin, sin, pad], axis=-1)
    ident = 256
    cos_t = jnp.concatenate([cos_t, jnp.ones((ident, LANES), F32)], axis=0)
    sin_t = jnp.concatenate([sin_t, jnp.zeros((ident, LANES), F32)], axis=0)
    return cos_t, sin_t


def _pair_state_in(state):
    b = state.shape[0]
    s = state.reshape(b, 2, N_PAIR, 2, HEAD_A, HEAD_A)
    z = jnp.zeros_like(s[:, :, :, 0])
    top = jnp.concatenate([s[:, :, :, 0], z], axis=-1)
    bot = jnp.concatenate([z, s[:, :, :, 1]], axis=-1)
    return jnp.concatenate([top, bot], axis=-2)


def kernel(x_prompt, x_sample, c, state_rwkv, cache_mla_ckv, cache_mla_kpe, c_ctx, w_mod, b_mod, norm_mix_w, w_in, rwkv_w0, rwkv_w_lora_b, rwkv_a0, rwkv_a_lora_b, rwkv_g_lora_b, rwkv_k_k, rwkv_k_a, rwkv_r_k, rwkv_ln_w, rwkv_ln_b, mla_q_norm_w, mla_w_uq, mla_kv_norm_w, mla_w_ukv, mla_q_head_norm, mla_k_head_norm, w_o_rwkv, w_o_mla, w_out, norm_ffn_w, w_ffn_in, w_ffn_out):
    l = 0
    xp = x_prompt.reshape(N_PROMPT, D_MODEL)
    xs = x_sample.reshape(N_SAMPLE, D_MODEL)

    w_in_t = _w_in_relayout(jnp.swapaxes(w_in[l], 0, 1))
    w_uq_p = jnp.pad(mla_w_uq[l].reshape(Q_LORA, H_B, QK_DIM), ((0, 0), (0, 0), (0, QK_PAD - QK_DIM)))
    w_uq_p = w_uq_p.reshape(Q_LORA, H_B * QK_PAD).astype(BF)
    q_head_w_p = jnp.pad(mla_q_head_norm[l], (0, QK_PAD - QK_DIM)).reshape(1, QK_PAD)
    wkv3 = mla_w_ukv[l].reshape(KV_LORA, H_B, NOPE_DIM + V_DIM)
    w_ukv_p = jnp.concatenate([wkv3[:, :, :NOPE_DIM].reshape(KV_LORA, H_B * NOPE_DIM),
                               wkv3[:, :, NOPE_DIM:].reshape(KV_LORA, H_B * V_DIM)], axis=1).astype(BF)
    wkn = mla_k_head_norm[l, :NOPE_DIM].reshape(1, NOPE_DIM)
    wkr = jnp.pad(mla_k_head_norm[l, NOPE_DIM:], (0, LANES - ROPE_DIM)).reshape(1, LANES)
    w_oa = w_o_rwkv[l].astype(BF)
    w_ob = w_o_mla[l].astype(BF)
    w_out_b = w_out[l].astype(BF)
    wg_p = jnp.pad(rwkv_g_lora_b[l], ((0, R_G_PAD - R_G), (0, 0)))
    row = lambda a: a.reshape(1, D_A)
    wkv_w = (rwkv_w0[l], rwkv_w_lora_b[l], rwkv_a0[l], rwkv_a_lora_b[l], wg_p, row(rwkv_k_k[l]), row(rwkv_k_a[l]),
             row(rwkv_r_k[l]), row(rwkv_ln_w[l]), row(rwkv_ln_b[l]))
    cos_t, sin_t = _rope_tables()

    cond8 = jnp.concatenate([c_ctx[None, :], c, jnp.zeros((8 - 1 - DEC_BATCH, D_MODEL), F32)], axis=0)
    mod = _modulation(cond8, w_mod[l], b_mod[l]).reshape(8, 6, D_MODEL)

    proj, w_ffn_in_p = _in_proj(_norm_mod((xp, xs), norm_mix_w[l], mod, 0, 1), w_in_t, w_ffn_in[l])

    ya_p, s_new = _wkv(proj, 0, BATCH, SEQ, wkv_w, None, WKV_PAIRS_PROMPT)
    ya_s = _wkv(proj, N_PROMPT, DEC_BATCH, DEC_SEQ, wkv_w, _pair_state_in(state_rwkv[:, l]), WKV_PAIRS_SAMPLE)[0]

    q = _q_proj(proj, mla_q_norm_w[l], w_uq_p, q_head_w_p, cos_t, sin_t)
    k, v, ckv = _kv_proj(proj, C_KVLAT, proj, C_KPE, N_TOK, mla_kv_norm_w[l], w_ukv_p, wkn, wkr, cos_t, sin_t,
                         _rope_block, True)
    ctx_ckv = cache_mla_ckv[:, l].reshape(DEC_BATCH * PAST_LEN, KV_LORA)
    ctx_kpe = jnp.pad(cache_mla_kpe[:, l].reshape(DEC_BATCH * PAST_LEN, ROPE_DIM), ((0, 0), (0, LANES - ROPE_DIM)))
    k_ctx, v_ctx = _kv_proj(ctx_ckv, 0, ctx_kpe, 0, DEC_BATCH * PAST_LEN, mla_kv_norm_w[l], w_ukv_p, wkn, wkr,
                            cos_t, sin_t, lambda i, tm: DEC_SEQ // tm, False)
    yb_p = _attn_prompt(q, k, v)
    yb_s = _attn_sample(q, k, v, k_ctx, v_ctx)

    mixed = _merge((ya_p, ya_s), (yb_p, yb_s), w_oa, w_ob, proj)
    x1 = _resid_proj(mixed, w_out_b, (xp, xs), mod, 2, 512, D_MODEL, 0, N_TOK)
    act, w_ffn_out_p = _ffn_in(_norm_mod((x1,), norm_ffn_w[l], mod, 3, 4), w_ffn_in_p, w_ffn_out[l])
    tk_ffn = D_FF_PAD // 4
    y_p = _resid_proj(act, w_ffn_out_p, (x1,), mod, 5, 1024, tk_ffn, 0, N_PROMPT).reshape(BATCH, SEQ, D_MODEL)
    y_s = _resid_proj(act, w_ffn_out_p, (x1,), mod, 5, 1024, tk_ffn, N_PROMPT, N_SAMPLE)
    y_s = y_s.reshape(DEC_BATCH, DEC_SEQ, D_MODEL)
    new_state = s_new[:, None]
    new_ckv = ckv[:N_PROMPT].reshape(BATCH, 1, SEQ, KV_LORA)
    new_kpe = proj[:N_PROMPT, C_KPE:C_KPE + ROPE_DIM].reshape(BATCH, 1, SEQ, ROPE_DIM)
    return y_p, y_s, new_state, new_ckv, new_kpe
```

```python
import functools

import numpy as np
import jax
import jax.numpy as jnp
from jax import lax
from jax.experimental import pallas as pl
from jax.experimental.pallas import tpu as pltpu

F32 = jnp.float32
BF = jnp.bfloat16

D_MODEL = 4096
BATCH, SEQ = 32, 256
DEC_BATCH, DEC_SEQ, PAST_LEN = 4, 1024, 512
GRID_W = 64
D_A, HEAD_A = 2048, 64
H_A = D_A // HEAD_A
R_LORA, R_G = 128, 480
LN_X_EPS = 64e-5
H_B, Q_LORA, KV_LORA = 16, 1024, 512
NOPE_DIM, ROPE_DIM, V_DIM = 128, 64, 128
QK_DIM = NOPE_DIM + ROPE_DIM
ROPE_THETA = 10000.0
D_FF = 11008
NORM_EPS = 1e-6

N_PROMPT = BATCH * SEQ
N_SAMPLE = DEC_BATCH * DEC_SEQ
N_TOK = N_PROMPT + N_SAMPLE

LANES = 128
R_G_PAD = 512
QK_PAD = 256
D_FF_PAD = 11264
FF_BLK = D_FF_PAD - D_FF
C_R, C_K, C_V = 0, 2048, 4096
C_LORA, C_LG, C_QLAT, C_KVLAT, C_KPE = 6144, 6656, 7168, 8192, 8704
C_GLA, C_GLB, D_IN_PAD = 9216, 13312, 17408
WKV_CHUNK = 64
WKV_UNROLL = 8
WKV_PAIRS_PROMPT = 4
WKV_PAIRS_SAMPLE = 2
DECAY_SCALE = float(np.exp(-0.5))
PAIR = 2 * HEAD_A
N_PAIR = D_A // PAIR

VMEM_LIMIT = 56 * 2**20


def _cparams(sem):
    return pltpu.CompilerParams(dimension_semantics=sem, vmem_limit_bytes=VMEM_LIMIT)


def _dot(a, b):
    return jnp.dot(a, b, preferred_element_type=F32)


def _dot_nt(a, b):
    return lax.dot_general(a, b, (((1,), (1,)), ((), ())), preferred_element_type=F32)


def _dot_tn(a, b):
    return lax.dot_general(a, b, (((0,), (0,)), ((), ())), preferred_element_type=F32)


def _mod_row(i, tm):
    n_prompt_tiles = N_PROMPT // tm
    tiles_per_seq = DEC_SEQ // tm
    return jnp.where(i < n_prompt_tiles, 0, 1 + (i - n_prompt_tiles) // tiles_per_seq)


def _mod_kernel(c_ref, w_ref, b_ref, o_ref):
    c = c_ref[...]
    s = c * jax.nn.sigmoid(c)
    o_ref[...] = _dot(s.astype(BF), w_ref[...].astype(BF)) + b_ref[...]


def _modulation(cond8, w_mod, b_mod):
    tn = 512
    n = w_mod.shape[1]
    return pl.pallas_call(
        _mod_kernel,
        out_shape=jax.ShapeDtypeStruct((8, n), F32),
        grid=(n // tn,),
        in_specs=[pl.BlockSpec((8, D_MODEL), lambda j: (0, 0)),
                  pl.BlockSpec((D_MODEL, tn), lambda j: (0, j)),
                  pl.BlockSpec((1, tn), lambda j: (0, j))],
        out_specs=pl.BlockSpec((8, tn), lambda j: (0, j)),
        compiler_params=_cparams(("arbitrary",)),
    )(cond8, w_mod, b_mod.reshape(1, n))


NORM_ROWS = 16


NORM_COLS = 512


def _normmod(x_ref, nw_ref, mod_ref, o_ref, gs_ref, shift_idx, scale_idx):
    gs_ref[0] = jnp.broadcast_to(nw_ref[...] * (1.0 + mod_ref[scale_idx:scale_idx + 1, :]), (8, D_MODEL))
    gs_ref[1] = jnp.broadcast_to(mod_ref[shift_idx:shift_idx + 1, :], (8, D_MODEL))
    reps = NORM_ROWS // 8

    def body(r, carry):
        rows = pl.ds(pl.multiple_of(r * NORM_ROWS, NORM_ROWS), NORM_ROWS)
        accs = [jnp.zeros((NORM_ROWS, LANES), F32) for _ in range(4)]
        for c in range(D_MODEL // LANES):
            xc = x_ref[rows, c * LANES:(c + 1) * LANES]
            accs[c % 4] = accs[c % 4] + xc * xc
        acc = (accs[0] + accs[1]) + (accs[2] + accs[3])
        inv = lax.rsqrt(jnp.sum(acc, axis=-1, keepdims=True) * (1.0 / D_MODEL) + NORM_EPS)
        for c in range(D_MODEL // NORM_COLS):
            cs = slice(c * NORM_COLS, (c + 1) * NORM_COLS)
            gain = jnp.concatenate([gs_ref[0, :, cs]] * reps, axis=0)
            shift = jnp.concatenate([gs_ref[1, :, cs]] * reps, axis=0)
            o_ref[rows, cs] = (x_ref[rows, cs] * inv * gain + shift).astype(BF)
        return carry

    lax.fori_loop(0, x_ref.shape[0] // NORM_ROWS, body, 0, unroll=4)


def _split_specs(tm, width, col):
    npt = N_PROMPT // tm
    return [pl.BlockSpec((tm, width), lambda i, *_: (jnp.minimum(i, npt - 1), col(*_))),
            pl.BlockSpec((tm, width), lambda i, *_: (jnp.maximum(i - npt, 0), col(*_)))]


def _norm_mod_kernel(*refs, shift_idx, scale_idx, n_prompt_tiles):
    *x_refs, nw_ref, mod_ref, o_ref, gs_ref = refs

    def run(x_ref):
        _normmod(x_ref, nw_ref, mod_ref, o_ref, gs_ref, shift_idx, scale_idx)

    if len(x_refs) == 1:
        run(x_refs[0])
    else:
        is_prompt = pl.program_id(0) < n_prompt_tiles
        pl.when(is_prompt)(lambda: run(x_refs[0]))
        pl.when(jnp.logical_not(is_prompt))(lambda: run(x_refs[1]))


def _norm_mod(xs, norm_w, mod, shift_idx, scale_idx):
    tm = 256
    if len(xs) == 1:
        x_specs = [pl.BlockSpec((tm, D_MODEL), lambda i: (i, 0))]
    else:
        x_specs = _split_specs(tm, D_MODEL, lambda: 0)
    return pl.pallas_call(
        functools.partial(_norm_mod_kernel, shift_idx=shift_idx, scale_idx=scale_idx, n_prompt_tiles=N_PROMPT // tm),
        out_shape=jax.ShapeDtypeStruct((N_TOK, D_MODEL), BF),
        grid=(N_TOK // tm,),
        in_specs=x_specs + [pl.BlockSpec((1, D_MODEL), lambda i: (0, 0)),
                            pl.BlockSpec((None, 6, D_MODEL), lambda i: (_mod_row(i, tm), 0, 0))],
        out_specs=pl.BlockSpec((tm, D_MODEL), lambda i: (i, 0)),
        scratch_shapes=[pltpu.VMEM((2, 8, D_MODEL), F32)],
        compiler_params=_cparams(("parallel",)),
    )(*xs, norm_w.reshape(1, D_MODEL), mod)


CAST_SPLIT = 2


class _CastJob:
    def __init__(self, w, axis, blk, n_real, n_seg, nj):
        self.w, self.axis, self.blk, self.n_real, self.nj = w, axis, blk, n_real, nj
        self.n_parts = n_seg * (n_real + 1) * CAST_SPLIT
        other = w.shape[1 - axis]
        part = other // CAST_SPLIT
        self.block = (blk, part) if axis == 0 else (part, blk)
        padded = n_seg * (n_real + 1) * blk
        self.out_shape = jax.ShapeDtypeStruct((padded, other) if axis == 0 else (other, padded), BF)

    def _place(self, b, part):
        return (b, part) if self.axis == 0 else (part, b)

    def _block_of(self, i, j):
        t = jnp.minimum(i * self.nj + j, self.n_parts - 1)
        return t // CAST_SPLIT, t % CAST_SPLIT

    def in_spec(self):
        def idx(i, j):
            b, part = self._block_of(i, j)
            seg, jj = b // (self.n_real + 1), b % (self.n_real + 1)
            return self._place(seg * self.n_real + jnp.minimum(jj, self.n_real - 1), part)
        return pl.BlockSpec(self.block, idx)

    def out_spec(self):
        return pl.BlockSpec(self.block, lambda i, j: self._place(*self._block_of(i, j)))

    def step(self, x_ref, o_ref):
        t = pl.program_id(0) * self.nj + pl.program_id(1)
        is_pad = (t // CAST_SPLIT) % (self.n_real + 1) == self.n_real
        active = t < self.n_parts

        @pl.when(active & is_pad)
        def _():
            o_ref[...] = jnp.zeros_like(o_ref)

        @pl.when(active & jnp.logical_not(is_pad))
        def _():
            o_ref[...] = x_ref[...].astype(BF)


def _in_proj_kernel(h_ref, wt_ref, wc_ref, o_ref, oc_ref, *, job):
    o_ref[...] = _dot_nt(h_ref[...], wt_ref[...])
    job.step(wc_ref, oc_ref)


def _in_proj(h, w_in_t, w_ffn_in):
    tm, tn = 1024, 1024
    nj = D_IN_PAD // tn
    job = _CastJob(w_ffn_in, 1, FF_BLK, D_FF // FF_BLK, 2, nj)
    return pl.pallas_call(
        functools.partial(_in_proj_kernel, job=job),
        out_shape=[jax.ShapeDtypeStruct((N_TOK, D_IN_PAD), F32), job.out_shape],
        grid=(N_TOK // tm, nj),
        in_specs=[pl.BlockSpec((tm, D_MODEL), lambda i, j: (i, 0)),
                  pl.BlockSpec((tn, D_MODEL), lambda i, j: (j, 0)),
                  job.in_spec()],
        out_specs=[pl.BlockSpec((tm, tn), lambda i, j: (i, j)), job.out_spec()],
        compiler_params=_cparams(("arbitrary", "arbitrary")),
    )(h, w_in_t, w_ffn_in)


def _ffn_in_kernel(h_ref, wg_ref, wu_ref, wc_ref, o_ref, oc_ref, *, job):
    h = h_ref[...]
    g = _dot(h, wg_ref[...])
    u = _dot(h, wu_ref[...])
    o_ref[...] = (g * jax.nn.sigmoid(g) * u).astype(BF)
    job.step(wc_ref, oc_ref)


def _ffn_in(h, w_ffn_in_p, w_ffn_out):
    tm, tn = 1024, 512
    nj = D_FF_PAD // tn
    job = _CastJob(w_ffn_out, 0, FF_BLK, D_FF // FF_BLK, 1, nj)
    return pl.pallas_call(
        functools.partial(_ffn_in_kernel, job=job),
        out_shape=[jax.ShapeDtypeStruct((N_TOK, D_FF_PAD), BF), job.out_shape],
        grid=(N_TOK // tm, nj),
        in_specs=[pl.BlockSpec((tm, D_MODEL), lambda i, j: (i, 0)),
                  pl.BlockSpec((D_MODEL, tn), lambda i, j: (0, j)),
                  pl.BlockSpec((D_MODEL, tn), lambda i, j: (0, j + nj)),
                  job.in_spec()],
        out_specs=[pl.BlockSpec((tm, tn), lambda i, j: (i, j)), job.out_spec()],
        compiler_params=_cparams(("arbitrary", "arbitrary")),
    )(h, w_ffn_in_p, w_ffn_in_p, w_ffn_out)


def _resid_proj_kernel(a_ref, w_ref, *refs, gate_idx, n_prompt_tiles, nk):
    if nk == 1:
        *x_refs, mod_ref, o_ref = refs
    else:
        *x_refs, mod_ref, o_ref, acc_ref = refs
    part = _dot(a_ref[...], w_ref[...])
    last = True
    if nk > 1:
        k = pl.program_id(2)
        last = k == nk - 1

        @pl.when(k == 0)
        def _():
            acc_ref[...] = part

        @pl.when((k > 0) & jnp.logical_not(last))
        def _():
            acc_ref[...] += part

    def finish(x_ref):
        total = part if nk == 1 else acc_ref[...] + part
        o_ref[...] = x_ref[...] + mod_ref[gate_idx:gate_idx + 1, :] * total

    if len(x_refs) == 1:
        pl.when(jnp.asarray(last))(lambda: finish(x_refs[0]))
    else:
        is_prompt = pl.program_id(0) < n_prompt_tiles
        pl.when(last & is_prompt)(lambda: finish(x_refs[0]))
        pl.when(last & jnp.logical_not(is_prompt))(lambda: finish(x_refs[1]))


def _resid_proj(a, w, xs, mod, gate_idx, tn, tk, row0, n_rows):
    tm = 1024
    kdim = a.shape[1]
    nk = kdim // tk
    r0 = row0 // tm
    if len(xs) == 1:
        x_specs = [pl.BlockSpec((tm, tn), lambda i, j, k: (r0 + i, j))]
    else:
        x_specs = _split_specs(tm, tn, lambda j, k: j)
    return pl.pallas_call(
        functools.partial(_resid_proj_kernel, gate_idx=gate_idx, n_prompt_tiles=N_PROMPT // tm, nk=nk),
        out_shape=jax.ShapeDtypeStruct((n_rows, D_MODEL), F32),
        grid=(n_rows // tm, D_MODEL // tn, nk),
        in_specs=[pl.BlockSpec((tm, tk), lambda i, j, k: (r0 + i, k)),
                  pl.BlockSpec((tk, tn), lambda i, j, k: (k, j))] + x_specs + [
                  pl.BlockSpec((None, 6, tn), lambda i, j, k: (_mod_row(r0 + i, tm), 0, j))],
        out_specs=pl.BlockSpec((tm, tn), lambda i, j, k: (i, j)),
        scratch_shapes=[pltpu.VMEM((tm, tn), F32)] if nk > 1 else [],
        compiler_params=_cparams(("parallel", "parallel", "arbitrary")),
    )(a, w, *xs, mod)


def _merge_kernel(yap_ref, yas_ref, ybp_ref, ybs_ref, wa_ref, wb_ref, gla_ref, glb_ref, o_ref, *, n_prompt_tiles):
    is_prompt = pl.program_id(1) < n_prompt_tiles
    ya = jnp.where(is_prompt, yap_ref[...], yas_ref[...])
    yb = jnp.where(is_prompt, ybp_ref[...], ybs_ref[...])
    pa = _dot(ya, wa_ref[...])
    pb = _dot(yb, wb_ref[...])
    o_ref[...] = (jax.nn.sigmoid(gla_ref[...]) * pa + jax.nn.sigmoid(glb_ref[...]) * pb).astype(BF)


def _merge(y_a, y_b, w_oa, w_ob, proj):
    tm, tn = 512, 1024
    npt = N_PROMPT // tm
    y_specs = [pl.BlockSpec((tm, D_A), lambda j, i: (jnp.minimum(i, npt - 1), 0)),
               pl.BlockSpec((tm, D_A), lambda j, i: (jnp.maximum(i - npt, 0), 0))]
    return pl.pallas_call(
        functools.partial(_merge_kernel, n_prompt_tiles=npt),
        out_shape=jax.ShapeDtypeStruct((N_TOK, D_MODEL), BF),
        grid=(D_MODEL // tn, N_TOK // tm),
        in_specs=y_specs + y_specs + [
                  pl.BlockSpec((D_A, tn), lambda j, i: (0, j)),
                  pl.BlockSpec((D_A, tn), lambda j, i: (0, j)),
                  pl.BlockSpec((tm, tn), lambda j, i: (i, C_GLA // tn + j)),
                  pl.BlockSpec((tm, tn), lambda j, i: (i, C_GLB // tn + j))],
        out_specs=pl.BlockSpec((tm, tn), lambda j, i: (i, j)),
        compiler_params=_cparams(("parallel", "arbitrary")),
    )(*y_a, *y_b, w_oa, w_ob, proj, proj)


def _rope(x, cos_ref, sin_ref):
    swapped = pltpu.roll(x, ROPE_DIM // 2, axis=1) + pltpu.roll(x, LANES - ROPE_DIM // 2, axis=1)
    return x * cos_ref[...] + swapped * sin_ref[...]


def _rope_block(i, tm):
    n_prompt_tiles = N_PROMPT // tm
    tiles_per_seq = DEC_SEQ // tm
    return jnp.where(i < n_prompt_tiles, tiles_per_seq, (i - n_prompt_tiles) % tiles_per_seq)


def _q_kernel(ql_ref, nw_ref, w_ref, hw_ref, cos_ref, sin_ref, o_ref):
    x = ql_ref[...]
    y = x * lax.rsqrt(jnp.mean(x * x, axis=-1, keepdims=True) + NORM_EPS) * nw_ref[...]
    q = _dot(y.astype(BF), w_ref[...])
    hw = hw_ref[...]
    for h in range(H_B):
        qh = q[:, h * QK_PAD:(h + 1) * QK_PAD]
        inv = lax.rsqrt(jnp.sum(qh * qh, axis=-1, keepdims=True) * (1.0 / QK_DIM) + NORM_EPS)
        qn = qh * inv * hw
        o_ref[:, h * QK_PAD:h * QK_PAD + NOPE_DIM] = qn[:, :NOPE_DIM].astype(BF)
        o_ref[:, h * QK_PAD + NOPE_DIM:(h + 1) * QK_PAD] = _rope(qn[:, NOPE_DIM:], cos_ref, sin_ref).astype(BF)


def _q_proj(proj, q_norm_w, w_uq_p, q_head_w_p, cos_t, sin_t):
    tm = 256
    return pl.pallas_call(
        _q_kernel,
        out_shape=jax.ShapeDtypeStruct((N_TOK, H_B * QK_PAD), BF),
        grid=(N_TOK // tm,),
        in_specs=[pl.BlockSpec((tm, Q_LORA), lambda i: (i, C_QLAT // Q_LORA)),
                  pl.BlockSpec((1, Q_LORA), lambda i: (0, 0)),
                  pl.BlockSpec((Q_LORA, H_B * QK_PAD), lambda i: (0, 0)),
                  pl.BlockSpec((1, QK_PAD), lambda i: (0, 0)),
                  pl.BlockSpec((tm, LANES), lambda i: (_rope_block(i, tm), 0)),
                  pl.BlockSpec((tm, LANES), lambda i: (_rope_block(i, tm), 0))],
        out_specs=pl.BlockSpec((tm, H_B * QK_PAD), lambda i: (i, 0)),
        compiler_params=_cparams(("parallel",)),
    )(proj, q_norm_w.reshape(1, Q_LORA), w_uq_p, q_head_w_p, cos_t, sin_t)


def _kv_kernel(lat_ref, kpe_ref, nw_ref, w_ref, wkn_ref, wkr_ref, cos_ref, sin_ref, *out_refs, pre_norm):
    if pre_norm:
        k_ref, v_ref, ckv_ref = out_refs
        x = lat_ref[...]
        ckv = x * lax.rsqrt(jnp.mean(x * x, axis=-1, keepdims=True) + NORM_EPS) * nw_ref[...]
        ckv_ref[...] = ckv
    else:
        k_ref, v_ref = out_refs
        ckv = lat_ref[...]
    kv = _dot(ckv.astype(BF), w_ref[...])
    v_ref[...] = kv[:, H_B * NOPE_DIM:].astype(BF)
    kpe = kpe_ref[...]
    kpe_ss = jnp.sum(kpe * kpe, axis=-1, keepdims=True)
    kpe_rot = _rope(kpe * wkr_ref[...], cos_ref, sin_ref)
    wkn = wkn_ref[...]
    for h in range(H_B):
        kn = kv[:, h * NOPE_DIM:(h + 1) * NOPE_DIM]
        inv = lax.rsqrt((jnp.sum(kn * kn, axis=-1, keepdims=True) + kpe_ss) * (1.0 / QK_DIM) + NORM_EPS)
        k_ref[:, h * QK_PAD:h * QK_PAD + NOPE_DIM] = (kn * inv * wkn).astype(BF)
        k_ref[:, h * QK_PAD + NOPE_DIM:(h + 1) * QK_PAD] = (kpe_rot * inv).astype(BF)


def _kv_proj(lat, lat_col, kpe, kpe_col, n_rows, kv_norm_w, w_ukv_p, wkn, wkr, cos_t, sin_t, rope_block, pre_norm):
    tm = 256
    out_shape = [jax.ShapeDtypeStruct((n_rows, H_B * QK_PAD), BF),
                 jax.ShapeDtypeStruct((n_rows, H_B * V_DIM), BF)]
    out_specs = [pl.BlockSpec((tm, H_B * QK_PAD), lambda i: (i, 0)),
                 pl.BlockSpec((tm, H_B * V_DIM), lambda i: (i, 0))]
    if pre_norm:
        out_shape.append(jax.ShapeDtypeStruct((n_rows, KV_LORA), F32))
        out_specs.append(pl.BlockSpec((tm, KV_LORA), lambda i: (i, 0)))
    return pl.pallas_call(
        functools.partial(_kv_kernel, pre_norm=pre_norm),
        out_shape=out_shape,
        grid=(n_rows // tm,),
        in_specs=[pl.BlockSpec((tm, KV_LORA), lambda i: (i, lat_col // KV_LORA)),
                  pl.BlockSpec((tm, LANES), lambda i: (i, kpe_col // LANES)),
                  pl.BlockSpec((1, KV_LORA), lambda i: (0, 0)),
                  pl.BlockSpec((KV_LORA, H_B * (NOPE_DIM + V_DIM)), lambda i: (0, 0)),
                  pl.BlockSpec((1, NOPE_DIM), lambda i: (0, 0)),
                  pl.BlockSpec((1, LANES), lambda i: (0, 0)),
                  pl.BlockSpec((tm, LANES), lambda i: (rope_block(i, tm), 0)),
                  pl.BlockSpec((tm, LANES), lambda i: (rope_block(i, tm), 0))],
        out_specs=out_specs,
        compiler_params=_cparams(("parallel",)),
    )(lat, kpe, kv_norm_w.reshape(1, KV_LORA), w_ukv_p, wkn, wkr, cos_t, sin_t)


_ATTN_SCALE = QK_DIM ** -0.5


def _attn_prompt_kernel(q_ref, k_ref, v_ref, o_ref, *, heads):
    for h in range(heads):
        q = q_ref[:, h * QK_PAD:(h + 1) * QK_PAD]
        k = k_ref[:, h * QK_PAD:(h + 1) * QK_PAD]
        s = _dot_nt(q, k) * _ATTN_SCALE
        e = jnp.exp(s - jnp.max(s, axis=-1, keepdims=True))
        p = e * (1.0 / jnp.sum(e, axis=-1, keepdims=True))
        o_ref[:, h * V_DIM:(h + 1) * V_DIM] = _dot(p.astype(BF), v_ref[:, h * V_DIM:(h + 1) * V_DIM]).astype(BF)


def _attn_prompt(q, k, v):
    heads = 8
    return pl.pallas_call(
        functools.partial(_attn_prompt_kernel, heads=heads),
        out_shape=jax.ShapeDtypeStruct((N_PROMPT, H_B * V_DIM), BF),
        grid=(BATCH, H_B // heads),
        in_specs=[pl.BlockSpec((SEQ, heads * QK_PAD), lambda b, g: (b, g)),
                  pl.BlockSpec((SEQ, heads * QK_PAD), lambda b, g: (b, g)),
                  pl.BlockSpec((SEQ, heads * V_DIM), lambda b, g: (b, g))],
        out_specs=pl.BlockSpec((SEQ, heads * V_DIM), lambda b, g: (b, g)),
        compiler_params=_cparams(("parallel", "parallel")),
    )(q, k, v)


def _attn_sample_kernel(q_ref, kc_ref, vc_ref, ks_ref, vs_ref, o_ref, *, heads):
    for h in range(heads):
        qk = slice(h * QK_PAD, (h + 1) * QK_PAD)
        vv = slice(h * V_DIM, (h + 1) * V_DIM)
        q = q_ref[:, qk]
        s1 = _dot_nt(q, kc_ref[:, qk]) * _ATTN_SCALE
        s2 = _dot_nt(q, ks_ref[:, qk]) * _ATTN_SCALE
        m = jnp.maximum(jnp.max(s1, axis=-1, keepdims=True), jnp.max(s2, axis=-1, keepdims=True))
        e1 = jnp.exp(s1 - m)
        e2 = jnp.exp(s2 - m)
        inv = 1.0 / (jnp.sum(e1, axis=-1, keepdims=True) + jnp.sum(e2, axis=-1, keepdims=True))
        o = _dot((e1 * inv).astype(BF), vc_ref[:, vv]) + _dot((e2 * inv).astype(BF), vs_ref[:, vv])
        o_ref[:, vv] = o.astype(BF)


def _attn_sample(q, k, v, k_ctx, v_ctx):
    tq = 512
    heads = 4
    qt = DEC_SEQ // tq
    q0 = N_PROMPT // tq
    s0 = N_PROMPT // DEC_SEQ
    return pl.pallas_call(
        functools.partial(_attn_sample_kernel, heads=heads),
        out_shape=jax.ShapeDtypeStruct((N_SAMPLE, H_B * V_DIM), BF),
        grid=(DEC_BATCH, H_B // heads, qt),
        in_specs=[pl.BlockSpec((tq, heads * QK_PAD), lambda b, g, t: (q0 + b * qt + t, g)),
                  pl.BlockSpec((PAST_LEN, heads * QK_PAD), lambda b, g, t: (b, g)),
                  pl.BlockSpec((PAST_LEN, heads * V_DIM), lambda b, g, t: (b, g)),
                  pl.BlockSpec((DEC_SEQ, heads * QK_PAD), lambda b, g, t: (s0 + b, g)),
                  pl.BlockSpec((DEC_SEQ, heads * V_DIM), lambda b, g, t: (s0 + b, g))],
        out_specs=pl.BlockSpec((tq, heads * V_DIM), lambda b, g, t: (b * qt + t, g)),
        compiler_params=_cparams(("parallel", "parallel", "arbitrary")),
    )(q, k_ctx, v_ctx, k, v)


def _head_sum(x, ones_blk):
    hi = x.astype(BF)
    lo = (x - hi.astype(F32)).astype(BF)
    return _dot(hi, ones_blk) + _dot(lo, ones_blk)


def _wkv_chunks_a(items, strict, incl, lane_lo, eye):
    C = WKV_CHUNK
    n2 = 2 * C

    def stack(x):
        return jnp.concatenate([jnp.where(lane_lo, x, 0.0), jnp.where(lane_lo, 0.0, x)], axis=0)

    row_c = lax.broadcasted_iota(jnp.int32, (C, PAIR), 0)

    def cumsum(it):
        x = it["lw"]
        s = 1
        while s < C:
            if it["d"] == 0:
                x = x + jnp.where(row_c >= s, pltpu.roll(x, s, axis=0), 0.0)
            else:
                x = x + jnp.where(row_c < C - s, pltpu.roll(x, C - s, axis=0), 0.0)
            s *= 2
        return x

    cs = [cumsum(it) for it in items]

    def operands(it, c):
        cl = c[C - 1:C, :] if it["d"] == 0 else c[0:1, :]
        e_inc = jnp.exp(c)
        e_neg = jnp.exp(-c)
        e_exc = jnp.exp(c - it["lw"])
        e_end = jnp.exp(cl - c)
        b, kd = it["b"], it["kd"]
        return dict(d=it["d"], e=jnp.exp(cl),
                    a_t=stack(-it["kk"] * e_exc).astype(BF), r_t=stack(it["r"] * e_inc),
                    b_t=stack(b * e_neg).astype(BF), k_t=stack(kd * e_neg).astype(BF),
                    bk_h=jnp.concatenate([stack(b * e_end), stack(kd * e_end)], axis=0).astype(BF),
                    v_s=stack(it["v"]).astype(BF))

    ops = [operands(it, c) for it, c in zip(items, cs)]
    scs = [_dot_nt(jnp.concatenate([o["a_t"], o["r_t"].astype(BF)], axis=0),
                   jnp.concatenate([o["b_t"], o["k_t"]], axis=0)) for o in ops]
    for o, sc in zip(ops, scs):
        d = o["d"]
        o["lp"] = jnp.where(strict[d], sc[:n2, :n2], 0.0).astype(BF)
        o["l_ak"] = jnp.where(strict[d], sc[:n2, n2:], 0.0).astype(BF)
        o["a_r"] = jnp.concatenate([jnp.where(incl[d], sc[n2:, :n2], 0.0),
                                    jnp.where(incl[d], sc[n2:, n2:], 0.0)], axis=1).astype(BF)

    ts = [eye + o["lp"].astype(F32) for o in ops]
    lps = [_dot(o["lp"], o["lp"]).astype(BF) for o in ops]
    span = 2
    while 2 * span < C:
        rs = [_dot(lp, jnp.concatenate([t.astype(BF), lp], axis=1)) for t, lp in zip(ts, lps)]
        ts = [t + r[:, :n2] for t, r in zip(ts, rs)]
        lps = [r[:, n2:].astype(BF) for r in rs]
        span *= 2
    ts = [t + _dot(lp, t.astype(BF)) for t, lp in zip(ts, lps)]

    lakvs = [_dot(o["l_ak"], o["v_s"]) for o in ops]
    wus = [_dot(t.astype(BF), jnp.concatenate([o["a_t"], lakv.astype(BF)], axis=1))
           for t, o, lakv in zip(ts, ops, lakvs)]
    zero = jnp.zeros((n2, n2), BF)
    outs = []
    for o, wu in zip(ops, wus):
        rhs = jnp.concatenate([wu.astype(BF), jnp.concatenate([zero, o["v_s"]], axis=1)], axis=0)
        qy = _dot(o["a_r"], rhs)
        mn = _dot_tn(o["bk_h"], rhs)
        q = (o["r_t"] + qy[:, :n2]).astype(BF)
        y0 = qy[:C, n2:] + qy[C:, n2:]
        outs.append((q, y0, mn[:, :n2].astype(BF), mn[:, n2:].T, o["e"]))
    return outs


def _wkv_kernel(*refs, T, n_pairs, has_s0, out_state):
    (r_ref, k_ref, v_ref, lora_ref, lg_ref, w0_ref, wl_ref, a0_ref, wa_ref, wg_ref,
     kkw_ref, ka_ref, rk_ref, lnw_ref, lnb_ref) = refs[:15]
    pos = 15
    if has_s0:
        s0_ref = refs[pos]
        pos += 1
    y_ref = refs[pos]
    pos += 1
    if out_state:
        so_ref = refs[pos]
        pos += 1
    lw_s, b_s, kd_s, kk_s, y_s, q_s, m_s, n_s, e_s = refs[pos:]

    C = WKV_CHUNK
    nc = T // C
    lane = lax.broadcasted_iota(jnp.int32, (PAIR, PAIR), 1)
    row = lax.broadcasted_iota(jnp.int32, (PAIR, PAIR), 0)
    ones_blk = jnp.where((lane // HEAD_A) == (row // HEAD_A), 1.0, 0.0).astype(BF)
    eye = jnp.where(lane == row, 1.0, 0.0)
    lane_lo = lax.broadcasted_iota(jnp.int32, (C, PAIR), 1) < HEAD_A
    strict = (lane < row, lane > row)
    incl = (lane <= row, lane >= row)
    pairs = range(n_pairs)

    def cols(pp):
        return slice(pp * PAIR, (pp + 1) * PAIR)

    for pp in pairs:
        k = k_ref[:, cols(pp)]
        kk = k * kkw_ref[:, cols(pp)]
        kk = kk * lax.rsqrt(_head_sum(kk * kk, ones_blk) + 1e-12)
        kk_s[pp] = kk
        for d in range(2):
            lw_in = lora_ref[:, d * R_LORA:(d + 1) * R_LORA]
            la_in = lora_ref[:, (2 + d) * R_LORA:(3 + d) * R_LORA]
            x = w0_ref[d:d + 1, cols(pp)] + _dot(jnp.tanh(lw_in).astype(BF), wl_ref[d, :, cols(pp)].astype(BF))
            lw_s[2 * pp + d] = -DECAY_SCALE * jax.nn.sigmoid(x)
            a = jax.nn.sigmoid(a0_ref[d:d + 1, cols(pp)] + _dot(la_in.astype(BF), wa_ref[d, :, cols(pp)].astype(BF)))
            kd_s[2 * pp + d] = k * (1.0 + (a - 1.0) * ka_ref[:, cols(pp)])
            b_s[2 * pp + d] = kk * a

    def chunk_rows(idx):
        return pl.ds(idx * C, C) if isinstance(idx, int) else pl.ds(pl.multiple_of(idx * C, C), C)

    per_iter = WKV_UNROLL // 2

    def phase_a(pp, base):
        keys = [(d, base + t) for t in range(per_iter) for d in range(2)]
        items = []
        for d, idx in keys:
            sl = chunk_rows(idx)
            items.append(dict(d=d, r=r_ref[sl, cols(pp)], v=v_ref[sl, cols(pp)], kk=kk_s[pp, sl, :],
                              lw=lw_s[2 * pp + d, sl, :], b=b_s[2 * pp + d, sl, :], kd=kd_s[2 * pp + d, sl, :]))
        outs = _wkv_chunks_a(items, strict, incl, lane_lo, eye)
        for (d, idx), (q, y0, mt, n, e) in zip(keys, outs):
            q_s[2 * pp + d, idx] = q
            y_s[2 * pp + d, chunk_rows(idx), :] = y0
            m_s[2 * pp + d, idx] = mt
            n_s[2 * pp + d, idx] = n
            e_s[2 * pp + d, idx] = jnp.broadcast_to(e, (8, PAIR))

    for pp in pairs:
        if nc == per_iter:
            phase_a(pp, 0)
        else:
            def body_a(j, carry, pp=pp):
                phase_a(pp, j * per_iter)
                return carry

            lax.fori_loop(0, nc // per_iter, body_a, 0)

    chains = [(pp, d) for pp in pairs for d in range(2)]

    def body_b(i, carry):
        nxt = []
        for (pp, d), s in zip(chains, carry):
            idx = i if d == 0 else nc - 1 - i
            sl = chunk_rows(idx)
            sb = s.astype(BF)
            yq = _dot_nt(q_s[2 * pp + d, idx], sb)
            y_s[2 * pp + d, sl, :] += yq[:C] + yq[C:]
            nxt.append(s * e_s[2 * pp + d, idx][0:1, :] + _dot_nt(sb, m_s[2 * pp + d, idx]) + n_s[2 * pp + d, idx])
        return tuple(nxt)

    if has_s0:
        init = tuple(s0_ref[d, pp] for pp, d in chains)
    else:
        init = tuple(jnp.zeros((PAIR, PAIR), F32) for _ in chains)
    s_fin = lax.fori_loop(0, nc, body_b, init)
    if out_state:
        for (pp, d), s in zip(chains, s_fin):
            so_ref[d, 2 * pp] = s[:HEAD_A, :HEAD_A]
            so_ref[d, 2 * pp + 1] = s[HEAD_A:, HEAD_A:]

    sig_lg = jax.nn.sigmoid(lg_ref[...]).astype(BF)
    inv_n = 1.0 / HEAD_A
    for pp in pairs:
        y = y_s[2 * pp] + y_s[2 * pp + 1]
        mu = _head_sum(y, ones_blk) * inv_n
        yc = y - mu
        var = _head_sum(yc * yc, ones_blk) * inv_n
        y = yc * lax.rsqrt(var + LN_X_EPS) * lnw_ref[:, cols(pp)] + lnb_ref[:, cols(pp)]
        rr = r_ref[:, cols(pp)] * rk_ref[:, cols(pp)]
        bonus = _head_sum(rr * (kd_s[2 * pp] + kd_s[2 * pp + 1]), ones_blk) * v_ref[:, cols(pp)]
        g = _dot(sig_lg, wg_ref[:, cols(pp)].astype(BF))
        y_ref[:, cols(pp)] = ((y + bonus) * g).astype(BF)


def _wkv(proj, row0, n_seq, T, wts, s0, n_pairs):
    (w0, wl, a0, wa, wg, kkw, ka, rk, lnw, lnb) = wts
    sb = row0 // T
    nc = T // WKV_CHUNK
    gw = n_pairs * PAIR
    cb = lambda col: col // gw
    vec = pl.BlockSpec((1, gw), lambda s, p: (0, p))
    in_specs = [pl.BlockSpec((T, gw), lambda s, p: (sb + s, cb(C_R) + p)),
                pl.BlockSpec((T, gw), lambda s, p: (sb + s, cb(C_K) + p)),
                pl.BlockSpec((T, gw), lambda s, p: (sb + s, cb(C_V) + p)),
                pl.BlockSpec((T, 4 * R_LORA), lambda s, p: (sb + s, C_LORA // (4 * R_LORA))),
                pl.BlockSpec((T, R_G_PAD), lambda s, p: (sb + s, C_LG // R_G_PAD)),
                pl.BlockSpec((2, gw), lambda s, p: (0, p)),
                pl.BlockSpec((2, R_LORA, gw), lambda s, p: (0, 0, p)),
                pl.BlockSpec((2, gw), lambda s, p: (0, p)),
                pl.BlockSpec((2, R_LORA, gw), lambda s, p: (0, 0, p)),
                pl.BlockSpec((R_G_PAD, gw), lambda s, p: (0, p)),
                vec, vec, vec, vec, vec]
    args = [proj, proj, proj, proj, proj, w0, wl, a0, wa, wg, kkw, ka, rk, lnw, lnb]
    has_s0 = s0 is not None
    if has_s0:
        in_specs.append(pl.BlockSpec((None, 2, n_pairs, PAIR, PAIR), lambda s, p: (s, 0, p, 0, 0)))
        args.append(s0)
    out_shape = [jax.ShapeDtypeStruct((n_seq * T, D_A), BF)]
    out_specs = [pl.BlockSpec((T, gw), lambda s, p: (s, p))]
    out_state = not has_s0
    if out_state:
        out_shape.append(jax.ShapeDtypeStruct((n_seq, 2, H_A, HEAD_A, HEAD_A), F32))
        out_specs.append(pl.BlockSpec((None, 2, 2 * n_pairs, HEAD_A, HEAD_A), lambda s, p: (s, 0, p, 0, 0)))
    nch = 2 * n_pairs
    return pl.pallas_call(
        functools.partial(_wkv_kernel, T=T, n_pairs=n_pairs, has_s0=has_s0, out_state=out_state),
        out_shape=out_shape,
        grid=(n_seq, N_PAIR // n_pairs),
        in_specs=in_specs,
        out_specs=out_specs,
        scratch_shapes=[pltpu.VMEM((nch, T, PAIR), F32), pltpu.VMEM((nch, T, PAIR), F32),
                        pltpu.VMEM((nch, T, PAIR), F32), pltpu.VMEM((n_pairs, T, PAIR), F32),
                        pltpu.VMEM((nch, T, PAIR), F32),
                        pltpu.VMEM((nch, nc, PAIR, PAIR), BF), pltpu.VMEM((nch, nc, PAIR, PAIR), BF),
                        pltpu.VMEM((nch, nc, PAIR, PAIR), F32), pltpu.VMEM((nch, nc, 8, PAIR), F32)],
        compiler_params=_cparams(("parallel", "arbitrary")),
    )(*args)


W_IN_BLK = 512
W_IN_SIDE = 128
W_IN_B0 = C_QLAT // W_IN_BLK
W_IN_C0 = C_GLA // W_IN_BLK
W_IN_SHIFT = R_G_PAD - R_G


def _w_in_relayout_kernel(main_ref, side_ref, o_ref):
    j = pl.program_id(0)
    row = lax.broadcasted_iota(jnp.int32, o_ref.shape, 0)
    sh = W_IN_SHIFT

    @pl.when(j < W_IN_B0)
    def _():
        valid = jnp.where(j == W_IN_B0 - 1, R_G, W_IN_BLK)
        o_ref[...] = jnp.where(row < valid, main_ref[...], 0.0).astype(BF)

    @pl.when((j >= W_IN_B0) & (j < W_IN_C0))
    def _():
        valid = jnp.where(j == W_IN_C0 - 1, ROPE_DIM, W_IN_BLK)
        x = jnp.concatenate([side_ref[W_IN_SIDE - sh:, :], main_ref[:W_IN_BLK - sh, :]], axis=0)
        o_ref[...] = jnp.where(row < valid, x, 0.0).astype(BF)

    @pl.when(j >= W_IN_C0)
    def _():
        x = jnp.concatenate([main_ref[sh:, :], side_ref[:sh, :]], axis=0)
        o_ref[...] = x.astype(BF)


def _w_in_relayout(w_in_t):
    tc = 2048
    per = W_IN_BLK // W_IN_SIDE

    def main_idx(j, c):
        return jnp.where(j >= W_IN_C0, j - 1, j), c

    def side_idx(j, c):
        return jnp.where(j >= W_IN_C0, per * j, per * jnp.maximum(j, W_IN_B0) - 1), c

    return pl.pallas_call(
        _w_in_relayout_kernel,
        out_shape=jax.ShapeDtypeStruct((D_IN_PAD, D_MODEL), BF),
        grid=(D_IN_PAD // W_IN_BLK, D_MODEL // tc),
        in_specs=[pl.BlockSpec((W_IN_BLK, tc), main_idx), pl.BlockSpec((W_IN_SIDE, tc), side_idx)],
        out_specs=pl.BlockSpec((W_IN_BLK, tc), lambda j, c: (j, c)),
        compiler_params=_cparams(("arbitrary", "arbitrary")),
    )(w_in_t, w_in_t)


def _rope_tables():
    rows = DEC_SEQ // GRID_W
    row = jnp.repeat(jnp.arange(rows), GRID_W).astype(F32)
    col = jnp.tile(jnp.arange(GRID_W), rows).astype(F32)
    n_freq = ROPE_DIM // 4
    inv = ROPE_THETA ** (-jnp.arange(n_freq, dtype=F32) / n_freq)
    ang = jnp.concatenate([row[:, None] * inv, col[:, None] * inv], axis=-1)
    cos, sin = jnp.cos(ang), jnp.sin(ang)
    pad = jnp.zeros((DEC_SEQ, LANES - ROPE_DIM), F32)
    cos_t = jnp.concatenate([cos, cos, pad], axis=-1)
    sin_t = jnp.concatenate([-sin, sin, pad], axis=-1)
    ident = 256
    cos_t = jnp.concatenate([cos_t, jnp.ones((ident, LANES), F32)], axis=0)
    sin_t = jnp.concatenate([sin_t, jnp.zeros((ident, LANES), F32)], axis=0)
    return cos_t, sin_t


def _pair_state_in(state):
    b = state.shape[0]
    s = state.reshape(b, 2, N_PAIR, 2, HEAD_A, HEAD_A)
    z = jnp.zeros_like(s[:, :, :, 0])
    top = jnp.concatenate([s[:, :, :, 0], z], axis=-1)
    bot = jnp.concatenate([z, s[:, :, :, 1]], axis=-1)
    return jnp.concatenate([top, bot], axis=-2)


def kernel(x_prompt, x_sample, c, state_rwkv, cache_mla_ckv, cache_mla_kpe, c_ctx, w_mod, b_mod, norm_mix_w, w_in, rwkv_w0, rwkv_w_lora_b, rwkv_a0, rwkv_a_lora_b, rwkv_g_lora_b, rwkv_k_k, rwkv_k_a, rwkv_r_k, rwkv_ln_w, rwkv_ln_b, mla_q_norm_w, mla_w_uq, mla_kv_norm_w, mla_w_ukv, mla_q_head_norm, mla_k_head_norm, w_o_rwkv, w_o_mla, w_out, norm_ffn_w, w_ffn_in, w_ffn_out):
    l = 0
    xp = x_prompt.reshape(N_PROMPT, D_MODEL)
    xs = x_sample.reshape(N_SAMPLE, D_MODEL)

    w_in_t = _w_in_relayout(jnp.swapaxes(w_in[l], 0, 1))
    w_uq_p = jnp.pad(mla_w_uq[l].reshape(Q_LORA, H_B, QK_DIM), ((0, 0), (0, 0), (0, QK_PAD - QK_DIM)))
    w_uq_p = w_uq_p.reshape(Q_LORA, H_B * QK_PAD).astype(BF)
    q_head_w_p = jnp.pad(mla_q_head_norm[l], (0, QK_PAD - QK_DIM)).reshape(1, QK_PAD)
    wkv3 = mla_w_ukv[l].reshape(KV_LORA, H_B, NOPE_DIM + V_DIM)
    w_ukv_p = jnp.concatenate([wkv3[:, :, :NOPE_DIM].reshape(KV_LORA, H_B * NOPE_DIM),
                               wkv3[:, :, NOPE_DIM:].reshape(KV_LORA, H_B * V_DIM)], axis=1).astype(BF)
    wkn = mla_k_head_norm[l, :NOPE_DIM].reshape(1, NOPE_DIM)
    wkr = jnp.pad(mla_k_head_norm[l, NOPE_DIM:], (0, LANES - ROPE_DIM)).reshape(1, LANES)
    w_oa = w_o_rwkv[l].astype(BF)
    w_ob = w_o_mla[l].astype(BF)
    w_out_b = w_out[l].astype(BF)
    wg_p = jnp.pad(rwkv_g_lora_b[l], ((0, R_G_PAD - R_G), (0, 0)))
    row = lambda a: a.reshape(1, D_A)
    wkv_w = (rwkv_w0[l], rwkv_w_lora_b[l], rwkv_a0[l], rwkv_a_lora_b[l], wg_p, row(rwkv_k_k[l]), row(rwkv_k_a[l]),
             row(rwkv_r_k[l]), row(rwkv_ln_w[l]), row(rwkv_ln_b[l]))
    cos_t, sin_t = _rope_tables()

    cond8 = jnp.concatenate([c_ctx[None, :], c, jnp.zeros((8 - 1 - DEC_BATCH, D_MODEL), F32)], axis=0)
    mod = _modulation(cond8, w_mod[l], b_mod[l]).reshape(8, 6, D_MODEL)

    proj, w_ffn_in_p = _in_proj(_norm_mod((xp, xs), norm_mix_w[l], mod, 0, 1), w_in_t, w_ffn_in[l])

    ya_p, s_new = _wkv(proj, 0, BATCH, SEQ, wkv_w, None, WKV_PAIRS_PROMPT)
    ya_s = _wkv(proj, N_PROMPT, DEC_BATCH, DEC_SEQ, wkv_w, _pair_state_in(state_rwkv[:, l]), WKV_PAIRS_SAMPLE)[0]

    q = _q_proj(proj, mla_q_norm_w[l], w_uq_p, q_head_w_p, cos_t, sin_t)
    k, v, ckv = _kv_proj(proj, C_KVLAT, proj, C_KPE, N_TOK, mla_kv_norm_w[l], w_ukv_p, wkn, wkr, cos_t, sin_t,
                         _rope_block, True)
    ctx_ckv = cache_mla_ckv[:, l].reshape(DEC_BATCH * PAST_LEN, KV_LORA)
    ctx_kpe = jnp.pad(cache_mla_kpe[:, l].reshape(DEC_BATCH * PAST_LEN, ROPE_DIM), ((0, 0), (0, LANES - ROPE_DIM)))
    k_ctx, v_ctx = _kv_proj(ctx_ckv, 0, ctx_kpe, 0, DEC_BATCH * PAST_LEN, mla_kv_norm_w[l], w_ukv_p, wkn, wkr,
                            cos_t, sin_t, lambda i, tm: DEC_SEQ // tm, False)
    yb_p = _attn_prompt(q, k, v)
    yb_s = _attn_sample(q, k, v, k_ctx, v_ctx)

    mixed = _merge((ya_p, ya_s), (yb_p, yb_s), w_oa, w_ob, proj)
    x1 = _resid_proj(mixed, w_out_b, (xp, xs), mod, 2, 512, D_MODEL, 0, N_TOK)
    act, w_ffn_out_p = _ffn_in(_norm_mod((x1,), norm_ffn_w[l], mod, 3, 4), w_ffn_in_p, w_ffn_out[l])
    tk_ffn = D_FF_PAD // 4
    y_p = _resid_proj(act, w_ffn_out_p, (x1,), mod, 5, 1024, tk_ffn, 0, N_PROMPT).reshape(BATCH, SEQ, D_MODEL)
    y_s = _resid_proj(act, w_ffn_out_p, (x1,), mod, 5, 1024, tk_ffn, N_PROMPT, N_SAMPLE)
    y_s = y_s.reshape(DEC_BATCH, DEC_SEQ, D_MODEL)
    new_state = s_new[:, None]
    new_ckv = ckv[:N_PROMPT].reshape(BATCH, 1, SEQ, KV_LORA)
    new_kpe = proj[:N_PROMPT, C_KPE:C_KPE + ROPE_DIM].reshape(BATCH, 1, SEQ, ROPE_DIM)
    return y_p, y_s, new_state, new_ckv, new_kpe
```

```python
import functools

import numpy as np
import jax
import jax.numpy as jnp
from jax import lax
from jax.experimental import pallas as pl
from jax.experimental.pallas import tpu as pltpu

F32 = jnp.float32
BF = jnp.bfloat16

D_MODEL = 4096
BATCH, SEQ = 32, 256
DEC_BATCH, DEC_SEQ, PAST_LEN = 4, 1024, 512
GRID_W = 64
D_A, HEAD_A = 2048, 64
H_A = D_A // HEAD_A
R_LORA, R_G = 128, 480
LN_X_EPS = 64e-5
H_B, Q_LORA, KV_LORA = 16, 1024, 512
NOPE_DIM, ROPE_DIM, V_DIM = 128, 64, 128
QK_DIM = NOPE_DIM + ROPE_DIM
ROPE_THETA = 10000.0
D_FF = 11008
NORM_EPS = 1e-6

N_PROMPT = BATCH * SEQ
N_SAMPLE = DEC_BATCH * DEC_SEQ
N_TOK = N_PROMPT + N_SAMPLE

LANES = 128
R_G_PAD = 512
QK_PAD = 256
D_FF_PAD = 11264
FF_BLK = D_FF_PAD - D_FF
C_R, C_K, C_V = 0, 2048, 4096
C_LORA, C_LG, C_QLAT, C_KVLAT, C_KPE = 6144, 6656, 7168, 8192, 8704
C_GLA, C_GLB, D_IN_PAD = 9216, 13312, 17408
WKV_CHUNK = 64
WKV_UNROLL = 16
WKV_PAIRS_PROMPT = 4
WKV_PAIRS_SAMPLE = 2
DECAY_SCALE = float(np.exp(-0.5))
PAIR = 2 * HEAD_A
N_PAIR = D_A // PAIR

VMEM_LIMIT = 56 * 2**20


def _cparams(sem):
    return pltpu.CompilerParams(dimension_semantics=sem, vmem_limit_bytes=VMEM_LIMIT)


def _dot(a, b):
    return jnp.dot(a, b, preferred_element_type=F32)


def _dot_nt(a, b):
    return lax.dot_general(a, b, (((1,), (1,)), ((), ())), preferred_element_type=F32)


def _dot_tn(a, b):
    return lax.dot_general(a, b, (((0,), (0,)), ((), ())), preferred_element_type=F32)


def _mod_row(i, tm):
    n_prompt_tiles = N_PROMPT // tm
    tiles_per_seq = DEC_SEQ // tm
    return jnp.where(i < n_prompt_tiles, 0, 1 + (i - n_prompt_tiles) // tiles_per_seq)


def _mod_kernel(c_ref, w_ref, b_ref, o_ref):
    c = c_ref[...]
    s = c * jax.nn.sigmoid(c)
    o_ref[...] = _dot(s.astype(BF), w_ref[...].astype(BF)) + b_ref[...]


def _modulation(cond8, w_mod, b_mod):
    tn = 1024
    n = w_mod.shape[1]
    return pl.pallas_call(
        _mod_kernel,
        out_shape=jax.ShapeDtypeStruct((8, n), F32),
        grid=(n // tn,),
        in_specs=[pl.BlockSpec((8, D_MODEL), lambda j: (0, 0)),
                  pl.BlockSpec((D_MODEL, tn), lambda j: (0, j)),
                  pl.BlockSpec((1, tn), lambda j: (0, j))],
        out_specs=pl.BlockSpec((8, tn), lambda j: (0, j)),
        compiler_params=_cparams(("arbitrary",)),
    )(cond8, w_mod, b_mod.reshape(1, n))


NORM_ROWS = 16


NORM_COLS = 512


def _normmod(x_ref, nw_ref, mod_ref, o_ref, gs_ref, shift_idx, scale_idx):
    gs_ref[0] = jnp.broadcast_to(nw_ref[...] * (1.0 + mod_ref[scale_idx:scale_idx + 1, :]), (8, D_MODEL))
    gs_ref[1] = jnp.broadcast_to(mod_ref[shift_idx:shift_idx + 1, :], (8, D_MODEL))
    reps = NORM_ROWS // 8

    def body(r, carry):
        rows = pl.ds(pl.multiple_of(r * NORM_ROWS, NORM_ROWS), NORM_ROWS)
        accs = [jnp.zeros((NORM_ROWS, LANES), F32) for _ in range(4)]
        for c in range(D_MODEL // LANES):
            xc = x_ref[rows, c * LANES:(c + 1) * LANES]
            accs[c % 4] = accs[c % 4] + xc * xc
        acc = (accs[0] + accs[1]) + (accs[2] + accs[3])
        inv = lax.rsqrt(jnp.sum(acc, axis=-1, keepdims=True) * (1.0 / D_MODEL) + NORM_EPS)
        for c in range(D_MODEL // NORM_COLS):
            cs = slice(c * NORM_COLS, (c + 1) * NORM_COLS)
            gain = jnp.concatenate([gs_ref[0, :, cs]] * reps, axis=0)
            shift = jnp.concatenate([gs_ref[1, :, cs]] * reps, axis=0)
            o_ref[rows, cs] = (x_ref[rows, cs] * inv * gain + shift).astype(BF)
        return carry

    lax.fori_loop(0, x_ref.shape[0] // NORM_ROWS, body, 0, unroll=4)


def _split_specs(tm, width, col):
    npt = N_PROMPT // tm
    return [pl.BlockSpec((tm, width), lambda i, *_: (jnp.minimum(i, npt - 1), col(*_))),
            pl.BlockSpec((tm, width), lambda i, *_: (jnp.maximum(i - npt, 0), col(*_)))]


def _norm_mod_kernel(*refs, shift_idx, scale_idx, n_prompt_tiles):
    *x_refs, nw_ref, mod_ref, o_ref, gs_ref = refs

    def run(x_ref):
        _normmod(x_ref, nw_ref, mod_ref, o_ref, gs_ref, shift_idx, scale_idx)

    if len(x_refs) == 1:
        run(x_refs[0])
    else:
        is_prompt = pl.program_id(0) < n_prompt_tiles
        pl.when(is_prompt)(lambda: run(x_refs[0]))
        pl.when(jnp.logical_not(is_prompt))(lambda: run(x_refs[1]))


def _norm_mod(xs, norm_w, mod, shift_idx, scale_idx):
    tm = 256
    if len(xs) == 1:
        x_specs = [pl.BlockSpec((tm, D_MODEL), lambda i: (i, 0))]
    else:
        x_specs = _split_specs(tm, D_MODEL, lambda: 0)
    return pl.pallas_call(
        functools.partial(_norm_mod_kernel, shift_idx=shift_idx, scale_idx=scale_idx, n_prompt_tiles=N_PROMPT // tm),
        out_shape=jax.ShapeDtypeStruct((N_TOK, D_MODEL), BF),
        grid=(N_TOK // tm,),
        in_specs=x_specs + [pl.BlockSpec((1, D_MODEL), lambda i: (0, 0)),
                            pl.BlockSpec((None, 6, D_MODEL), lambda i: (_mod_row(i, tm), 0, 0))],
        out_specs=pl.BlockSpec((tm, D_MODEL), lambda i: (i, 0)),
        scratch_shapes=[pltpu.VMEM((2, 8, D_MODEL), F32)],
        compiler_params=_cparams(("parallel",)),
    )(*xs, norm_w.reshape(1, D_MODEL), mod)


CAST_SPLIT = 2


class _CastJob:
    def __init__(self, w, axis, blk, n_real, n_seg, nj):
        self.w, self.axis, self.blk, self.n_real, self.nj = w, axis, blk, n_real, nj
        self.n_parts = n_seg * (n_real + 1) * CAST_SPLIT
        other = w.shape[1 - axis]
        part = other // CAST_SPLIT
        self.block = (blk, part) if axis == 0 else (part, blk)
        padded = n_seg * (n_real + 1) * blk
        self.out_shape = jax.ShapeDtypeStruct((padded, other) if axis == 0 else (other, padded), BF)

    def _place(self, b, part):
        return (b, part) if self.axis == 0 else (part, b)

    def _block_of(self, i, j):
        t = jnp.minimum(i * self.nj + j, self.n_parts - 1)
        return t // CAST_SPLIT, t % CAST_SPLIT

    def in_spec(self):
        def idx(i, j):
            b, part = self._block_of(i, j)
            seg, jj = b // (self.n_real + 1), b % (self.n_real + 1)
            return self._place(seg * self.n_real + jnp.minimum(jj, self.n_real - 1), part)
        return pl.BlockSpec(self.block, idx)

    def out_spec(self):
        return pl.BlockSpec(self.block, lambda i, j: self._place(*self._block_of(i, j)))

    def step(self, x_ref, o_ref):
        t = pl.program_id(0) * self.nj + pl.program_id(1)
        is_pad = (t // CAST_SPLIT) % (self.n_real + 1) == self.n_real
        active = t < self.n_parts

        @pl.when(active & is_pad)
        def _():
            o_ref[...] = jnp.zeros_like(o_ref)

        @pl.when(active & jnp.logical_not(is_pad))
        def _():
            o_ref[...] = x_ref[...].astype(BF)


def _in_proj_kernel(h_ref, wt_ref, wc_ref, o_ref, oc_ref, *, job):
    o_ref[...] = _dot_nt(h_ref[...], wt_ref[...])
    job.step(wc_ref, oc_ref)


def _in_proj(h, w_in_t, w_ffn_in):
    tm, tn = 1024, 1024
    nj = D_IN_PAD // tn
    job = _CastJob(w_ffn_in, 1, FF_BLK, D_FF // FF_BLK, 2, nj)
    return pl.pallas_call(
        functools.partial(_in_proj_kernel, job=job),
        out_shape=[jax.ShapeDtypeStruct((N_TOK, D_IN_PAD), F32), job.out_shape],
        grid=(N_TOK // tm, nj),
        in_specs=[pl.BlockSpec((tm, D_MODEL), lambda i, j: (i, 0)),
                  pl.BlockSpec((tn, D_MODEL), lambda i, j: (j, 0)),
                  job.in_spec()],
        out_specs=[pl.BlockSpec((tm, tn), lambda i, j: (i, j)), job.out_spec()],
        compiler_params=_cparams(("arbitrary", "arbitrary")),
    )(h, w_in_t, w_ffn_in)


def _ffn_in_kernel(h_ref, wg_ref, wu_ref, wc_ref, o_ref, oc_ref, *, job):
    h = h_ref[...]
    g = _dot(h, wg_ref[...])
    u = _dot(h, wu_ref[...])
    o_ref[...] = (g * jax.nn.sigmoid(g) * u).astype(BF)
    job.step(wc_ref, oc_ref)


def _ffn_in(h, w_ffn_in_p, w_ffn_out):
    tm, tn = 1024, 512
    nj = D_FF_PAD // tn
    job = _CastJob(w_ffn_out, 0, FF_BLK, D_FF // FF_BLK, 1, nj)
    return pl.pallas_call(
        functools.partial(_ffn_in_kernel, job=job),
        out_shape=[jax.ShapeDtypeStruct((N_TOK, D_FF_PAD), BF), job.out_shape],
        grid=(N_TOK // tm, nj),
        in_specs=[pl.BlockSpec((tm, D_MODEL), lambda i, j: (i, 0)),
                  pl.BlockSpec((D_MODEL, tn), lambda i, j: (0, j)),
                  pl.BlockSpec((D_MODEL, tn), lambda i, j: (0, j + nj)),
                  job.in_spec()],
        out_specs=[pl.BlockSpec((tm, tn), lambda i, j: (i, j)), job.out_spec()],
        compiler_params=_cparams(("arbitrary", "arbitrary")),
    )(h, w_ffn_in_p, w_ffn_in_p, w_ffn_out)


def _resid_proj_kernel(a_ref, w_ref, *refs, gate_idx, n_prompt_tiles, nk):
    if nk == 1:
        *x_refs, mod_ref, o_ref = refs
    else:
        *x_refs, mod_ref, o_ref, acc_ref = refs
    part = _dot(a_ref[...], w_ref[...])
    last = True
    if nk > 1:
        k = pl.program_id(2)
        last = k == nk - 1

        @pl.when(k == 0)
        def _():
            acc_ref[...] = part

        @pl.when((k > 0) & jnp.logical_not(last))
        def _():
            acc_ref[...] += part

    def finish(x_ref):
        total = part if nk == 1 else acc_ref[...] + part
        o_ref[...] = x_ref[...] + mod_ref[gate_idx:gate_idx + 1, :] * total

    if len(x_refs) == 1:
        pl.when(jnp.asarray(last))(lambda: finish(x_refs[0]))
    else:
        is_prompt = pl.program_id(0) < n_prompt_tiles
        pl.when(last & is_prompt)(lambda: finish(x_refs[0]))
        pl.when(last & jnp.logical_not(is_prompt))(lambda: finish(x_refs[1]))


def _resid_proj(a, w, xs, mod, gate_idx, tn, tk, row0, n_rows):
    tm = 1024
    kdim = a.shape[1]
    nk = kdim // tk
    r0 = row0 // tm
    if len(xs) == 1:
        x_specs = [pl.BlockSpec((tm, tn), lambda i, j, k: (r0 + i, j))]
    else:
        x_specs = _split_specs(tm, tn, lambda j, k: j)
    return pl.pallas_call(
        functools.partial(_resid_proj_kernel, gate_idx=gate_idx, n_prompt_tiles=N_PROMPT // tm, nk=nk),
        out_shape=jax.ShapeDtypeStruct((n_rows, D_MODEL), F32),
        grid=(n_rows // tm, D_MODEL // tn, nk),
        in_specs=[pl.BlockSpec((tm, tk), lambda i, j, k: (r0 + i, k)),
                  pl.BlockSpec((tk, tn), lambda i, j, k: (k, j))] + x_specs + [
                  pl.BlockSpec((None, 6, tn), lambda i, j, k: (_mod_row(r0 + i, tm), 0, j))],
        out_specs=pl.BlockSpec((tm, tn), lambda i, j, k: (i, j)),
        scratch_shapes=[pltpu.VMEM((tm, tn), F32)] if nk > 1 else [],
        compiler_params=_cparams(("parallel", "parallel", "arbitrary")),
    )(a, w, *xs, mod)


def _merge_kernel(yap_ref, yas_ref, ybp_ref, ybs_ref, wa_ref, wb_ref, gla_ref, glb_ref, o_ref, *, n_prompt_tiles):
    is_prompt = pl.program_id(1) < n_prompt_tiles
    ya = jnp.where(is_prompt, yap_ref[...], yas_ref[...])
    yb = jnp.where(is_prompt, ybp_ref[...], ybs_ref[...])
    pa = _dot(ya, wa_ref[...])
    pb = _dot(yb, wb_ref[...])
    o_ref[...] = (jax.nn.sigmoid(gla_ref[...]) * pa + jax.nn.sigmoid(glb_ref[...]) * pb).astype(BF)


def _merge(y_a, y_b, w_oa, w_ob, proj):
    tm, tn = 512, 1024
    npt = N_PROMPT // tm
    y_specs = [pl.BlockSpec((tm, D_A), lambda j, i: (jnp.minimum(i, npt - 1), 0)),
               pl.BlockSpec((tm, D_A), lambda j, i: (jnp.maximum(i - npt, 0), 0))]
    return pl.pallas_call(
        functools.partial(_merge_kernel, n_prompt_tiles=npt),
        out_shape=jax.ShapeDtypeStruct((N_TOK, D_MODEL), BF),
        grid=(D_MODEL // tn, N_TOK // tm),
        in_specs=y_specs + y_specs + [
                  pl.BlockSpec((D_A, tn), lambda j, i: (0, j)),
                  pl.BlockSpec((D_A, tn), lambda j, i: (0, j)),
                  pl.BlockSpec((tm, tn), lambda j, i: (i, C_GLA // tn + j)),
                  pl.BlockSpec((tm, tn), lambda j, i: (i, C_GLB // tn + j))],
        out_specs=pl.BlockSpec((tm, tn), lambda j, i: (i, j)),
        compiler_params=_cparams(("parallel", "arbitrary")),
    )(*y_a, *y_b, w_oa, w_ob, proj, proj)


def _rope(x, cos_ref, sin_ref):
    swapped = pltpu.roll(x, ROPE_DIM // 2, axis=1) + pltpu.roll(x, LANES - ROPE_DIM // 2, axis=1)
    return x * cos_ref[...] + swapped * sin_ref[...]


def _rope_block(i, tm):
    n_prompt_tiles = N_PROMPT // tm
    tiles_per_seq = DEC_SEQ // tm
    return jnp.where(i < n_prompt_tiles, tiles_per_seq, (i - n_prompt_tiles) % tiles_per_seq)


def _q_kernel(ql_ref, nw_ref, w_ref, hw_ref, cos_ref, sin_ref, o_ref):
    x = ql_ref[...]
    y = x * lax.rsqrt(jnp.mean(x * x, axis=-1, keepdims=True) + NORM_EPS) * nw_ref[...]
    q = _dot(y.astype(BF), w_ref[...])
    hw = hw_ref[...]
    for h in range(H_B):
        qh = q[:, h * QK_PAD:(h + 1) * QK_PAD]
        inv = lax.rsqrt(jnp.sum(qh * qh, axis=-1, keepdims=True) * (1.0 / QK_DIM) + NORM_EPS)
        qn = qh * inv * hw
        o_ref[:, h * QK_PAD:h * QK_PAD + NOPE_DIM] = qn[:, :NOPE_DIM].astype(BF)
        o_ref[:, h * QK_PAD + NOPE_DIM:(h + 1) * QK_PAD] = _rope(qn[:, NOPE_DIM:], cos_ref, sin_ref).astype(BF)


def _q_proj(proj, q_norm_w, w_uq_p, q_head_w_p, cos_t, sin_t):
    tm = 256
    return pl.pallas_call(
        _q_kernel,
        out_shape=jax.ShapeDtypeStruct((N_TOK, H_B * QK_PAD), BF),
        grid=(N_TOK // tm,),
        in_specs=[pl.BlockSpec((tm, Q_LORA), lambda i: (i, C_QLAT // Q_LORA)),
                  pl.BlockSpec((1, Q_LORA), lambda i: (0, 0)),
                  pl.BlockSpec((Q_LORA, H_B * QK_PAD), lambda i: (0, 0)),
                  pl.BlockSpec((1, QK_PAD), lambda i: (0, 0)),
                  pl.BlockSpec((tm, LANES), lambda i: (_rope_block(i, tm), 0)),
                  pl.BlockSpec((tm, LANES), lambda i: (_rope_block(i, tm), 0))],
        out_specs=pl.BlockSpec((tm, H_B * QK_PAD), lambda i: (i, 0)),
        compiler_params=_cparams(("parallel",)),
    )(proj, q_norm_w.reshape(1, Q_LORA), w_uq_p, q_head_w_p, cos_t, sin_t)


def _kv_kernel(lat_ref, kpe_ref, nw_ref, w_ref, wkn_ref, wkr_ref, cos_ref, sin_ref, *out_refs, pre_norm):
    if pre_norm:
        k_ref, v_ref, ckv_ref = out_refs
        x = lat_ref[...]
        ckv = x * lax.rsqrt(jnp.mean(x * x, axis=-1, keepdims=True) + NORM_EPS) * nw_ref[...]
        ckv_ref[...] = ckv
    else:
        k_ref, v_ref = out_refs
        ckv = lat_ref[...]
    kv = _dot(ckv.astype(BF), w_ref[...])
    v_ref[...] = kv[:, H_B * NOPE_DIM:].astype(BF)
    kpe = kpe_ref[...]
    kpe_ss = jnp.sum(kpe * kpe, axis=-1, keepdims=True)
    kpe_rot = _rope(kpe * wkr_ref[...], cos_ref, sin_ref)
    wkn = wkn_ref[...]
    for h in range(H_B):
        kn = kv[:, h * NOPE_DIM:(h + 1) * NOPE_DIM]
        inv = lax.rsqrt((jnp.sum(kn * kn, axis=-1, keepdims=True) + kpe_ss) * (1.0 / QK_DIM) + NORM_EPS)
        k_ref[:, h * QK_PAD:h * QK_PAD + NOPE_DIM] = (kn * inv * wkn).astype(BF)
        k_ref[:, h * QK_PAD + NOPE_DIM:(h + 1) * QK_PAD] = (kpe_rot * inv).astype(BF)


def _kv_proj(lat, lat_col, kpe, kpe_col, n_rows, kv_norm_w, w_ukv_p, wkn, wkr, cos_t, sin_t, rope_block, pre_norm):
    tm = 256
    out_shape = [jax.ShapeDtypeStruct((n_rows, H_B * QK_PAD), BF),
                 jax.ShapeDtypeStruct((n_rows, H_B * V_DIM), BF)]
    out_specs = [pl.BlockSpec((tm, H_B * QK_PAD), lambda i: (i, 0)),
                 pl.BlockSpec((tm, H_B * V_DIM), lambda i: (i, 0))]
    if pre_norm:
        out_shape.append(jax.ShapeDtypeStruct((n_rows, KV_LORA), F32))
        out_specs.append(pl.BlockSpec((tm, KV_LORA), lambda i: (i, 0)))
    return pl.pallas_call(
        functools.partial(_kv_kernel, pre_norm=pre_norm),
        out_shape=out_shape,
        grid=(n_rows // tm,),
        in_specs=[pl.BlockSpec((tm, KV_LORA), lambda i: (i, lat_col // KV_LORA)),
                  pl.BlockSpec((tm, LANES), lambda i: (i, kpe_col // LANES)),
                  pl.BlockSpec((1, KV_LORA), lambda i: (0, 0)),
                  pl.BlockSpec((KV_LORA, H_B * (NOPE_DIM + V_DIM)), lambda i: (0, 0)),
                  pl.BlockSpec((1, NOPE_DIM), lambda i: (0, 0)),
                  pl.BlockSpec((1, LANES), lambda i: (0, 0)),
                  pl.BlockSpec((tm, LANES), lambda i: (rope_block(i, tm), 0)),
                  pl.BlockSpec((tm, LANES), lambda i: (rope_block(i, tm), 0))],
        out_specs=out_specs,
        compiler_params=_cparams(("parallel",)),
    )(lat, kpe, kv_norm_w.reshape(1, KV_LORA), w_ukv_p, wkn, wkr, cos_t, sin_t)


_ATTN_SCALE = QK_DIM ** -0.5


def _attn_prompt_kernel(q_ref, k_ref, v_ref, o_ref, *, heads):
    for h in range(heads):
        q = q_ref[:, h * QK_PAD:(h + 1) * QK_PAD]
        k = k_ref[:, h * QK_PAD:(h + 1) * QK_PAD]
        s = _dot_nt(q, k) * _ATTN_SCALE
        e = jnp.exp(s - jnp.max(s, axis=-1, keepdims=True))
        p = e * (1.0 / jnp.sum(e, axis=-1, keepdims=True))
        o_ref[:, h * V_DIM:(h + 1) * V_DIM] = _dot(p.astype(BF), v_ref[:, h * V_DIM:(h + 1) * V_DIM]).astype(BF)


def _attn_prompt(q, k, v):
    heads = 8
    return pl.pallas_call(
        functools.partial(_attn_prompt_kernel, heads=heads),
        out_shape=jax.ShapeDtypeStruct((N_PROMPT, H_B * V_DIM), BF),
        grid=(BATCH, H_B // heads),
        in_specs=[pl.BlockSpec((SEQ, heads * QK_PAD), lambda b, g: (b, g)),
                  pl.BlockSpec((SEQ, heads * QK_PAD), lambda b, g: (b, g)),
                  pl.BlockSpec((SEQ, heads * V_DIM), lambda b, g: (b, g))],
        out_specs=pl.BlockSpec((SEQ, heads * V_DIM), lambda b, g: (b, g)),
        compiler_params=_cparams(("parallel", "parallel")),
    )(q, k, v)


def _attn_sample_kernel(q_ref, kc_ref, vc_ref, ks_ref, vs_ref, o_ref, *, heads):
    for h in range(heads):
        qk = slice(h * QK_PAD, (h + 1) * QK_PAD)
        vv = slice(h * V_DIM, (h + 1) * V_DIM)
        q = q_ref[:, qk]
        s1 = _dot_nt(q, kc_ref[:, qk]) * _ATTN_SCALE
        s2 = _dot_nt(q, ks_ref[:, qk]) * _ATTN_SCALE
        m = jnp.maximum(jnp.max(s1, axis=-1, keepdims=True), jnp.max(s2, axis=-1, keepdims=True))
        e1 = jnp.exp(s1 - m)
        e2 = jnp.exp(s2 - m)
        inv = 1.0 / (jnp.sum(e1, axis=-1, keepdims=True) + jnp.sum(e2, axis=-1, keepdims=True))
        o = _dot((e1 * inv).astype(BF), vc_ref[:, vv]) + _dot((e2 * inv).astype(BF), vs_ref[:, vv])
        o_ref[:, vv] = o.astype(BF)


def _attn_sample(q, k, v, k_ctx, v_ctx):
    tq = 512
    heads = 4
    qt = DEC_SEQ // tq
    q0 = N_PROMPT // tq
    s0 = N_PROMPT // DEC_SEQ
    return pl.pallas_call(
        functools.partial(_attn_sample_kernel, heads=heads),
        out_shape=jax.ShapeDtypeStruct((N_SAMPLE, H_B * V_DIM), BF),
        grid=(DEC_BATCH, H_B // heads, qt),
        in_specs=[pl.BlockSpec((tq, heads * QK_PAD), lambda b, g, t: (q0 + b * qt + t, g)),
                  pl.BlockSpec((PAST_LEN, heads * QK_PAD), lambda b, g, t: (b, g)),
                  pl.BlockSpec((PAST_LEN, heads * V_DIM), lambda b, g, t: (b, g)),
                  pl.BlockSpec((DEC_SEQ, heads * QK_PAD), lambda b, g, t: (s0 + b, g)),
                  pl.BlockSpec((DEC_SEQ, heads * V_DIM), lambda b, g, t: (s0 + b, g))],
        out_specs=pl.BlockSpec((tq, heads * V_DIM), lambda b, g, t: (b * qt + t, g)),
        compiler_params=_cparams(("parallel", "parallel", "arbitrary")),
    )(q, k_ctx, v_ctx, k, v)


def _head_sum(x, ones_blk):
    hi = x.astype(BF)
    lo = (x - hi.astype(F32)).astype(BF)
    return _dot(hi, ones_blk) + _dot(lo, ones_blk)


def _wkv_chunks_a(items, strict, incl, lane_lo, eye):
    C = WKV_CHUNK
    n2 = 2 * C

    def stack(x):
        return jnp.concatenate([jnp.where(lane_lo, x, 0.0), jnp.where(lane_lo, 0.0, x)], axis=0)

    row_c = lax.broadcasted_iota(jnp.int32, (C, PAIR), 0)

    def cumsum(it):
        x = it["lw"]
        s = 1
        while s < C:
            if it["d"] == 0:
                x = x + jnp.where(row_c >= s, pltpu.roll(x, s, axis=0), 0.0)
            else:
                x = x + jnp.where(row_c < C - s, pltpu.roll(x, C - s, axis=0), 0.0)
            s *= 2
        return x

    cs = [cumsum(it) for it in items]

    def operands(it, c):
        cl = c[C - 1:C, :] if it["d"] == 0 else c[0:1, :]
        e_inc = jnp.exp(c)
        e_neg = jnp.exp(-c)
        e_exc = jnp.exp(c - it["lw"])
        e_end = jnp.exp(cl - c)
        b, kd = it["b"], it["kd"]
        return dict(d=it["d"], e=jnp.exp(cl),
                    a_t=stack(-it["kk"] * e_exc).astype(BF), r_t=stack(it["r"] * e_inc),
                    b_t=stack(b * e_neg).astype(BF), k_t=stack(kd * e_neg).astype(BF),
                    bk_h=jnp.concatenate([stack(b * e_end), stack(kd * e_end)], axis=0).astype(BF),
                    v_s=stack(it["v"]).astype(BF))

    ops = [operands(it, c) for it, c in zip(items, cs)]
    scs = [_dot_nt(jnp.concatenate([o["a_t"], o["r_t"].astype(BF)], axis=0),
                   jnp.concatenate([o["b_t"], o["k_t"]], axis=0)) for o in ops]
    for o, sc in zip(ops, scs):
        d = o["d"]
        o["lp"] = jnp.where(strict[d], sc[:n2, :n2], 0.0).astype(BF)
        o["l_ak"] = jnp.where(strict[d], sc[:n2, n2:], 0.0).astype(BF)
        o["a_r"] = jnp.concatenate([jnp.where(incl[d], sc[n2:, :n2], 0.0),
                                    jnp.where(incl[d], sc[n2:, n2:], 0.0)], axis=1).astype(BF)

    ts = [eye + o["lp"].astype(F32) for o in ops]
    lps = [_dot(o["lp"], o["lp"]).astype(BF) for o in ops]
    span = 2
    while 2 * span < C:
        rs = [_dot(lp, jnp.concatenate([t.astype(BF), lp], axis=1)) for t, lp in zip(ts, lps)]
        ts = [t + r[:, :n2] for t, r in zip(ts, rs)]
        lps = [r[:, n2:].astype(BF) for r in rs]
        span *= 2
    ts = [t + _dot(lp, t.astype(BF)) for t, lp in zip(ts, lps)]

    lakvs = [_dot(o["l_ak"], o["v_s"]) for o in ops]
    wus = [_dot(t.astype(BF), jnp.concatenate([o["a_t"], lakv.astype(BF)], axis=1))
           for t, o, lakv in zip(ts, ops, lakvs)]
    zero = jnp.zeros((n2, n2), BF)
    outs = []
    for o, wu in zip(ops, wus):
        rhs = jnp.concatenate([wu.astype(BF), jnp.concatenate([zero, o["v_s"]], axis=1)], axis=0)
        qy = _dot(o["a_r"], rhs)
        mn = _dot_tn(o["bk_h"], rhs)
        q = (o["r_t"] + qy[:, :n2]).astype(BF)
        y0 = qy[:C, n2:] + qy[C:, n2:]
        outs.append((q, y0, mn[:, :n2].astype(BF), mn[:, n2:].T, o["e"]))
    return outs


def _wkv_kernel(*refs, T, n_pairs, has_s0, out_state):
    (r_ref, k_ref, v_ref, lora_ref, lg_ref, w0_ref, wl_ref, a0_ref, wa_ref, wg_ref,
     kkw_ref, ka_ref, rk_ref, lnw_ref, lnb_ref) = refs[:15]
    pos = 15
    if has_s0:
        s0_ref = refs[pos]
        pos += 1
    y_ref = refs[pos]
    pos += 1
    if out_state:
        so_ref = refs[pos]
        pos += 1
    lw_s, b_s, kd_s, kk_s, y_s, q_s, m_s, n_s, e_s = refs[pos:]

    C = WKV_CHUNK
    nc = T // C
    lane = lax.broadcasted_iota(jnp.int32, (PAIR, PAIR), 1)
    row = lax.broadcasted_iota(jnp.int32, (PAIR, PAIR), 0)
    ones_blk = jnp.where((lane // HEAD_A) == (row // HEAD_A), 1.0, 0.0).astype(BF)
    eye = jnp.where(lane == row, 1.0, 0.0)
    lane_lo = lax.broadcasted_iota(jnp.int32, (C, PAIR), 1) < HEAD_A
    strict = (lane < row, lane > row)
    incl = (lane <= row, lane >= row)
    pairs = range(n_pairs)

    def cols(pp):
        return slice(pp * PAIR, (pp + 1) * PAIR)

    for pp in pairs:
        k = k_ref[:, cols(pp)]
        kk = k * kkw_ref[:, cols(pp)]
        kk = kk * lax.rsqrt(_head_sum(kk * kk, ones_blk) + 1e-12)
        kk_s[pp] = kk
        for d in range(2):
            lw_in = lora_ref[:, d * R_LORA:(d + 1) * R_LORA]
            la_in = lora_ref[:, (2 + d) * R_LORA:(3 + d) * R_LORA]
            x = w0_ref[d:d + 1, cols(pp)] + _dot(jnp.tanh(lw_in).astype(BF), wl_ref[d, :, cols(pp)].astype(BF))
            lw_s[2 * pp + d] = -DECAY_SCALE * jax.nn.sigmoid(x)
            a = jax.nn.sigmoid(a0_ref[d:d + 1, cols(pp)] + _dot(la_in.astype(BF), wa_ref[d, :, cols(pp)].astype(BF)))
            kd_s[2 * pp + d] = k * (1.0 + (a - 1.0) * ka_ref[:, cols(pp)])
            b_s[2 * pp + d] = kk * a

    def chunk_rows(idx):
        return pl.ds(idx * C, C) if isinstance(idx, int) else pl.ds(pl.multiple_of(idx * C, C), C)

    def phase_a(keys):
        items = []
        for pp, d, idx in keys:
            sl = chunk_rows(idx)
            items.append(dict(d=d, r=r_ref[sl, cols(pp)], v=v_ref[sl, cols(pp)], kk=kk_s[pp, sl, :],
                              lw=lw_s[2 * pp + d, sl, :], b=b_s[2 * pp + d, sl, :], kd=kd_s[2 * pp + d, sl, :]))
        outs = _wkv_chunks_a(items, strict, incl, lane_lo, eye)
        for (pp, d, idx), (q, y0, mt, n, e) in zip(keys, outs):
            q_s[2 * pp + d, idx] = q
            y_s[2 * pp + d, chunk_rows(idx), :] = y0
            m_s[2 * pp + d, idx] = mt
            n_s[2 * pp + d, idx] = n
            e_s[2 * pp + d, idx] = jnp.broadcast_to(e, (8, PAIR))

    if 2 * nc <= WKV_UNROLL:
        group = min(WKV_UNROLL // (2 * nc), n_pairs)
        for g in range(0, n_pairs, group):
            phase_a([(pp, d, t) for pp in range(g, g + group) for t in range(nc) for d in range(2)])
    else:
        per_iter = WKV_UNROLL // 2
        for pp in pairs:
            def body_a(j, carry, pp=pp):
                phase_a([(pp, d, j * per_iter + t) for t in range(per_iter) for d in range(2)])
                return carry

            lax.fori_loop(0, nc // per_iter, body_a, 0)

    chains = [(pp, d) for pp in pairs for d in range(2)]

    def body_b(i, carry):
        nxt = []
        for (pp, d), s in zip(chains, carry):
            idx = i if d == 0 else nc - 1 - i
            sl = chunk_rows(idx)
            sb = s.astype(BF)
            yq = _dot_nt(q_s[2 * pp + d, idx], sb)
            y_s[2 * pp + d, sl, :] += yq[:C] + yq[C:]
            nxt.append(s * e_s[2 * pp + d, idx][0:1, :] + _dot_nt(sb, m_s[2 * pp + d, idx]) + n_s[2 * pp + d, idx])
        return tuple(nxt)

    if has_s0:
        init = tuple(s0_ref[d, pp] for pp, d in chains)
    else:
        init = tuple(jnp.zeros((PAIR, PAIR), F32) for _ in chains)
    s_fin = lax.fori_loop(0, nc, body_b, init)
    if out_state:
        for (pp, d), s in zip(chains, s_fin):
            so_ref[d, 2 * pp] = s[:HEAD_A, :HEAD_A]
            so_ref[d, 2 * pp + 1] = s[HEAD_A:, HEAD_A:]

    sig_lg = jax.nn.sigmoid(lg_ref[...]).astype(BF)
    inv_n = 1.0 / HEAD_A
    for pp in pairs:
        y = y_s[2 * pp] + y_s[2 * pp + 1]
        mu = _head_sum(y, ones_blk) * inv_n
        yc = y - mu
        var = _head_sum(yc * yc, ones_blk) * inv_n
        y = yc * lax.rsqrt(var + LN_X_EPS) * lnw_ref[:, cols(pp)] + lnb_ref[:, cols(pp)]
        rr = r_ref[:, cols(pp)] * rk_ref[:, cols(pp)]
        bonus = _head_sum(rr * (kd_s[2 * pp] + kd_s[2 * pp + 1]), ones_blk) * v_ref[:, cols(pp)]
        g = _dot(sig_lg, wg_ref[:, cols(pp)].astype(BF))
        y_ref[:, cols(pp)] = ((y + bonus) * g).astype(BF)


def _wkv(proj, row0, n_seq, T, wts, s0, n_pairs):
    (w0, wl, a0, wa, wg, kkw, ka, rk, lnw, lnb) = wts
    sb = row0 // T
    nc = T // WKV_CHUNK
    gw = n_pairs * PAIR
    cb = lambda col: col // gw
    vec = pl.BlockSpec((1, gw), lambda s, p: (0, p))
    in_specs = [pl.BlockSpec((T, gw), lambda s, p: (sb + s, cb(C_R) + p)),
                pl.BlockSpec((T, gw), lambda s, p: (sb + s, cb(C_K) + p)),
                pl.BlockSpec((T, gw), lambda s, p: (sb + s, cb(C_V) + p)),
                pl.BlockSpec((T, 4 * R_LORA), lambda s, p: (sb + s, C_LORA // (4 * R_LORA))),
                pl.BlockSpec((T, R_G_PAD), lambda s, p: (sb + s, C_LG // R_G_PAD)),
                pl.BlockSpec((2, gw), lambda s, p: (0, p)),
                pl.BlockSpec((2, R_LORA, gw), lambda s, p: (0, 0, p)),
                pl.BlockSpec((2, gw), lambda s, p: (0, p)),
                pl.BlockSpec((2, R_LORA, gw), lambda s, p: (0, 0, p)),
                pl.BlockSpec((R_G_PAD, gw), lambda s, p: (0, p)),
                vec, vec, vec, vec, vec]
    args = [proj, proj, proj, proj, proj, w0, wl, a0, wa, wg, kkw, ka, rk, lnw, lnb]
    has_s0 = s0 is not None
    if has_s0:
        in_specs.append(pl.BlockSpec((None, 2, n_pairs, PAIR, PAIR), lambda s, p: (s, 0, p, 0, 0)))
        args.append(s0)
    out_shape = [jax.ShapeDtypeStruct((n_seq * T, D_A), BF)]
    out_specs = [pl.BlockSpec((T, gw), lambda s, p: (s, p))]
    out_state = not has_s0
    if out_state:
        out_shape.append(jax.ShapeDtypeStruct((n_seq, 2, H_A, HEAD_A, HEAD_A), F32))
        out_specs.append(pl.BlockSpec((None, 2, 2 * n_pairs, HEAD_A, HEAD_A), lambda s, p: (s, 0, p, 0, 0)))
    nch = 2 * n_pairs
    return pl.pallas_call(
        functools.partial(_wkv_kernel, T=T, n_pairs=n_pairs, has_s0=has_s0, out_state=out_state),
        out_shape=out_shape,
        grid=(n_seq, N_PAIR // n_pairs),
        in_specs=in_specs,
        out_specs=out_specs,
        scratch_shapes=[pltpu.VMEM((nch, T, PAIR), F32), pltpu.VMEM((nch, T, PAIR), F32),
                        pltpu.VMEM((nch, T, PAIR), F32), pltpu.VMEM((n_pairs, T, PAIR), F32),
                        pltpu.VMEM((nch, T, PAIR), F32),
                        pltpu.VMEM((nch, nc, PAIR, PAIR), BF), pltpu.VMEM((nch, nc, PAIR, PAIR), BF),
                        pltpu.VMEM((nch, nc, PAIR, PAIR), F32), pltpu.VMEM((nch, nc, 8, PAIR), F32)],
        compiler_params=_cparams(("parallel", "arbitrary")),
    )(*args)


W_IN_BLK = 512
W_IN_SIDE = 128
W_IN_B0 = C_QLAT // W_IN_BLK
W_IN_C0 = C_GLA // W_IN_BLK
W_IN_SHIFT = R_G_PAD - R_G


def _w_in_relayout_kernel(main_ref, side_ref, o_ref):
    j = pl.program_id(0)
    row = lax.broadcasted_iota(jnp.int32, o_ref.shape, 0)
    sh = W_IN_SHIFT

    @pl.when(j < W_IN_B0)
    def _():
        valid = jnp.where(j == W_IN_B0 - 1, R_G, W_IN_BLK)
        o_ref[...] = jnp.where(row < valid, main_ref[...], 0.0).astype(BF)

    @pl.when((j >= W_IN_B0) & (j < W_IN_C0))
    def _():
        valid = jnp.where(j == W_IN_C0 - 1, ROPE_DIM, W_IN_BLK)
        x = jnp.concatenate([side_ref[W_IN_SIDE - sh:, :], main_ref[:W_IN_BLK - sh, :]], axis=0)
        o_ref[...] = jnp.where(row < valid, x, 0.0).astype(BF)

    @pl.when(j >= W_IN_C0)
    def _():
        x = jnp.concatenate([main_ref[sh:, :], side_ref[:sh, :]], axis=0)
        o_ref[...] = x.astype(BF)


def _w_in_relayout(w_in_t):
    tc = 2048
    per = W_IN_BLK // W_IN_SIDE

    def main_idx(j, c):
        return jnp.where(j >= W_IN_C0, j - 1, j), c

    def side_idx(j, c):
        return jnp.where(j >= W_IN_C0, per * j, per * jnp.maximum(j, W_IN_B0) - 1), c

    return pl.pallas_call(
        _w_in_relayout_kernel,
        out_shape=jax.ShapeDtypeStruct((D_IN_PAD, D_MODEL), BF),
        grid=(D_IN_PAD // W_IN_BLK, D_MODEL // tc),
        in_specs=[pl.BlockSpec((W_IN_BLK, tc), main_idx), pl.BlockSpec((W_IN_SIDE, tc), side_idx)],
        out_specs=pl.BlockSpec((W_IN_BLK, tc), lambda j, c: (j, c)),
        compiler_params=_cparams(("arbitrary", "arbitrary")),
    )(w_in_t, w_in_t)


def _rope_tables():
    rows = DEC_SEQ // GRID_W
    row = jnp.repeat(jnp.arange(rows), GRID_W).astype(F32)
    col = jnp.tile(jnp.arange(GRID_W), rows).astype(F32)
    n_freq = ROPE_DIM // 4
    inv = ROPE_THETA ** (-jnp.arange(n_freq, dtype=F32) / n_freq)
    ang = jnp.concatenate([row[:, None] * inv, col[:, None] * inv], axis=-1)
    cos, sin = jnp.cos(ang), jnp.sin(ang)
    pad = jnp.zeros((DEC_SEQ, LANES - ROPE_DIM), F32)
    cos_t = jnp.concatenate([cos, cos, pad], axis=-1)
    sin_t = jnp.concatenate([-sin, sin, pad], axis=-1)
    ident = 256
    cos_t = jnp.concatenate([cos_t, jnp.ones((ident, LANES), F32)], axis=0)
    sin_t = jnp.concatenate([sin_t, jnp.zeros((ident, LANES), F32)], axis=0)
    return cos_t, sin_t


def _pair_state_in(state):
    b = state.shape[0]
    s = state.reshape(b, 2, N_PAIR, 2, HEAD_A, HEAD_A)
    z = jnp.zeros_like(s[:, :, :, 0])
    top = jnp.concatenate([s[:, :, :, 0], z], axis=-1)
    bot = jnp.concatenate([z, s[:, :, :, 1]], axis=-1)
    return jnp.concatenate([top, bot], axis=-2)


def kernel(x_prompt, x_sample, c, state_rwkv, cache_mla_ckv, cache_mla_kpe, c_ctx, w_mod, b_mod, norm_mix_w, w_in, rwkv_w0, rwkv_w_lora_b, rwkv_a0, rwkv_a_lora_b, rwkv_g_lora_b, rwkv_k_k, rwkv_k_a, rwkv_r_k, rwkv_ln_w, rwkv_ln_b, mla_q_norm_w, mla_w_uq, mla_kv_norm_w, mla_w_ukv, mla_q_head_norm, mla_k_head_norm, w_o_rwkv, w_o_mla, w_out, norm_ffn_w, w_ffn_in, w_ffn_out):
    l = 0
    xp = x_prompt.reshape(N_PROMPT, D_MODEL)
    xs = x_sample.reshape(N_SAMPLE, D_MODEL)

    w_in_t = _w_in_relayout(jnp.swapaxes(w_in[l], 0, 1))
    w_uq_p = jnp.pad(mla_w_uq[l].reshape(Q_LORA, H_B, QK_DIM), ((0, 0), (0, 0), (0, QK_PAD - QK_DIM)))
    w_uq_p = w_uq_p.reshape(Q_LORA, H_B * QK_PAD).astype(BF)
    q_head_w_p = jnp.pad(mla_q_head_norm[l], (0, QK_PAD - QK_DIM)).reshape(1, QK_PAD)
    wkv3 = mla_w_ukv[l].reshape(KV_LORA, H_B, NOPE_DIM + V_DIM)
    w_ukv_p = jnp.concatenate([wkv3[:, :, :NOPE_DIM].reshape(KV_LORA, H_B * NOPE_DIM),
                               wkv3[:, :, NOPE_DIM:].reshape(KV_LORA, H_B * V_DIM)], axis=1).astype(BF)
    wkn = mla_k_head_norm[l, :NOPE_DIM].reshape(1, NOPE_DIM)
    wkr = jnp.pad(mla_k_head_norm[l, NOPE_DIM:], (0, LANES - ROPE_DIM)).reshape(1, LANES)
    w_oa = w_o_rwkv[l].astype(BF)
    w_ob = w_o_mla[l].astype(BF)
    w_out_b = w_out[l].astype(BF)
    wg_p = jnp.pad(rwkv_g_lora_b[l], ((0, R_G_PAD - R_G), (0, 0)))
    row = lambda a: a.reshape(1, D_A)
    wkv_w = (rwkv_w0[l], rwkv_w_lora_b[l], rwkv_a0[l], rwkv_a_lora_b[l], wg_p, row(rwkv_k_k[l]), row(rwkv_k_a[l]),
             row(rwkv_r_k[l]), row(rwkv_ln_w[l]), row(rwkv_ln_b[l]))
    cos_t, sin_t = _rope_tables()

    cond8 = jnp.concatenate([c_ctx[None, :], c, jnp.zeros((8 - 1 - DEC_BATCH, D_MODEL), F32)], axis=0)
    mod = _modulation(cond8, w_mod[l], b_mod[l]).reshape(8, 6, D_MODEL)

    proj, w_ffn_in_p = _in_proj(_norm_mod((xp, xs), norm_mix_w[l], mod, 0, 1), w_in_t, w_ffn_in[l])

    ya_p, s_new = _wkv(proj, 0, BATCH, SEQ, wkv_w, None, WKV_PAIRS_PROMPT)
    ya_s = _wkv(proj, N_PROMPT, DEC_BATCH, DEC_SEQ, wkv_w, _pair_state_in(state_rwkv[:, l]), WKV_PAIRS_SAMPLE)[0]

    q = _q_proj(proj, mla_q_norm_w[l], w_uq_p, q_head_w_p, cos_t, sin_t)
    k, v, ckv = _kv_proj(proj, C_KVLAT, proj, C_KPE, N_TOK, mla_kv_norm_w[l], w_ukv_p, wkn, wkr, cos_t, sin_t,
                         _rope_block, True)
    ctx_ckv = cache_mla_ckv[:, l].reshape(DEC_BATCH * PAST_LEN, KV_LORA)
    ctx_kpe = jnp.pad(cache_mla_kpe[:, l].reshape(DEC_BATCH * PAST_LEN, ROPE_DIM), ((0, 0), (0, LANES - ROPE_DIM)))
    k_ctx, v_ctx = _kv_proj(ctx_ckv, 0, ctx_kpe, 0, DEC_BATCH * PAST_LEN, mla_kv_norm_w[l], w_ukv_p, wkn, wkr,
                            cos_t, sin_t, lambda i, tm: DEC_SEQ // tm, False)
    yb_p = _attn_prompt(q, k, v)
    yb_s = _attn_sample(q, k, v, k_ctx, v_ctx)

    mixed = _merge((ya_p, ya_s), (yb_p, yb_s), w_oa, w_ob, proj)
    x1 = _resid_proj(mixed, w_out_b, (xp, xs), mod, 2, 512, D_MODEL, 0, N_TOK)
    act, w_ffn_out_p = _ffn_in(_norm_mod((x1,), norm_ffn_w[l], mod, 3, 4), w_ffn_in_p, w_ffn_out[l])
    tk_ffn = D_FF_PAD // 4
    y_p = _resid_proj(act, w_ffn_out_p, (x1,), mod, 5, 1024, tk_ffn, 0, N_PROMPT).reshape(BATCH, SEQ, D_MODEL)
    y_s = _resid_proj(act, w_ffn_out_p, (x1,), mod, 5, 1024, tk_ffn, N_PROMPT, N_SAMPLE)
    y_s = y_s.reshape(DEC_BATCH, DEC_SEQ, D_MODEL)
    new_state = s_new[:, None]
    new_ckv = ckv[:N_PROMPT].reshape(BATCH, 1, SEQ, KV_LORA)
    new_kpe = proj[:N_PROMPT, C_KPE:C_KPE + ROPE_DIM].reshape(BATCH, 1, SEQ, ROPE_DIM)
    return y_p, y_s, new_state, new_ckv, new_kpe
```

```python
import functools

import numpy as np
import jax
import jax.numpy as jnp
from jax import lax
from jax.experimental import pallas as pl
from jax.experimental.pallas import tpu as pltpu

F32 = jnp.float32
BF = jnp.bfloat16

D_MODEL = 4096
BATCH, SEQ = 32, 256
DEC_BATCH, DEC_SEQ, PAST_LEN = 4, 1024, 512
GRID_W = 64
D_A, HEAD_A = 2048, 64
H_A = D_A // HEAD_A
R_LORA, R_G = 128, 480
LN_X_EPS = 64e-5
H_B, Q_LORA, KV_LORA = 16, 1024, 512
NOPE_DIM, ROPE_DIM, V_DIM = 128, 64, 128
QK_DIM = NOPE_DIM + ROPE_DIM
ROPE_THETA = 10000.0
D_FF = 11008
NORM_EPS = 1e-6

N_PROMPT = BATCH * SEQ
N_SAMPLE = DEC_BATCH * DEC_SEQ
N_TOK = N_PROMPT + N_SAMPLE

LANES = 128
R_G_PAD = 512
QK_PAD = 256
D_FF_PAD = 11264
FF_BLK = D_FF_PAD - D_FF
C_R, C_K, C_V = 0, 2048, 4096
C_LORA, C_LG, C_QLAT, C_KVLAT, C_KPE = 6144, 6656, 7168, 8192, 8704
C_GLA, C_GLB, D_IN_PAD = 9216, 13312, 17408
QKV_TM = 256
WKV_CHUNK = 64
WKV_UNROLL = 16
WKV_PAIRS_PROMPT = 4
WKV_PAIRS_SAMPLE = 2
DECAY_SCALE = float(np.exp(-0.5))
PAIR = 2 * HEAD_A
N_PAIR = D_A // PAIR

VMEM_LIMIT = 56 * 2**20


def _cparams(sem):
    return pltpu.CompilerParams(dimension_semantics=sem, vmem_limit_bytes=VMEM_LIMIT)


def _dot(a, b):
    return jnp.dot(a, b, preferred_element_type=F32)


def _dot_nt(a, b):
    return lax.dot_general(a, b, (((1,), (1,)), ((), ())), preferred_element_type=F32)


def _dot_tn(a, b):
    return lax.dot_general(a, b, (((0,), (0,)), ((), ())), preferred_element_type=F32)


def _mod_row(i, tm):
    n_prompt_tiles = N_PROMPT // tm
    tiles_per_seq = DEC_SEQ // tm
    return jnp.where(i < n_prompt_tiles, 0, 1 + (i - n_prompt_tiles) // tiles_per_seq)


def _mod_kernel(c_ref, w_ref, b_ref, o_ref):
    c = c_ref[...]
    s = c * jax.nn.sigmoid(c)
    o_ref[...] = _dot(s.astype(BF), w_ref[...].astype(BF)) + b_ref[...]


def _modulation(cond8, w_mod, b_mod):
    tn = 1024
    n = w_mod.shape[1]
    return pl.pallas_call(
        _mod_kernel,
        out_shape=jax.ShapeDtypeStruct((8, n), F32),
        grid=(n // tn,),
        in_specs=[pl.BlockSpec((8, D_MODEL), lambda j: (0, 0)),
                  pl.BlockSpec((D_MODEL, tn), lambda j: (0, j)),
                  pl.BlockSpec((1, tn), lambda j: (0, j))],
        out_specs=pl.BlockSpec((8, tn), lambda j: (0, j)),
        compiler_params=_cparams(("arbitrary",)),
    )(cond8, w_mod, b_mod.reshape(1, n))


NORM_ROWS = 16


NORM_COLS = 512


def _normmod(x_ref, nw_ref, mod_ref, o_ref, gs_ref, shift_idx, scale_idx):
    gs_ref[0] = jnp.broadcast_to(nw_ref[...] * (1.0 + mod_ref[scale_idx:scale_idx + 1, :]), (8, D_MODEL))
    gs_ref[1] = jnp.broadcast_to(mod_ref[shift_idx:shift_idx + 1, :], (8, D_MODEL))
    reps = NORM_ROWS // 8

    def body(r, carry):
        rows = pl.ds(pl.multiple_of(r * NORM_ROWS, NORM_ROWS), NORM_ROWS)
        accs = [jnp.zeros((NORM_ROWS, LANES), F32) for _ in range(4)]
        for c in range(D_MODEL // LANES):
            xc = x_ref[rows, c * LANES:(c + 1) * LANES]
            accs[c % 4] = accs[c % 4] + xc * xc
        acc = (accs[0] + accs[1]) + (accs[2] + accs[3])
        inv = lax.rsqrt(jnp.sum(acc, axis=-1, keepdims=True) * (1.0 / D_MODEL) + NORM_EPS)
        for c in range(D_MODEL // NORM_COLS):
            cs = slice(c * NORM_COLS, (c + 1) * NORM_COLS)
            gain = jnp.concatenate([gs_ref[0, :, cs]] * reps, axis=0)
            shift = jnp.concatenate([gs_ref[1, :, cs]] * reps, axis=0)
            o_ref[rows, cs] = (x_ref[rows, cs] * inv * gain + shift).astype(BF)
        return carry

    lax.fori_loop(0, x_ref.shape[0] // NORM_ROWS, body, 0, unroll=4)


def _split_specs(tm, width, col):
    npt = N_PROMPT // tm
    return [pl.BlockSpec((tm, width), lambda i, *_: (jnp.minimum(i, npt - 1), col(*_))),
            pl.BlockSpec((tm, width), lambda i, *_: (jnp.maximum(i - npt, 0), col(*_)))]


def _norm_mod_kernel(*refs, shift_idx, scale_idx, n_prompt_tiles):
    *x_refs, nw_ref, mod_ref, o_ref, gs_ref = refs

    def run(x_ref):
        _normmod(x_ref, nw_ref, mod_ref, o_ref, gs_ref, shift_idx, scale_idx)

    if len(x_refs) == 1:
        run(x_refs[0])
    else:
        is_prompt = pl.program_id(0) < n_prompt_tiles
        pl.when(is_prompt)(lambda: run(x_refs[0]))
        pl.when(jnp.logical_not(is_prompt))(lambda: run(x_refs[1]))


def _norm_mod(xs, norm_w, mod, shift_idx, scale_idx):
    tm = 256
    if len(xs) == 1:
        x_specs = [pl.BlockSpec((tm, D_MODEL), lambda i: (i, 0))]
    else:
        x_specs = _split_specs(tm, D_MODEL, lambda: 0)
    return pl.pallas_call(
        functools.partial(_norm_mod_kernel, shift_idx=shift_idx, scale_idx=scale_idx, n_prompt_tiles=N_PROMPT // tm),
        out_shape=jax.ShapeDtypeStruct((N_TOK, D_MODEL), BF),
        grid=(N_TOK // tm,),
        in_specs=x_specs + [pl.BlockSpec((1, D_MODEL), lambda i: (0, 0)),
                            pl.BlockSpec((None, 6, D_MODEL), lambda i: (_mod_row(i, tm), 0, 0))],
        out_specs=pl.BlockSpec((tm, D_MODEL), lambda i: (i, 0)),
        scratch_shapes=[pltpu.VMEM((2, 8, D_MODEL), F32)],
        compiler_params=_cparams(("parallel",)),
    )(*xs, norm_w.reshape(1, D_MODEL), mod)


CAST_SPLIT = 2


class _CastJob:
    def __init__(self, w, axis, blk, n_real, n_seg, nj):
        self.w, self.axis, self.blk, self.n_real, self.nj = w, axis, blk, n_real, nj
        self.n_parts = n_seg * (n_real + 1) * CAST_SPLIT
        other = w.shape[1 - axis]
        part = other // CAST_SPLIT
        self.block = (blk, part) if axis == 0 else (part, blk)
        padded = n_seg * (n_real + 1) * blk
        self.out_shape = jax.ShapeDtypeStruct((padded, other) if axis == 0 else (other, padded), BF)

    def _place(self, b, part):
        return (b, part) if self.axis == 0 else (part, b)

    def _block_of(self, i, j):
        t = jnp.minimum(i * self.nj + j, self.n_parts - 1)
        return t // CAST_SPLIT, t % CAST_SPLIT

    def in_spec(self):
        def idx(i, j):
            b, part = self._block_of(i, j)
            seg, jj = b // (self.n_real + 1), b % (self.n_real + 1)
            return self._place(seg * self.n_real + jnp.minimum(jj, self.n_real - 1), part)
        return pl.BlockSpec(self.block, idx)

    def out_spec(self):
        return pl.BlockSpec(self.block, lambda i, j: self._place(*self._block_of(i, j)))

    def step(self, x_ref, o_ref):
        t = pl.program_id(0) * self.nj + pl.program_id(1)
        is_pad = (t // CAST_SPLIT) % (self.n_real + 1) == self.n_real
        active = t < self.n_parts

        @pl.when(active & is_pad)
        def _():
            o_ref[...] = jnp.zeros_like(o_ref)

        @pl.when(active & jnp.logical_not(is_pad))
        def _():
            o_ref[...] = x_ref[...].astype(BF)


def _in_proj_kernel(h_ref, wt_ref, wc_ref, o_ref, oc_ref, *, job):
    o_ref[...] = _dot_nt(h_ref[...], wt_ref[...])
    job.step(wc_ref, oc_ref)


def _in_proj(h, w_in_t, w_ffn_in):
    tm, tn = 1024, 1024
    nj = D_IN_PAD // tn
    job = _CastJob(w_ffn_in, 1, FF_BLK, D_FF // FF_BLK, 2, nj)
    return pl.pallas_call(
        functools.partial(_in_proj_kernel, job=job),
        out_shape=[jax.ShapeDtypeStruct((N_TOK, D_IN_PAD), F32), job.out_shape],
        grid=(N_TOK // tm, nj),
        in_specs=[pl.BlockSpec((tm, D_MODEL), lambda i, j: (i, 0)),
                  pl.BlockSpec((tn, D_MODEL), lambda i, j: (j, 0)),
                  job.in_spec()],
        out_specs=[pl.BlockSpec((tm, tn), lambda i, j: (i, j)), job.out_spec()],
        compiler_params=_cparams(("arbitrary", "arbitrary")),
    )(h, w_in_t, w_ffn_in)


def _ffn_in_kernel(h_ref, wg_ref, wu_ref, wc_ref, o_ref, oc_ref, *, job):
    h = h_ref[...]
    g = _dot(h, wg_ref[...])
    u = _dot(h, wu_ref[...])
    o_ref[...] = (g * jax.nn.sigmoid(g) * u).astype(BF)
    job.step(wc_ref, oc_ref)


def _ffn_in(h, w_ffn_in_p, w_ffn_out):
    tm, tn = 1024, 512
    nj = D_FF_PAD // tn
    job = _CastJob(w_ffn_out, 0, FF_BLK, D_FF // FF_BLK, 1, nj)
    return pl.pallas_call(
        functools.partial(_ffn_in_kernel, job=job),
        out_shape=[jax.ShapeDtypeStruct((N_TOK, D_FF_PAD), BF), job.out_shape],
        grid=(N_TOK // tm, nj),
        in_specs=[pl.BlockSpec((tm, D_MODEL), lambda i, j: (i, 0)),
                  pl.BlockSpec((D_MODEL, tn), lambda i, j: (0, j)),
                  pl.BlockSpec((D_MODEL, tn), lambda i, j: (0, j + nj)),
                  job.in_spec()],
        out_specs=[pl.BlockSpec((tm, tn), lambda i, j: (i, j)), job.out_spec()],
        compiler_params=_cparams(("arbitrary", "arbitrary")),
    )(h, w_ffn_in_p, w_ffn_in_p, w_ffn_out)


def _resid_proj_kernel(a_ref, w_ref, *refs, gate_idx, n_prompt_tiles, nk):
    if nk == 1:
        *x_refs, mod_ref, o_ref = refs
    else:
        *x_refs, mod_ref, o_ref, acc_ref = refs
    part = _dot(a_ref[...], w_ref[...])
    last = True
    if nk > 1:
        k = pl.program_id(2)
        last = k == nk - 1

        @pl.when(k == 0)
        def _():
            acc_ref[...] = part

        @pl.when((k > 0) & jnp.logical_not(last))
        def _():
            acc_ref[...] += part

    def finish(x_ref):
        total = part if nk == 1 else acc_ref[...] + part
        o_ref[...] = x_ref[...] + mod_ref[gate_idx:gate_idx + 1, :] * total

    if len(x_refs) == 1:
        pl.when(jnp.asarray(last))(lambda: finish(x_refs[0]))
    else:
        is_prompt = pl.program_id(0) < n_prompt_tiles
        pl.when(last & is_prompt)(lambda: finish(x_refs[0]))
        pl.when(last & jnp.logical_not(is_prompt))(lambda: finish(x_refs[1]))


def _resid_proj(a, w, xs, mod, gate_idx, tn, tk, row0, n_rows):
    tm = 1024
    kdim = a.shape[1]
    nk = kdim // tk
    r0 = row0 // tm
    if len(xs) == 1:
        x_specs = [pl.BlockSpec((tm, tn), lambda i, j, k: (r0 + i, j))]
    else:
        x_specs = _split_specs(tm, tn, lambda j, k: j)
    return pl.pallas_call(
        functools.partial(_resid_proj_kernel, gate_idx=gate_idx, n_prompt_tiles=N_PROMPT // tm, nk=nk),
        out_shape=jax.ShapeDtypeStruct((n_rows, D_MODEL), F32),
        grid=(n_rows // tm, D_MODEL // tn, nk),
        in_specs=[pl.BlockSpec((tm, tk), lambda i, j, k: (r0 + i, k)),
                  pl.BlockSpec((tk, tn), lambda i, j, k: (k, j))] + x_specs + [
                  pl.BlockSpec((None, 6, tn), lambda i, j, k: (_mod_row(r0 + i, tm), 0, j))],
        out_specs=pl.BlockSpec((tm, tn), lambda i, j, k: (i, j)),
        scratch_shapes=[pltpu.VMEM((tm, tn), F32)] if nk > 1 else [],
        compiler_params=_cparams(("parallel", "parallel", "arbitrary")),
    )(a, w, *xs, mod)


def _merge_kernel(yap_ref, yas_ref, ybp_ref, ybs_ref, wa_ref, wb_ref, gla_ref, glb_ref, o_ref, *, n_prompt_tiles):
    is_prompt = pl.program_id(1) < n_prompt_tiles
    ya = jnp.where(is_prompt, yap_ref[...], yas_ref[...])
    yb = jnp.where(is_prompt, ybp_ref[...], ybs_ref[...])
    pa = _dot(ya, wa_ref[...])
    pb = _dot(yb, wb_ref[...])
    o_ref[...] = (jax.nn.sigmoid(gla_ref[...]) * pa + jax.nn.sigmoid(glb_ref[...]) * pb).astype(BF)


def _merge(y_a, y_b, w_oa, w_ob, proj):
    tm, tn = 512, 1024
    npt = N_PROMPT // tm
    y_specs = [pl.BlockSpec((tm, D_A), lambda j, i: (jnp.minimum(i, npt - 1), 0)),
               pl.BlockSpec((tm, D_A), lambda j, i: (jnp.maximum(i - npt, 0), 0))]
    return pl.pallas_call(
        functools.partial(_merge_kernel, n_prompt_tiles=npt),
        out_shape=jax.ShapeDtypeStruct((N_TOK, D_MODEL), BF),
        grid=(D_MODEL // tn, N_TOK // tm),
        in_specs=y_specs + y_specs + [
                  pl.BlockSpec((D_A, tn), lambda j, i: (0, j)),
                  pl.BlockSpec((D_A, tn), lambda j, i: (0, j)),
                  pl.BlockSpec((tm, tn), lambda j, i: (i, C_GLA // tn + j)),
                  pl.BlockSpec((tm, tn), lambda j, i: (i, C_GLB // tn + j))],
        out_specs=pl.BlockSpec((tm, tn), lambda j, i: (i, j)),
        compiler_params=_cparams(("parallel", "arbitrary")),
    )(*y_a, *y_b, w_oa, w_ob, proj, proj)


def _rope(x, cos_ref, sin_ref):
    swapped = pltpu.roll(x, ROPE_DIM // 2, axis=1) + pltpu.roll(x, LANES - ROPE_DIM // 2, axis=1)
    return x * cos_ref[...] + swapped * sin_ref[...]


def _rope_block(i, tm):
    n_prompt_tiles = N_PROMPT // tm
    tiles_per_seq = DEC_SEQ // tm
    return jnp.where(i < n_prompt_tiles, tiles_per_seq, (i - n_prompt_tiles) % tiles_per_seq)


def _q_kernel(ql_ref, nw_ref, w_ref, hw_ref, cos_ref, sin_ref, o_ref):
    x = ql_ref[...]
    y = x * lax.rsqrt(jnp.mean(x * x, axis=-1, keepdims=True) + NORM_EPS) * nw_ref[...]
    q = _dot(y.astype(BF), w_ref[...])
    hw = hw_ref[...]
    for h in range(H_B):
        qh = q[:, h * QK_PAD:(h + 1) * QK_PAD]
        inv = lax.rsqrt(jnp.sum(qh * qh, axis=-1, keepdims=True) * (1.0 / QK_DIM) + NORM_EPS)
        qn = qh * inv * hw
        o_ref[:, h * QK_PAD:h * QK_PAD + NOPE_DIM] = qn[:, :NOPE_DIM].astype(BF)
        o_ref[:, h * QK_PAD + NOPE_DIM:(h + 1) * QK_PAD] = _rope(qn[:, NOPE_DIM:], cos_ref, sin_ref).astype(BF)


def _q_proj(proj, q_norm_w, w_uq_p, q_head_w_p, cos_t, sin_t):
    tm = QKV_TM
    return pl.pallas_call(
        _q_kernel,
        out_shape=jax.ShapeDtypeStruct((N_TOK, H_B * QK_PAD), BF),
        grid=(N_TOK // tm,),
        in_specs=[pl.BlockSpec((tm, Q_LORA), lambda i: (i, C_QLAT // Q_LORA)),
                  pl.BlockSpec((1, Q_LORA), lambda i: (0, 0)),
                  pl.BlockSpec((Q_LORA, H_B * QK_PAD), lambda i: (0, 0)),
                  pl.BlockSpec((1, QK_PAD), lambda i: (0, 0)),
                  pl.BlockSpec((tm, LANES), lambda i: (_rope_block(i, tm), 0)),
                  pl.BlockSpec((tm, LANES), lambda i: (_rope_block(i, tm), 0))],
        out_specs=pl.BlockSpec((tm, H_B * QK_PAD), lambda i: (i, 0)),
        compiler_params=_cparams(("parallel",)),
    )(proj, q_norm_w.reshape(1, Q_LORA), w_uq_p, q_head_w_p, cos_t, sin_t)


def _kv_kernel(lat_ref, kpe_ref, nw_ref, w_ref, wkn_ref, wkr_ref, cos_ref, sin_ref, *out_refs, pre_norm):
    if pre_norm:
        k_ref, v_ref, ckv_ref = out_refs
        x = lat_ref[...]
        ckv = x * lax.rsqrt(jnp.mean(x * x, axis=-1, keepdims=True) + NORM_EPS) * nw_ref[...]
        ckv_ref[...] = ckv
    else:
        k_ref, v_ref = out_refs
        ckv = lat_ref[...]
    kv = _dot(ckv.astype(BF), w_ref[...])
    v_ref[...] = kv[:, H_B * NOPE_DIM:].astype(BF)
    kpe = kpe_ref[...]
    kpe_ss = jnp.sum(kpe * kpe, axis=-1, keepdims=True)
    kpe_rot = _rope(kpe * wkr_ref[...], cos_ref, sin_ref)
    wkn = wkn_ref[...]
    for h in range(H_B):
        kn = kv[:, h * NOPE_DIM:(h + 1) * NOPE_DIM]
        inv = lax.rsqrt((jnp.sum(kn * kn, axis=-1, keepdims=True) + kpe_ss) * (1.0 / QK_DIM) + NORM_EPS)
        k_ref[:, h * QK_PAD:h * QK_PAD + NOPE_DIM] = (kn * inv * wkn).astype(BF)
        k_ref[:, h * QK_PAD + NOPE_DIM:(h + 1) * QK_PAD] = (kpe_rot * inv).astype(BF)


def _kv_proj(lat, lat_col, kpe, kpe_col, n_rows, kv_norm_w, w_ukv_p, wkn, wkr, cos_t, sin_t, rope_block, pre_norm):
    tm = QKV_TM
    out_shape = [jax.ShapeDtypeStruct((n_rows, H_B * QK_PAD), BF),
                 jax.ShapeDtypeStruct((n_rows, H_B * V_DIM), BF)]
    out_specs = [pl.BlockSpec((tm, H_B * QK_PAD), lambda i: (i, 0)),
                 pl.BlockSpec((tm, H_B * V_DIM), lambda i: (i, 0))]
    if pre_norm:
        out_shape.append(jax.ShapeDtypeStruct((n_rows, KV_LORA), F32))
        out_specs.append(pl.BlockSpec((tm, KV_LORA), lambda i: (i, 0)))
    return pl.pallas_call(
        functools.partial(_kv_kernel, pre_norm=pre_norm),
        out_shape=out_shape,
        grid=(n_rows // tm,),
        in_specs=[pl.BlockSpec((tm, KV_LORA), lambda i: (i, lat_col // KV_LORA)),
                  pl.BlockSpec((tm, LANES), lambda i: (i, kpe_col // LANES)),
                  pl.BlockSpec((1, KV_LORA), lambda i: (0, 0)),
                  pl.BlockSpec((KV_LORA, H_B * (NOPE_DIM + V_DIM)), lambda i: (0, 0)),
                  pl.BlockSpec((1, NOPE_DIM), lambda i: (0, 0)),
                  pl.BlockSpec((1, LANES), lambda i: (0, 0)),
                  pl.BlockSpec((tm, LANES), lambda i: (rope_block(i, tm), 0)),
                  pl.BlockSpec((tm, LANES), lambda i: (rope_block(i, tm), 0))],
        out_specs=out_specs,
        compiler_params=_cparams(("parallel",)),
    )(lat, kpe, kv_norm_w.reshape(1, KV_LORA), w_ukv_p, wkn, wkr, cos_t, sin_t)


_ATTN_SCALE = QK_DIM ** -0.5


def _attn_prompt_kernel(q_ref, k_ref, v_ref, o_ref, *, heads):
    for h in range(heads):
        q = q_ref[:, h * QK_PAD:(h + 1) * QK_PAD]
        k = k_ref[:, h * QK_PAD:(h + 1) * QK_PAD]
        s = _dot_nt(q, k) * _ATTN_SCALE
        e = jnp.exp(s - jnp.max(s, axis=-1, keepdims=True))
        p = e * (1.0 / jnp.sum(e, axis=-1, keepdims=True))
        o_ref[:, h * V_DIM:(h + 1) * V_DIM] = _dot(p.astype(BF), v_ref[:, h * V_DIM:(h + 1) * V_DIM]).astype(BF)


def _attn_prompt(q, k, v):
    heads = 8
    return pl.pallas_call(
        functools.partial(_attn_prompt_kernel, heads=heads),
        out_shape=jax.ShapeDtypeStruct((N_PROMPT, H_B * V_DIM), BF),
        grid=(BATCH, H_B // heads),
        in_specs=[pl.BlockSpec((SEQ, heads * QK_PAD), lambda b, g: (b, g)),
                  pl.BlockSpec((SEQ, heads * QK_PAD), lambda b, g: (b, g)),
                  pl.BlockSpec((SEQ, heads * V_DIM), lambda b, g: (b, g))],
        out_specs=pl.BlockSpec((SEQ, heads * V_DIM), lambda b, g: (b, g)),
        compiler_params=_cparams(("parallel", "parallel")),
    )(q, k, v)


def _attn_sample_kernel(q_ref, kc_ref, vc_ref, ks_ref, vs_ref, o_ref, *, heads):
    for h in range(heads):
        qk = slice(h * QK_PAD, (h + 1) * QK_PAD)
        vv = slice(h * V_DIM, (h + 1) * V_DIM)
        q = q_ref[:, qk]
        s1 = _dot_nt(q, kc_ref[:, qk]) * _ATTN_SCALE
        s2 = _dot_nt(q, ks_ref[:, qk]) * _ATTN_SCALE
        m = jnp.maximum(jnp.max(s1, axis=-1, keepdims=True), jnp.max(s2, axis=-1, keepdims=True))
        e1 = jnp.exp(s1 - m)
        e2 = jnp.exp(s2 - m)
        inv = 1.0 / (jnp.sum(e1, axis=-1, keepdims=True) + jnp.sum(e2, axis=-1, keepdims=True))
        o = _dot((e1 * inv).astype(BF), vc_ref[:, vv]) + _dot((e2 * inv).astype(BF), vs_ref[:, vv])
        o_ref[:, vv] = o.astype(BF)


def _attn_sample(q, k, v, k_ctx, v_ctx):
    tq = 512
    heads = 4
    qt = DEC_SEQ // tq
    q0 = N_PROMPT // tq
    s0 = N_PROMPT // DEC_SEQ
    return pl.pallas_call(
        functools.partial(_attn_sample_kernel, heads=heads),
        out_shape=jax.ShapeDtypeStruct((N_SAMPLE, H_B * V_DIM), BF),
        grid=(DEC_BATCH, H_B // heads, qt),
        in_specs=[pl.BlockSpec((tq, heads * QK_PAD), lambda b, g, t: (q0 + b * qt + t, g)),
                  pl.BlockSpec((PAST_LEN, heads * QK_PAD), lambda b, g, t: (b, g)),
                  pl.BlockSpec((PAST_LEN, heads * V_DIM), lambda b, g, t: (b, g)),
                  pl.BlockSpec((DEC_SEQ, heads * QK_PAD), lambda b, g, t: (s0 + b, g)),
                  pl.BlockSpec((DEC_SEQ, heads * V_DIM), lambda b, g, t: (s0 + b, g))],
        out_specs=pl.BlockSpec((tq, heads * V_DIM), lambda b, g, t: (b * qt + t, g)),
        compiler_params=_cparams(("parallel", "parallel", "arbitrary")),
    )(q, k_ctx, v_ctx, k, v)


def _head_sum(x, ones_blk):
    hi = x.astype(BF)
    lo = (x - hi.astype(F32)).astype(BF)
    return _dot(hi, ones_blk) + _dot(lo, ones_blk)


def _wkv_chunks_a(items, strict, incl, lane_lo, eye):
    C = WKV_CHUNK
    n2 = 2 * C

    def stack(x):
        return jnp.concatenate([jnp.where(lane_lo, x, 0.0), jnp.where(lane_lo, 0.0, x)], axis=0)

    row_c = lax.broadcasted_iota(jnp.int32, (C, PAIR), 0)

    def cumsum(it):
        x = it["lw"]
        s = 1
        while s < C:
            if it["d"] == 0:
                x = x + jnp.where(row_c >= s, pltpu.roll(x, s, axis=0), 0.0)
            else:
                x = x + jnp.where(row_c < C - s, pltpu.roll(x, C - s, axis=0), 0.0)
            s *= 2
        return x

    cs = [cumsum(it) for it in items]

    def operands(it, c):
        cl = c[C - 1:C, :] if it["d"] == 0 else c[0:1, :]
        e_inc = jnp.exp(c)
        e_neg = jnp.exp(-c)
        e_exc = jnp.exp(c - it["lw"])
        e_end = jnp.exp(cl - c)
        b, kd = it["b"], it["kd"]
        return dict(d=it["d"], e=jnp.exp(cl),
                    a_t=stack(-it["kk"] * e_exc).astype(BF), r_t=stack(it["r"] * e_inc),
                    b_t=stack(b * e_neg).astype(BF), k_t=stack(kd * e_neg).astype(BF),
                    bk_h=jnp.concatenate([stack(b * e_end), stack(kd * e_end)], axis=0).astype(BF),
                    v_s=stack(it["v"]).astype(BF))

    ops = [operands(it, c) for it, c in zip(items, cs)]
    scs = [_dot_nt(jnp.concatenate([o["a_t"], o["r_t"].astype(BF)], axis=0),
                   jnp.concatenate([o["b_t"], o["k_t"]], axis=0)) for o in ops]
    for o, sc in zip(ops, scs):
        d = o["d"]
        o["lp"] = jnp.where(strict[d], sc[:n2, :n2], 0.0).astype(BF)
        o["l_ak"] = jnp.where(strict[d], sc[:n2, n2:], 0.0).astype(BF)
        o["a_r"] = jnp.concatenate([jnp.where(incl[d], sc[n2:, :n2], 0.0),
                                    jnp.where(incl[d], sc[n2:, n2:], 0.0)], axis=1).astype(BF)

    ts = [eye + o["lp"].astype(F32) for o in ops]
    lps = [_dot(o["lp"], o["lp"]).astype(BF) for o in ops]
    span = 2
    while 2 * span < C:
        rs = [_dot(lp, jnp.concatenate([t.astype(BF), lp], axis=1)) for t, lp in zip(ts, lps)]
        ts = [t + r[:, :n2] for t, r in zip(ts, rs)]
        lps = [r[:, n2:].astype(BF) for r in rs]
        span *= 2
    ts = [t + _dot(lp, t.astype(BF)) for t, lp in zip(ts, lps)]

    lakvs = [_dot(o["l_ak"], o["v_s"]) for o in ops]
    wus = [_dot(t.astype(BF), jnp.concatenate([o["a_t"], lakv.astype(BF)], axis=1))
           for t, o, lakv in zip(ts, ops, lakvs)]
    zero = jnp.zeros((n2, n2), BF)
    outs = []
    for o, wu in zip(ops, wus):
        rhs = jnp.concatenate([wu.astype(BF), jnp.concatenate([zero, o["v_s"]], axis=1)], axis=0)
        qy = _dot(o["a_r"], rhs)
        mn = _dot_tn(o["bk_h"], rhs)
        q = (o["r_t"] + qy[:, :n2]).astype(BF)
        y0 = qy[:C, n2:] + qy[C:, n2:]
        qm = jnp.concatenate([q, mn[:, :n2].astype(BF)], axis=0)
        e_rows = jnp.broadcast_to(o["e"], (n2, n2)).T
        outs.append((qm, y0, mn[:, n2:], e_rows))
    return outs


def _wkv_kernel(*refs, T, n_pairs, has_s0, out_state):
    (r_ref, k_ref, v_ref, lora_ref, lg_ref, w0_ref, wl_ref, a0_ref, wa_ref, wg_ref,
     kkw_ref, ka_ref, rk_ref, lnw_ref, lnb_ref) = refs[:15]
    pos = 15
    if has_s0:
        s0_ref = refs[pos]
        pos += 1
    y_ref = refs[pos]
    pos += 1
    if out_state:
        so_ref = refs[pos]
        pos += 1
    lw_s, b_s, kd_s, kk_s, y_s, qm_s, n_s, e_s = refs[pos:]

    C = WKV_CHUNK
    nc = T // C
    lane = lax.broadcasted_iota(jnp.int32, (PAIR, PAIR), 1)
    row = lax.broadcasted_iota(jnp.int32, (PAIR, PAIR), 0)
    ones_blk = jnp.where((lane // HEAD_A) == (row // HEAD_A), 1.0, 0.0).astype(BF)
    eye = jnp.where(lane == row, 1.0, 0.0)
    lane_lo = lax.broadcasted_iota(jnp.int32, (C, PAIR), 1) < HEAD_A
    strict = (lane < row, lane > row)
    incl = (lane <= row, lane >= row)
    pairs = range(n_pairs)

    def cols(pp):
        return slice(pp * PAIR, (pp + 1) * PAIR)

    for pp in pairs:
        k = k_ref[:, cols(pp)]
        kk = k * kkw_ref[:, cols(pp)]
        kk = kk * lax.rsqrt(_head_sum(kk * kk, ones_blk) + 1e-12)
        kk_s[pp] = kk
        for d in range(2):
            lw_in = lora_ref[:, d * R_LORA:(d + 1) * R_LORA]
            la_in = lora_ref[:, (2 + d) * R_LORA:(3 + d) * R_LORA]
            x = w0_ref[d:d + 1, cols(pp)] + _dot(jnp.tanh(lw_in).astype(BF), wl_ref[d, :, cols(pp)].astype(BF))
            lw_s[2 * pp + d] = -DECAY_SCALE * jax.nn.sigmoid(x)
            a = jax.nn.sigmoid(a0_ref[d:d + 1, cols(pp)] + _dot(la_in.astype(BF), wa_ref[d, :, cols(pp)].astype(BF)))
            kd_s[2 * pp + d] = k * (1.0 + (a - 1.0) * ka_ref[:, cols(pp)])
            b_s[2 * pp + d] = kk * a

    def chunk_rows(idx):
        return pl.ds(idx * C, C) if isinstance(idx, int) else pl.ds(pl.multiple_of(idx * C, C), C)

    def phase_a(keys):
        items = []
        for pp, d, idx in keys:
            sl = chunk_rows(idx)
            items.append(dict(d=d, r=r_ref[sl, cols(pp)], v=v_ref[sl, cols(pp)], kk=kk_s[pp, sl, :],
                              lw=lw_s[2 * pp + d, sl, :], b=b_s[2 * pp + d, sl, :], kd=kd_s[2 * pp + d, sl, :]))
        outs = _wkv_chunks_a(items, strict, incl, lane_lo, eye)
        for (pp, d, idx), (qm, y0, nt, e_rows) in zip(keys, outs):
            qm_s[2 * pp + d, idx] = qm
            y_s[2 * pp + d, chunk_rows(idx), :] = y0
            n_s[2 * pp + d, idx] = nt
            e_s[2 * pp + d, idx] = e_rows

    if 2 * nc <= WKV_UNROLL:
        group = min(WKV_UNROLL // (2 * nc), n_pairs)
        for g in range(0, n_pairs, group):
            phase_a([(pp, d, t) for pp in range(g, g + group) for t in range(nc) for d in range(2)])
    else:
        per_iter = WKV_UNROLL // 2
        for pp in pairs:
            def body_a(j, carry, pp=pp):
                phase_a([(pp, d, j * per_iter + t) for t in range(per_iter) for d in range(2)])
                return carry

            lax.fori_loop(0, nc // per_iter, body_a, 0)

    chains = [(pp, d) for pp in pairs for d in range(2)]

    def body_b(i, carry):
        nxt = []
        for (pp, d), s in zip(chains, carry):
            idx = i if d == 0 else nc - 1 - i
            sl = chunk_rows(idx)
            prod = _dot(qm_s[2 * pp + d, idx], s.astype(BF))
            y_s[2 * pp + d, sl, :] += prod[:C] + prod[C:PAIR]
            nxt.append(s * e_s[2 * pp + d, idx] + prod[PAIR:] + n_s[2 * pp + d, idx])
        return tuple(nxt)

    if has_s0:
        init = tuple(s0_ref[d, pp] for pp, d in chains)
    else:
        init = tuple(jnp.zeros((PAIR, PAIR), F32) for _ in chains)
    s_fin = lax.fori_loop(0, nc, body_b, init)
    if out_state:
        for (pp, d), st in zip(chains, s_fin):
            s = st.T
            so_ref[d, 2 * pp] = s[:HEAD_A, :HEAD_A]
            so_ref[d, 2 * pp + 1] = s[HEAD_A:, HEAD_A:]

    sig_lg = jax.nn.sigmoid(lg_ref[...]).astype(BF)
    inv_n = 1.0 / HEAD_A
    for pp in pairs:
        y = y_s[2 * pp] + y_s[2 * pp + 1]
        mu = _head_sum(y, ones_blk) * inv_n
        yc = y - mu
        var = _head_sum(yc * yc, ones_blk) * inv_n
        y = yc * lax.rsqrt(var + LN_X_EPS) * lnw_ref[:, cols(pp)] + lnb_ref[:, cols(pp)]
        rr = r_ref[:, cols(pp)] * rk_ref[:, cols(pp)]
        bonus = _head_sum(rr * (kd_s[2 * pp] + kd_s[2 * pp + 1]), ones_blk) * v_ref[:, cols(pp)]
        g = _dot(sig_lg, wg_ref[:, cols(pp)].astype(BF))
        y_ref[:, cols(pp)] = ((y + bonus) * g).astype(BF)


def _wkv(proj, row0, n_seq, T, wts, s0, n_pairs):
    (w0, wl, a0, wa, wg, kkw, ka, rk, lnw, lnb) = wts
    sb = row0 // T
    nc = T // WKV_CHUNK
    gw = n_pairs * PAIR
    cb = lambda col: col // gw
    vec = pl.BlockSpec((1, gw), lambda s, p: (0, p))
    in_specs = [pl.BlockSpec((T, gw), lambda s, p: (sb + s, cb(C_R) + p)),
                pl.BlockSpec((T, gw), lambda s, p: (sb + s, cb(C_K) + p)),
                pl.BlockSpec((T, gw), lambda s, p: (sb + s, cb(C_V) + p)),
                pl.BlockSpec((T, 4 * R_LORA), lambda s, p: (sb + s, C_LORA // (4 * R_LORA))),
                pl.BlockSpec((T, R_G_PAD), lambda s, p: (sb + s, C_LG // R_G_PAD)),
                pl.BlockSpec((2, gw), lambda s, p: (0, p)),
                pl.BlockSpec((2, R_LORA, gw), lambda s, p: (0, 0, p)),
                pl.BlockSpec((2, gw), lambda s, p: (0, p)),
                pl.BlockSpec((2, R_LORA, gw), lambda s, p: (0, 0, p)),
                pl.BlockSpec((R_G_PAD, gw), lambda s, p: (0, p)),
                vec, vec, vec, vec, vec]
    args = [proj, proj, proj, proj, proj, w0, wl, a0, wa, wg, kkw, ka, rk, lnw, lnb]
    has_s0 = s0 is not None
    if has_s0:
        in_specs.append(pl.BlockSpec((None, 2, n_pairs, PAIR, PAIR), lambda s, p: (s, 0, p, 0, 0)))
        args.append(s0)
    out_shape = [jax.ShapeDtypeStruct((n_seq * T, D_A), BF)]
    out_specs = [pl.BlockSpec((T, gw), lambda s, p: (s, p))]
    out_state = not has_s0
    if out_state:
        out_shape.append(jax.ShapeDtypeStruct((n_seq, 2, H_A, HEAD_A, HEAD_A), F32))
        out_specs.append(pl.BlockSpec((None, 2, 2 * n_pairs, HEAD_A, HEAD_A), lambda s, p: (s, 0, p, 0, 0)))
    nch = 2 * n_pairs
    return pl.pallas_call(
        functools.partial(_wkv_kernel, T=T, n_pairs=n_pairs, has_s0=has_s0, out_state=out_state),
        out_shape=out_shape,
        grid=(n_seq, N_PAIR // n_pairs),
        in_specs=in_specs,
        out_specs=out_specs,
        scratch_shapes=[pltpu.VMEM((nch, T, PAIR), F32), pltpu.VMEM((nch, T, PAIR), F32),
                        pltpu.VMEM((nch, T, PAIR), F32), pltpu.VMEM((n_pairs, T, PAIR), F32),
                        pltpu.VMEM((nch, T, PAIR), F32),
                        pltpu.VMEM((nch, nc, 2 * PAIR, PAIR), BF),
                        pltpu.VMEM((nch, nc, PAIR, PAIR), F32), pltpu.VMEM((nch, nc, PAIR, PAIR), F32)],
        compiler_params=_cparams(("parallel", "arbitrary")),
    )(*args)


W_IN_BLK = 512
W_IN_SIDE = 128
W_IN_B0 = C_QLAT // W_IN_BLK
W_IN_C0 = C_GLA // W_IN_BLK
W_IN_SHIFT = R_G_PAD - R_G


def _w_in_relayout_kernel(main_ref, side_ref, o_ref):
    j = pl.program_id(0)
    row = lax.broadcasted_iota(jnp.int32, o_ref.shape, 0)
    sh = W_IN_SHIFT

    @pl.when(j < W_IN_B0)
    def _():
        valid = jnp.where(j == W_IN_B0 - 1, R_G, W_IN_BLK)
        o_ref[...] = jnp.where(row < valid, main_ref[...], 0.0).astype(BF)

    @pl.when((j >= W_IN_B0) & (j < W_IN_C0))
    def _():
        valid = jnp.where(j == W_IN_C0 - 1, ROPE_DIM, W_IN_BLK)
        x = jnp.concatenate([side_ref[W_IN_SIDE - sh:, :], main_ref[:W_IN_BLK - sh, :]], axis=0)
        o_ref[...] = jnp.where(row < valid, x, 0.0).astype(BF)

    @pl.when(j >= W_IN_C0)
    def _():
        x = jnp.concatenate([main_ref[sh:, :], side_ref[:sh, :]], axis=0)
        o_ref[...] = x.astype(BF)


def _w_in_relayout(w_in_t):
    tc = 2048
    per = W_IN_BLK // W_IN_SIDE

    def main_idx(j, c):
        return jnp.where(j >= W_IN_C0, j - 1, j), c

    def side_idx(j, c):
        return jnp.where(j >= W_IN_C0, per * j, per * jnp.maximum(j, W_IN_B0) - 1), c

    return pl.pallas_call(
        _w_in_relayout_kernel,
        out_shape=jax.ShapeDtypeStruct((D_IN_PAD, D_MODEL), BF),
        grid=(D_IN_PAD // W_IN_BLK, D_MODEL // tc),
        in_specs=[pl.BlockSpec((W_IN_BLK, tc), main_idx), pl.BlockSpec((W_IN_SIDE, tc), side_idx)],
        out_specs=pl.BlockSpec((W_IN_BLK, tc), lambda j, c: (j, c)),
        compiler_params=_cparams(("arbitrary", "arbitrary")),
    )(w_in_t, w_in_t)


def _rope_tables():
    rows = DEC_SEQ // GRID_W
    row = jnp.repeat(jnp.arange(rows), GRID_W).astype(F32)
    col = jnp.tile(jnp.arange(GRID_W), rows).astype(F32)
    n_freq = ROPE_DIM // 4
    inv = ROPE_THETA ** (-jnp.arange(n_freq, dtype=F32) / n_freq)
    ang = jnp.concatenate([row[:, None] * inv, col[:, None] * inv], axis=-1)
    cos, sin = jnp.cos(ang), jnp.sin(ang)
    pad = jnp.zeros((DEC_SEQ, LANES - ROPE_DIM), F32)
    cos_t = jnp.concatenate([cos, cos, pad], axis=-1)
    sin_t = jnp.concatenate([-sin, sin, pad], axis=-1)
    cos_t = jnp.concatenate([cos_t, jnp.ones((QKV_TM, LANES), F32)], axis=0)
    sin_t = jnp.concatenate([sin_t, jnp.zeros((QKV_TM, LANES), F32)], axis=0)
    return cos_t, sin_t


def _pair_state_in(state):
    b = state.shape[0]
    s = jnp.swapaxes(state, -1, -2).reshape(b, 2, N_PAIR, 2, HEAD_A, HEAD_A)
    z = jnp.zeros_like(s[:, :, :, 0])
    top = jnp.concatenate([s[:, :, :, 0], z], axis=-1)
    bot = jnp.concatenate([z, s[:, :, :, 1]], axis=-1)
    return jnp.concatenate([top, bot], axis=-2)


def kernel(x_prompt, x_sample, c, state_rwkv, cache_mla_ckv, cache_mla_kpe, c_ctx, w_mod, b_mod, norm_mix_w, w_in, rwkv_w0, rwkv_w_lora_b, rwkv_a0, rwkv_a_lora_b, rwkv_g_lora_b, rwkv_k_k, rwkv_k_a, rwkv_r_k, rwkv_ln_w, rwkv_ln_b, mla_q_norm_w, mla_w_uq, mla_kv_norm_w, mla_w_ukv, mla_q_head_norm, mla_k_head_norm, w_o_rwkv, w_o_mla, w_out, norm_ffn_w, w_ffn_in, w_ffn_out):
    l = 0
    xp = x_prompt.reshape(N_PROMPT, D_MODEL)
    xs = x_sample.reshape(N_SAMPLE, D_MODEL)

    w_in_t = _w_in_relayout(jnp.swapaxes(w_in[l], 0, 1))
    w_uq_p = jnp.pad(mla_w_uq[l].reshape(Q_LORA, H_B, QK_DIM), ((0, 0), (0, 0), (0, QK_PAD - QK_DIM)))
    w_uq_p = w_uq_p.reshape(Q_LORA, H_B * QK_PAD).astype(BF)
    q_head_w_p = jnp.pad(mla_q_head_norm[l], (0, QK_PAD - QK_DIM)).reshape(1, QK_PAD)
    wkv3 = mla_w_ukv[l].reshape(KV_LORA, H_B, NOPE_DIM + V_DIM)
    w_ukv_p = jnp.concatenate([wkv3[:, :, :NOPE_DIM].reshape(KV_LORA, H_B * NOPE_DIM),
                               wkv3[:, :, NOPE_DIM:].reshape(KV_LORA, H_B * V_DIM)], axis=1).astype(BF)
    wkn = mla_k_head_norm[l, :NOPE_DIM].reshape(1, NOPE_DIM)
    wkr = jnp.pad(mla_k_head_norm[l, NOPE_DIM:], (0, LANES - ROPE_DIM)).reshape(1, LANES)
    w_oa = w_o_rwkv[l].astype(BF)
    w_ob = w_o_mla[l].astype(BF)
    w_out_b = w_out[l].astype(BF)
    wg_p = jnp.pad(rwkv_g_lora_b[l], ((0, R_G_PAD - R_G), (0, 0)))
    row = lambda a: a.reshape(1, D_A)
    wkv_w = (rwkv_w0[l], rwkv_w_lora_b[l], rwkv_a0[l], rwkv_a_lora_b[l], wg_p, row(rwkv_k_k[l]), row(rwkv_k_a[l]),
             row(rwkv_r_k[l]), row(rwkv_ln_w[l]), row(rwkv_ln_b[l]))
    cos_t, sin_t = _rope_tables()

    cond8 = jnp.concatenate([c_ctx[None, :], c, jnp.zeros((8 - 1 - DEC_BATCH, D_MODEL), F32)], axis=0)
    mod = _modulation(cond8, w_mod[l], b_mod[l]).reshape(8, 6, D_MODEL)

    proj, w_ffn_in_p = _in_proj(_norm_mod((xp, xs), norm_mix_w[l], mod, 0, 1), w_in_t, w_ffn_in[l])

    ya_p, s_new = _wkv(proj, 0, BATCH, SEQ, wkv_w, None, WKV_PAIRS_PROMPT)
    ya_s = _wkv(proj, N_PROMPT, DEC_BATCH, DEC_SEQ, wkv_w, _pair_state_in(state_rwkv[:, l]), WKV_PAIRS_SAMPLE)[0]

    q = _q_proj(proj, mla_q_norm_w[l], w_uq_p, q_head_w_p, cos_t, sin_t)
    k, v, ckv = _kv_proj(proj, C_KVLAT, proj, C_KPE, N_TOK, mla_kv_norm_w[l], w_ukv_p, wkn, wkr, cos_t, sin_t,
                         _rope_block, True)
    ctx_ckv = cache_mla_ckv[:, l].reshape(DEC_BATCH * PAST_LEN, KV_LORA)
    ctx_kpe = jnp.pad(cache_mla_kpe[:, l].reshape(DEC_BATCH * PAST_LEN, ROPE_DIM), ((0, 0), (0, LANES - ROPE_DIM)))
    k_ctx, v_ctx = _kv_proj(ctx_ckv, 0, ctx_kpe, 0, DEC_BATCH * PAST_LEN, mla_kv_norm_w[l], w_ukv_p, wkn, wkr,
                            cos_t, sin_t, lambda i, tm: DEC_SEQ // tm, False)
    yb_p = _attn_prompt(q, k, v)
    yb_s = _attn_sample(q, k, v, k_ctx, v_ctx)

    mixed = _merge((ya_p, ya_s), (yb_p, yb_s), w_oa, w_ob, proj)
    x1 = _resid_proj(mixed, w_out_b, (xp, xs), mod, 2, 512, D_MODEL, 0, N_TOK)
    act, w_ffn_out_p = _ffn_in(_norm_mod((x1,), norm_ffn_w[l], mod, 3, 4), w_ffn_in_p, w_ffn_out[l])
    tk_ffn = D_FF_PAD // 4
    y_p = _resid_proj(act, w_ffn_out_p, (x1,), mod, 5, 1024, tk_ffn, 0, N_PROMPT).reshape(BATCH, SEQ, D_MODEL)
    y_s = _resid_proj(act, w_ffn_out_p, (x1,), mod, 5, 1024, tk_ffn, N_PROMPT, N_SAMPLE)
    y_s = y_s.reshape(DEC_BATCH, DEC_SEQ, D_MODEL)
    new_state = s_new[:, None]
    new_ckv = ckv[:N_PROMPT].reshape(BATCH, 1, SEQ, KV_LORA)
    new_kpe = proj[:N_PROMPT, C_KPE:C_KPE + ROPE_DIM].reshape(BATCH, 1, SEQ, ROPE_DIM)
    return y_p, y_s, new_state, new_ckv, new_kpe
```

```python
import functools

import numpy as np
import jax
import jax.numpy as jnp
from jax import lax
from jax.experimental import pallas as pl
from jax.experimental.pallas import tpu as pltpu

F32 = jnp.float32
BF = jnp.bfloat16

D_MODEL = 4096
BATCH, SEQ = 32, 256
DEC_BATCH, DEC_SEQ, PAST_LEN = 4, 1024, 512
GRID_W = 64
D_A, HEAD_A = 2048, 64
H_A = D_A // HEAD_A
R_LORA, R_G = 128, 480
LN_X_EPS = 64e-5
H_B, Q_LORA, KV_LORA = 16, 1024, 512
NOPE_DIM, ROPE_DIM, V_DIM = 128, 64, 128
QK_DIM = NOPE_DIM + ROPE_DIM
ROPE_THETA = 10000.0
D_FF = 11008
NORM_EPS = 1e-6

N_PROMPT = BATCH * SEQ
N_SAMPLE = DEC_BATCH * DEC_SEQ
N_TOK = N_PROMPT + N_SAMPLE

LANES = 128
R_G_PAD = 512
QK_PAD = 256
D_FF_PAD = 11264
FF_BLK = D_FF_PAD - D_FF
C_R, C_K, C_V = 0, 2048, 4096
C_LORA, C_LG, C_QLAT, C_KVLAT, C_KPE = 6144, 6656, 7168, 8192, 8704
C_GLA, C_GLB, D_IN_PAD = 9216, 13312, 17408
QKV_TM = 256
WKV_CHUNK = 64
WKV_UNROLL = 16
WKV_PAIRS_PROMPT = 4
WKV_PAIRS_SAMPLE = 2
DECAY_SCALE = float(np.exp(-0.5))
PAIR = 2 * HEAD_A
N_PAIR = D_A // PAIR

VMEM_LIMIT = 56 * 2**20


def _cparams(sem):
    return pltpu.CompilerParams(dimension_semantics=sem, vmem_limit_bytes=VMEM_LIMIT)


def _dot(a, b):
    return jnp.dot(a, b, preferred_element_type=F32)


def _dot_nt(a, b):
    return lax.dot_general(a, b, (((1,), (1,)), ((), ())), preferred_element_type=F32)


def _dot_tn(a, b):
    return lax.dot_general(a, b, (((0,), (0,)), ((), ())), preferred_element_type=F32)


def _mod_row(i, tm):
    n_prompt_tiles = N_PROMPT // tm
    tiles_per_seq = DEC_SEQ // tm
    return jnp.where(i < n_prompt_tiles, 0, 1 + (i - n_prompt_tiles) // tiles_per_seq)


def _mod_kernel(c_ref, w_ref, b_ref, o_ref):
    c = c_ref[...]
    s = c * jax.nn.sigmoid(c)
    o_ref[...] = _dot(s.astype(BF), w_ref[...].astype(BF)) + b_ref[...]


def _modulation(cond8, w_mod, b_mod):
    tn = 1024
    n = w_mod.shape[1]
    return pl.pallas_call(
        _mod_kernel,
        out_shape=jax.ShapeDtypeStruct((8, n), F32),
        grid=(n // tn,),
        in_specs=[pl.BlockSpec((8, D_MODEL), lambda j: (0, 0)),
                  pl.BlockSpec((D_MODEL, tn), lambda j: (0, j)),
                  pl.BlockSpec((1, tn), lambda j: (0, j))],
        out_specs=pl.BlockSpec((8, tn), lambda j: (0, j)),
        compiler_params=_cparams(("arbitrary",)),
    )(cond8, w_mod, b_mod.reshape(1, n))


NORM_ROWS = 16


NORM_COLS = 512


def _normmod(x_ref, nw_ref, mod_ref, o_ref, gs_ref, shift_idx, scale_idx):
    gs_ref[0] = jnp.broadcast_to(nw_ref[...] * (1.0 + mod_ref[scale_idx:scale_idx + 1, :]), (8, D_MODEL))
    gs_ref[1] = jnp.broadcast_to(mod_ref[shift_idx:shift_idx + 1, :], (8, D_MODEL))
    reps = NORM_ROWS // 8

    def body(r, carry):
        rows = pl.ds(pl.multiple_of(r * NORM_ROWS, NORM_ROWS), NORM_ROWS)
        accs = [jnp.zeros((NORM_ROWS, LANES), F32) for _ in range(4)]
        for c in range(D_MODEL // LANES):
            xc = x_ref[rows, c * LANES:(c + 1) * LANES]
            accs[c % 4] = accs[c % 4] + xc * xc
        acc = (accs[0] + accs[1]) + (accs[2] + accs[3])
        inv = lax.rsqrt(jnp.sum(acc, axis=-1, keepdims=True) * (1.0 / D_MODEL) + NORM_EPS)
        for c in range(D_MODEL // NORM_COLS):
            cs = slice(c * NORM_COLS, (c + 1) * NORM_COLS)
            gain = jnp.concatenate([gs_ref[0, :, cs]] * reps, axis=0)
            shift = jnp.concatenate([gs_ref[1, :, cs]] * reps, axis=0)
            o_ref[rows, cs] = (x_ref[rows, cs] * inv * gain + shift).astype(BF)
        return carry

    lax.fori_loop(0, x_ref.shape[0] // NORM_ROWS, body, 0, unroll=4)


def _split_specs(tm, width, col):
    npt = N_PROMPT // tm
    return [pl.BlockSpec((tm, width), lambda i, *_: (jnp.minimum(i, npt - 1), col(*_))),
            pl.BlockSpec((tm, width), lambda i, *_: (jnp.maximum(i - npt, 0), col(*_)))]


def _norm_mod_kernel(*refs, shift_idx, scale_idx, n_prompt_tiles):
    *x_refs, nw_ref, mod_ref, o_ref, gs_ref = refs

    def run(x_ref):
        _normmod(x_ref, nw_ref, mod_ref, o_ref, gs_ref, shift_idx, scale_idx)

    if len(x_refs) == 1:
        run(x_refs[0])
    else:
        is_prompt = pl.program_id(0) < n_prompt_tiles
        pl.when(is_prompt)(lambda: run(x_refs[0]))
        pl.when(jnp.logical_not(is_prompt))(lambda: run(x_refs[1]))


def _norm_mod(xs, norm_w, mod, shift_idx, scale_idx):
    tm = 256
    if len(xs) == 1:
        x_specs = [pl.BlockSpec((tm, D_MODEL), lambda i: (i, 0))]
    else:
        x_specs = _split_specs(tm, D_MODEL, lambda: 0)
    return pl.pallas_call(
        functools.partial(_norm_mod_kernel, shift_idx=shift_idx, scale_idx=scale_idx, n_prompt_tiles=N_PROMPT // tm),
        out_shape=jax.ShapeDtypeStruct((N_TOK, D_MODEL), BF),
        grid=(N_TOK // tm,),
        in_specs=x_specs + [pl.BlockSpec((1, D_MODEL), lambda i: (0, 0)),
                            pl.BlockSpec((None, 6, D_MODEL), lambda i: (_mod_row(i, tm), 0, 0))],
        out_specs=pl.BlockSpec((tm, D_MODEL), lambda i: (i, 0)),
        scratch_shapes=[pltpu.VMEM((2, 8, D_MODEL), F32)],
        compiler_params=_cparams(("parallel",)),
    )(*xs, norm_w.reshape(1, D_MODEL), mod)


CAST_SPLIT = 2


class _CastJob:
    def __init__(self, w, axis, blk, n_real, n_seg, nj):
        self.w, self.axis, self.blk, self.n_real, self.nj = w, axis, blk, n_real, nj
        self.n_parts = n_seg * (n_real + 1) * CAST_SPLIT
        other = w.shape[1 - axis]
        part = other // CAST_SPLIT
        self.block = (blk, part) if axis == 0 else (part, blk)
        padded = n_seg * (n_real + 1) * blk
        self.out_shape = jax.ShapeDtypeStruct((padded, other) if axis == 0 else (other, padded), BF)

    def _place(self, b, part):
        return (b, part) if self.axis == 0 else (part, b)

    def _block_of(self, i, j):
        t = jnp.minimum(i * self.nj + j, self.n_parts - 1)
        return t // CAST_SPLIT, t % CAST_SPLIT

    def in_spec(self):
        def idx(i, j):
            b, part = self._block_of(i, j)
            seg, jj = b // (self.n_real + 1), b % (self.n_real + 1)
            return self._place(seg * self.n_real + jnp.minimum(jj, self.n_real - 1), part)
        return pl.BlockSpec(self.block, idx)

    def out_spec(self):
        return pl.BlockSpec(self.block, lambda i, j: self._place(*self._block_of(i, j)))

    def step(self, x_ref, o_ref):
        t = pl.program_id(0) * self.nj + pl.program_id(1)
        is_pad = (t // CAST_SPLIT) % (self.n_real + 1) == self.n_real
        active = t < self.n_parts

        @pl.when(active & is_pad)
        def _():
            o_ref[...] = jnp.zeros_like(o_ref)

        @pl.when(active & jnp.logical_not(is_pad))
        def _():
            o_ref[...] = x_ref[...].astype(BF)


def _in_proj_kernel(h_ref, wt_ref, wc_ref, o_ref, oc_ref, *, job):
    o_ref[...] = _dot_nt(h_ref[...], wt_ref[...])
    job.step(wc_ref, oc_ref)


def _in_proj(h, w_in_t, w_ffn_in):
    tm, tn = 1024, 1024
    nj = D_IN_PAD // tn
    job = _CastJob(w_ffn_in, 1, FF_BLK, D_FF // FF_BLK, 2, nj)
    return pl.pallas_call(
        functools.partial(_in_proj_kernel, job=job),
        out_shape=[jax.ShapeDtypeStruct((N_TOK, D_IN_PAD), F32), job.out_shape],
        grid=(N_TOK // tm, nj),
        in_specs=[pl.BlockSpec((tm, D_MODEL), lambda i, j: (i, 0)),
                  pl.BlockSpec((tn, D_MODEL), lambda i, j: (j, 0)),
                  job.in_spec()],
        out_specs=[pl.BlockSpec((tm, tn), lambda i, j: (i, j)), job.out_spec()],
        compiler_params=_cparams(("arbitrary", "arbitrary")),
    )(h, w_in_t, w_ffn_in)


def _ffn_in_kernel(h_ref, wg_ref, wu_ref, wc_ref, o_ref, oc_ref, *, job):
    h = h_ref[...]
    g = _dot(h, wg_ref[...])
    u = _dot(h, wu_ref[...])
    o_ref[...] = (g * jax.nn.sigmoid(g) * u).astype(BF)
    job.step(wc_ref, oc_ref)


def _ffn_in(h, w_ffn_in_p, w_ffn_out):
    tm, tn = 1024, 512
    nj = D_FF_PAD // tn
    job = _CastJob(w_ffn_out, 0, FF_BLK, D_FF // FF_BLK, 1, nj)
    return pl.pallas_call(
        functools.partial(_ffn_in_kernel, job=job),
        out_shape=[jax.ShapeDtypeStruct((N_TOK, D_FF_PAD), BF), job.out_shape],
        grid=(N_TOK // tm, nj),
        in_specs=[pl.BlockSpec((tm, D_MODEL), lambda i, j: (i, 0)),
                  pl.BlockSpec((D_MODEL, tn), lambda i, j: (0, j)),
                  pl.BlockSpec((D_MODEL, tn), lambda i, j: (0, j + nj)),
                  job.in_spec()],
        out_specs=[pl.BlockSpec((tm, tn), lambda i, j: (i, j)), job.out_spec()],
        compiler_params=_cparams(("arbitrary", "arbitrary")),
    )(h, w_ffn_in_p, w_ffn_in_p, w_ffn_out)


def _resid_proj_kernel(a_ref, w_ref, *refs, gate_idx, n_prompt_tiles, nk):
    if nk == 1:
        *x_refs, mod_ref, o_ref = refs
    else:
        *x_refs, mod_ref, o_ref, acc_ref = refs
    part = _dot(a_ref[...], w_ref[...])
    last = True
    if nk > 1:
        k = pl.program_id(2)
        last = k == nk - 1

        @pl.when(k == 0)
        def _():
            acc_ref[...] = part

        @pl.when((k > 0) & jnp.logical_not(last))
        def _():
            acc_ref[...] += part

    def finish(x_ref):
        total = part if nk == 1 else acc_ref[...] + part
        o_ref[...] = x_ref[...] + mod_ref[gate_idx:gate_idx + 1, :] * total

    if len(x_refs) == 1:
        pl.when(jnp.asarray(last))(lambda: finish(x_refs[0]))
    else:
        is_prompt = pl.program_id(0) < n_prompt_tiles
        pl.when(last & is_prompt)(lambda: finish(x_refs[0]))
        pl.when(last & jnp.logical_not(is_prompt))(lambda: finish(x_refs[1]))


def _resid_proj(a, w, xs, mod, gate_idx, tm, tn, tk, row0, n_rows):
    kdim = a.shape[1]
    nk = kdim // tk
    r0 = row0 // tm
    if len(xs) == 1:
        x_specs = [pl.BlockSpec((tm, tn), lambda i, j, k: (r0 + i, j))]
    else:
        x_specs = _split_specs(tm, tn, lambda j, k: j)
    return pl.pallas_call(
        functools.partial(_resid_proj_kernel, gate_idx=gate_idx, n_prompt_tiles=N_PROMPT // tm, nk=nk),
        out_shape=jax.ShapeDtypeStruct((n_rows, D_MODEL), F32),
        grid=(n_rows // tm, D_MODEL // tn, nk),
        in_specs=[pl.BlockSpec((tm, tk), lambda i, j, k: (r0 + i, k)),
                  pl.BlockSpec((tk, tn), lambda i, j, k: (k, j))] + x_specs + [
                  pl.BlockSpec((None, 6, tn), lambda i, j, k: (_mod_row(r0 + i, tm), 0, j))],
        out_specs=pl.BlockSpec((tm, tn), lambda i, j, k: (i, j)),
        scratch_shapes=[pltpu.VMEM((tm, tn), F32)] if nk > 1 else [],
        compiler_params=_cparams(("parallel", "parallel", "arbitrary")),
    )(a, w, *xs, mod)


def _merge_kernel(yap_ref, yas_ref, ybp_ref, ybs_ref, wa_ref, wb_ref, gla_ref, glb_ref, o_ref, *, n_prompt_tiles):
    is_prompt = pl.program_id(1) < n_prompt_tiles
    ya = jnp.where(is_prompt, yap_ref[...], yas_ref[...])
    yb = jnp.where(is_prompt, ybp_ref[...], ybs_ref[...])
    pa = _dot(ya, wa_ref[...])
    pb = _dot(yb, wb_ref[...])
    o_ref[...] = (jax.nn.sigmoid(gla_ref[...]) * pa + jax.nn.sigmoid(glb_ref[...]) * pb).astype(BF)


def _merge(y_a, y_b, w_oa, w_ob, proj):
    tm, tn = 512, 1024
    npt = N_PROMPT // tm
    y_specs = [pl.BlockSpec((tm, D_A), lambda j, i: (jnp.minimum(i, npt - 1), 0)),
               pl.BlockSpec((tm, D_A), lambda j, i: (jnp.maximum(i - npt, 0), 0))]
    return pl.pallas_call(
        functools.partial(_merge_kernel, n_prompt_tiles=npt),
        out_shape=jax.ShapeDtypeStruct((N_TOK, D_MODEL), BF),
        grid=(D_MODEL // tn, N_TOK // tm),
        in_specs=y_specs + y_specs + [
                  pl.BlockSpec((D_A, tn), lambda j, i: (0, j)),
                  pl.BlockSpec((D_A, tn), lambda j, i: (0, j)),
                  pl.BlockSpec((tm, tn), lambda j, i: (i, C_GLA // tn + j)),
                  pl.BlockSpec((tm, tn), lambda j, i: (i, C_GLB // tn + j))],
        out_specs=pl.BlockSpec((tm, tn), lambda j, i: (i, j)),
        compiler_params=_cparams(("parallel", "arbitrary")),
    )(*y_a, *y_b, w_oa, w_ob, proj, proj)


def _rope(x, cos_ref, sin_ref):
    swapped = pltpu.roll(x, ROPE_DIM // 2, axis=1) + pltpu.roll(x, LANES - ROPE_DIM // 2, axis=1)
    return x * cos_ref[...] + swapped * sin_ref[...]


def _rope_block(i, tm):
    n_prompt_tiles = N_PROMPT // tm
    tiles_per_seq = DEC_SEQ // tm
    return jnp.where(i < n_prompt_tiles, tiles_per_seq, (i - n_prompt_tiles) % tiles_per_seq)


def _q_kernel(ql_ref, nw_ref, w_ref, hw_ref, cos_ref, sin_ref, o_ref):
    x = ql_ref[...]
    y = x * lax.rsqrt(jnp.mean(x * x, axis=-1, keepdims=True) + NORM_EPS) * nw_ref[...]
    q = _dot(y.astype(BF), w_ref[...])
    hw = hw_ref[...]
    for h in range(H_B):
        qh = q[:, h * QK_PAD:(h + 1) * QK_PAD]
        inv = lax.rsqrt(jnp.sum(qh * qh, axis=-1, keepdims=True) * (1.0 / QK_DIM) + NORM_EPS)
        qn = qh * inv * hw
        o_ref[:, h * QK_PAD:h * QK_PAD + NOPE_DIM] = qn[:, :NOPE_DIM].astype(BF)
        o_ref[:, h * QK_PAD + NOPE_DIM:(h + 1) * QK_PAD] = _rope(qn[:, NOPE_DIM:], cos_ref, sin_ref).astype(BF)


def _q_proj(proj, q_norm_w, w_uq_p, q_head_w_p, cos_t, sin_t):
    tm = QKV_TM
    return pl.pallas_call(
        _q_kernel,
        out_shape=jax.ShapeDtypeStruct((N_TOK, H_B * QK_PAD), BF),
        grid=(N_TOK // tm,),
        in_specs=[pl.BlockSpec((tm, Q_LORA), lambda i: (i, C_QLAT // Q_LORA)),
                  pl.BlockSpec((1, Q_LORA), lambda i: (0, 0)),
                  pl.BlockSpec((Q_LORA, H_B * QK_PAD), lambda i: (0, 0)),
                  pl.BlockSpec((1, QK_PAD), lambda i: (0, 0)),
                  pl.BlockSpec((tm, LANES), lambda i: (_rope_block(i, tm), 0)),
                  pl.BlockSpec((tm, LANES), lambda i: (_rope_block(i, tm), 0))],
        out_specs=pl.BlockSpec((tm, H_B * QK_PAD), lambda i: (i, 0)),
        compiler_params=_cparams(("parallel",)),
    )(proj, q_norm_w.reshape(1, Q_LORA), w_uq_p, q_head_w_p, cos_t, sin_t)


def _kv_kernel(lat_ref, kpe_ref, nw_ref, w_ref, wkn_ref, wkr_ref, cos_ref, sin_ref, *out_refs, pre_norm):
    if pre_norm:
        k_ref, v_ref, ckv_ref = out_refs
        x = lat_ref[...]
        ckv = x * lax.rsqrt(jnp.mean(x * x, axis=-1, keepdims=True) + NORM_EPS) * nw_ref[...]
        ckv_ref[...] = ckv
    else:
        k_ref, v_ref = out_refs
        ckv = lat_ref[...]
    kv = _dot(ckv.astype(BF), w_ref[...])
    v_ref[...] = kv[:, H_B * NOPE_DIM:].astype(BF)
    kpe = kpe_ref[...]
    kpe_ss = jnp.sum(kpe * kpe, axis=-1, keepdims=True)
    kpe_rot = _rope(kpe * wkr_ref[...], cos_ref, sin_ref)
    wkn = wkn_ref[...]
    for h in range(H_B):
        kn = kv[:, h * NOPE_DIM:(h + 1) * NOPE_DIM]
        inv = lax.rsqrt((jnp.sum(kn * kn, axis=-1, keepdims=True) + kpe_ss) * (1.0 / QK_DIM) + NORM_EPS)
        k_ref[:, h * QK_PAD:h * QK_PAD + NOPE_DIM] = (kn * inv * wkn).astype(BF)
        k_ref[:, h * QK_PAD + NOPE_DIM:(h + 1) * QK_PAD] = (kpe_rot * inv).astype(BF)


def _kv_proj(lat, lat_col, kpe, kpe_col, n_rows, kv_norm_w, w_ukv_p, wkn, wkr, cos_t, sin_t, rope_block, pre_norm):
    tm = QKV_TM
    out_shape = [jax.ShapeDtypeStruct((n_rows, H_B * QK_PAD), BF),
                 jax.ShapeDtypeStruct((n_rows, H_B * V_DIM), BF)]
    out_specs = [pl.BlockSpec((tm, H_B * QK_PAD), lambda i: (i, 0)),
                 pl.BlockSpec((tm, H_B * V_DIM), lambda i: (i, 0))]
    if pre_norm:
        out_shape.append(jax.ShapeDtypeStruct((n_rows, KV_LORA), F32))
        out_specs.append(pl.BlockSpec((tm, KV_LORA), lambda i: (i, 0)))
    return pl.pallas_call(
        functools.partial(_kv_kernel, pre_norm=pre_norm),
        out_shape=out_shape,
        grid=(n_rows // tm,),
        in_specs=[pl.BlockSpec((tm, KV_LORA), lambda i: (i, lat_col // KV_LORA)),
                  pl.BlockSpec((tm, LANES), lambda i: (i, kpe_col // LANES)),
                  pl.BlockSpec((1, KV_LORA), lambda i: (0, 0)),
                  pl.BlockSpec((KV_LORA, H_B * (NOPE_DIM + V_DIM)), lambda i: (0, 0)),
                  pl.BlockSpec((1, NOPE_DIM), lambda i: (0, 0)),
                  pl.BlockSpec((1, LANES), lambda i: (0, 0)),
                  pl.BlockSpec((tm, LANES), lambda i: (rope_block(i, tm), 0)),
                  pl.BlockSpec((tm, LANES), lambda i: (rope_block(i, tm), 0))],
        out_specs=out_specs,
        compiler_params=_cparams(("parallel",)),
    )(lat, kpe, kv_norm_w.reshape(1, KV_LORA), w_ukv_p, wkn, wkr, cos_t, sin_t)


_ATTN_SCALE = QK_DIM ** -0.5


def _attn_prompt_kernel(q_ref, k_ref, v_ref, o_ref, *, heads):
    for h in range(heads):
        q = q_ref[:, h * QK_PAD:(h + 1) * QK_PAD]
        k = k_ref[:, h * QK_PAD:(h + 1) * QK_PAD]
        s = _dot_nt(q, k) * _ATTN_SCALE
        e = jnp.exp(s - jnp.max(s, axis=-1, keepdims=True))
        p = e * (1.0 / jnp.sum(e, axis=-1, keepdims=True))
        o_ref[:, h * V_DIM:(h + 1) * V_DIM] = _dot(p.astype(BF), v_ref[:, h * V_DIM:(h + 1) * V_DIM]).astype(BF)


def _attn_prompt(q, k, v):
    heads = 8
    return pl.pallas_call(
        functools.partial(_attn_prompt_kernel, heads=heads),
        out_shape=jax.ShapeDtypeStruct((N_PROMPT, H_B * V_DIM), BF),
        grid=(BATCH, H_B // heads),
        in_specs=[pl.BlockSpec((SEQ, heads * QK_PAD), lambda b, g: (b, g)),
                  pl.BlockSpec((SEQ, heads * QK_PAD), lambda b, g: (b, g)),
                  pl.BlockSpec((SEQ, heads * V_DIM), lambda b, g: (b, g))],
        out_specs=pl.BlockSpec((SEQ, heads * V_DIM), lambda b, g: (b, g)),
        compiler_params=_cparams(("parallel", "parallel")),
    )(q, k, v)


def _attn_sample_kernel(q_ref, kc_ref, vc_ref, ks_ref, vs_ref, o_ref, *, heads):
    for h in range(heads):
        qk = slice(h * QK_PAD, (h + 1) * QK_PAD)
        vv = slice(h * V_DIM, (h + 1) * V_DIM)
        q = q_ref[:, qk]
        s1 = _dot_nt(q, kc_ref[:, qk]) * _ATTN_SCALE
        s2 = _dot_nt(q, ks_ref[:, qk]) * _ATTN_SCALE
        m = jnp.maximum(jnp.max(s1, axis=-1, keepdims=True), jnp.max(s2, axis=-1, keepdims=True))
        e1 = jnp.exp(s1 - m)
        e2 = jnp.exp(s2 - m)
        inv = 1.0 / (jnp.sum(e1, axis=-1, keepdims=True) + jnp.sum(e2, axis=-1, keepdims=True))
        o = _dot((e1 * inv).astype(BF), vc_ref[:, vv]) + _dot((e2 * inv).astype(BF), vs_ref[:, vv])
        o_ref[:, vv] = o.astype(BF)


def _attn_sample(q, k, v, k_ctx, v_ctx):
    tq = 512
    heads = 4
    qt = DEC_SEQ // tq
    q0 = N_PROMPT // tq
    s0 = N_PROMPT // DEC_SEQ
    return pl.pallas_call(
        functools.partial(_attn_sample_kernel, heads=heads),
        out_shape=jax.ShapeDtypeStruct((N_SAMPLE, H_B * V_DIM), BF),
        grid=(DEC_BATCH, H_B // heads, qt),
        in_specs=[pl.BlockSpec((tq, heads * QK_PAD), lambda b, g, t: (q0 + b * qt + t, g)),
                  pl.BlockSpec((PAST_LEN, heads * QK_PAD), lambda b, g, t: (b, g)),
                  pl.BlockSpec((PAST_LEN, heads * V_DIM), lambda b, g, t: (b, g)),
                  pl.BlockSpec((DEC_SEQ, heads * QK_PAD), lambda b, g, t: (s0 + b, g)),
                  pl.BlockSpec((DEC_SEQ, heads * V_DIM), lambda b, g, t: (s0 + b, g))],
        out_specs=pl.BlockSpec((tq, heads * V_DIM), lambda b, g, t: (b * qt + t, g)),
        compiler_params=_cparams(("parallel", "parallel", "arbitrary")),
    )(q, k_ctx, v_ctx, k, v)


def _head_sum(x, ones_blk):
    hi = x.astype(BF)
    lo = (x - hi.astype(F32)).astype(BF)
    return _dot(hi, ones_blk) + _dot(lo, ones_blk)


def _wkv_chunks_a(items, strict, incl, lane_lo, eye):
    C = WKV_CHUNK
    n2 = 2 * C

    def stack(x):
        return jnp.concatenate([jnp.where(lane_lo, x, 0.0), jnp.where(lane_lo, 0.0, x)], axis=0)

    row_c = lax.broadcasted_iota(jnp.int32, (C, PAIR), 0)

    def cumsum(it):
        x = it["lw"]
        s = 1
        while s < C:
            if it["d"] == 0:
                x = x + jnp.where(row_c >= s, pltpu.roll(x, s, axis=0), 0.0)
            else:
                x = x + jnp.where(row_c < C - s, pltpu.roll(x, C - s, axis=0), 0.0)
            s *= 2
        return x

    cs = [cumsum(it) for it in items]

    def operands(it, c):
        cl = c[C - 1:C, :] if it["d"] == 0 else c[0:1, :]
        e_inc = jnp.exp(c)
        e_neg = jnp.exp(-c)
        e_exc = jnp.exp(c - it["lw"])
        e_end = jnp.exp(cl - c)
        b, kd = it["b"], it["kd"]
        return dict(d=it["d"], e=jnp.exp(cl),
                    a_t=stack(-it["kk"] * e_exc).astype(BF), r_t=stack(it["r"] * e_inc),
                    b_t=stack(b * e_neg).astype(BF), k_t=stack(kd * e_neg).astype(BF),
                    bk_h=jnp.concatenate([stack(b * e_end), stack(kd * e_end)], axis=0).astype(BF),
                    v_s=stack(it["v"]).astype(BF))

    ops = [operands(it, c) for it, c in zip(items, cs)]
    scs = [_dot_nt(jnp.concatenate([o["a_t"], o["r_t"].astype(BF)], axis=0),
                   jnp.concatenate([o["b_t"], o["k_t"]], axis=0)) for o in ops]
    for o, sc in zip(ops, scs):
        d = o["d"]
        o["lp"] = jnp.where(strict[d], sc[:n2, :n2], 0.0).astype(BF)
        o["l_ak"] = jnp.where(strict[d], sc[:n2, n2:], 0.0).astype(BF)
        o["a_r"] = jnp.concatenate([jnp.where(incl[d], sc[n2:, :n2], 0.0),
                                    jnp.where(incl[d], sc[n2:, n2:], 0.0)], axis=1).astype(BF)

    ts = [eye + o["lp"].astype(F32) for o in ops]
    lps = [_dot(o["lp"], o["lp"]).astype(BF) for o in ops]
    span = 2
    while 2 * span < C:
        rs = [_dot(lp, jnp.concatenate([t.astype(BF), lp], axis=1)) for t, lp in zip(ts, lps)]
        ts = [t + r[:, :n2] for t, r in zip(ts, rs)]
        lps = [r[:, n2:].astype(BF) for r in rs]
        span *= 2
    ts = [t + _dot(lp, t.astype(BF)) for t, lp in zip(ts, lps)]

    lakvs = [_dot(o["l_ak"], o["v_s"]) for o in ops]
    wus = [_dot(t.astype(BF), jnp.concatenate([o["a_t"], lakv.astype(BF)], axis=1))
           for t, o, lakv in zip(ts, ops, lakvs)]
    zero = jnp.zeros((n2, n2), BF)
    outs = []
    for o, wu in zip(ops, wus):
        rhs = jnp.concatenate([wu.astype(BF), jnp.concatenate([zero, o["v_s"]], axis=1)], axis=0)
        qy = _dot(o["a_r"], rhs)
        mn = _dot_tn(o["bk_h"], rhs)
        q = (o["r_t"] + qy[:, :n2]).astype(BF)
        y0 = qy[:C, n2:] + qy[C:, n2:]
        qm = jnp.concatenate([q, mn[:, :n2].astype(BF)], axis=0)
        e_rows = jnp.broadcast_to(o["e"], (n2, n2)).T
        outs.append((qm, y0, mn[:, n2:], e_rows))
    return outs


def _wkv_kernel(*refs, T, n_pairs, has_s0, out_state):
    (r_ref, k_ref, v_ref, lora_ref, lg_ref, w0_ref, wl_ref, a0_ref, wa_ref, wg_ref,
     kkw_ref, ka_ref, rk_ref, lnw_ref, lnb_ref) = refs[:15]
    pos = 15
    if has_s0:
        s0_ref = refs[pos]
        pos += 1
    y_ref = refs[pos]
    pos += 1
    if out_state:
        so_ref = refs[pos]
        pos += 1
    lw_s, b_s, kd_s, kk_s, y_s, qm_s, n_s, e_s = refs[pos:]

    C = WKV_CHUNK
    nc = T // C
    lane = lax.broadcasted_iota(jnp.int32, (PAIR, PAIR), 1)
    row = lax.broadcasted_iota(jnp.int32, (PAIR, PAIR), 0)
    ones_blk = jnp.where((lane // HEAD_A) == (row // HEAD_A), 1.0, 0.0).astype(BF)
    eye = jnp.where(lane == row, 1.0, 0.0)
    lane_lo = lax.broadcasted_iota(jnp.int32, (C, PAIR), 1) < HEAD_A
    strict = (lane < row, lane > row)
    incl = (lane <= row, lane >= row)
    pairs = range(n_pairs)

    def cols(pp):
        return slice(pp * PAIR, (pp + 1) * PAIR)

    for pp in pairs:
        k = k_ref[:, cols(pp)]
        kk = k * kkw_ref[:, cols(pp)]
        kk = kk * lax.rsqrt(_head_sum(kk * kk, ones_blk) + 1e-12)
        kk_s[pp] = kk
        for d in range(2):
            lw_in = lora_ref[:, d * R_LORA:(d + 1) * R_LORA]
            la_in = lora_ref[:, (2 + d) * R_LORA:(3 + d) * R_LORA]
            x = w0_ref[d:d + 1, cols(pp)] + _dot(jnp.tanh(lw_in).astype(BF), wl_ref[d, :, cols(pp)].astype(BF))
            lw_s[2 * pp + d] = -DECAY_SCALE * jax.nn.sigmoid(x)
            a = jax.nn.sigmoid(a0_ref[d:d + 1, cols(pp)] + _dot(la_in.astype(BF), wa_ref[d, :, cols(pp)].astype(BF)))
            kd_s[2 * pp + d] = k * (1.0 + (a - 1.0) * ka_ref[:, cols(pp)])
            b_s[2 * pp + d] = kk * a

    def chunk_rows(idx):
        return pl.ds(idx * C, C) if isinstance(idx, int) else pl.ds(pl.multiple_of(idx * C, C), C)

    def phase_a(keys):
        items = []
        for pp, d, idx in keys:
            sl = chunk_rows(idx)
            items.append(dict(d=d, r=r_ref[sl, cols(pp)], v=v_ref[sl, cols(pp)], kk=kk_s[pp, sl, :],
                              lw=lw_s[2 * pp + d, sl, :], b=b_s[2 * pp + d, sl, :], kd=kd_s[2 * pp + d, sl, :]))
        outs = _wkv_chunks_a(items, strict, incl, lane_lo, eye)
        for (pp, d, idx), (qm, y0, nt, e_rows) in zip(keys, outs):
            qm_s[2 * pp + d, idx] = qm
            y_s[2 * pp + d, chunk_rows(idx), :] = y0
            n_s[2 * pp + d, idx] = nt
            e_s[2 * pp + d, idx] = e_rows

    if 2 * nc <= WKV_UNROLL:
        group = min(WKV_UNROLL // (2 * nc), n_pairs)
        for g in range(0, n_pairs, group):
            phase_a([(pp, d, t) for pp in range(g, g + group) for t in range(nc) for d in range(2)])
    else:
        per_iter = WKV_UNROLL // 2
        for pp in pairs:
            def body_a(j, carry, pp=pp):
                phase_a([(pp, d, j * per_iter + t) for t in range(per_iter) for d in range(2)])
                return carry

            lax.fori_loop(0, nc // per_iter, body_a, 0)

    chains = [(pp, d) for pp in pairs for d in range(2)]

    def body_b(i, carry):
        nxt = []
        for (pp, d), s in zip(chains, carry):
            idx = i if d == 0 else nc - 1 - i
            sl = chunk_rows(idx)
            prod = _dot(qm_s[2 * pp + d, idx], s.astype(BF))
            y_s[2 * pp + d, sl, :] += prod[:C] + prod[C:PAIR]
            nxt.append(s * e_s[2 * pp + d, idx] + prod[PAIR:] + n_s[2 * pp + d, idx])
        return tuple(nxt)

    if has_s0:
        init = tuple(s0_ref[d, pp] for pp, d in chains)
    else:
        init = tuple(jnp.zeros((PAIR, PAIR), F32) for _ in chains)
    s_fin = lax.fori_loop(0, nc, body_b, init)
    if out_state:
        for (pp, d), st in zip(chains, s_fin):
            s = st.T
            so_ref[d, 2 * pp] = s[:HEAD_A, :HEAD_A]
            so_ref[d, 2 * pp + 1] = s[HEAD_A:, HEAD_A:]

    sig_lg = jax.nn.sigmoid(lg_ref[...]).astype(BF)
    inv_n = 1.0 / HEAD_A
    for pp in pairs:
        y = y_s[2 * pp] + y_s[2 * pp + 1]
        mu = _head_sum(y, ones_blk) * inv_n
        yc = y - mu
        var = _head_sum(yc * yc, ones_blk) * inv_n
        y = yc * lax.rsqrt(var + LN_X_EPS) * lnw_ref[:, cols(pp)] + lnb_ref[:, cols(pp)]
        rr = r_ref[:, cols(pp)] * rk_ref[:, cols(pp)]
        bonus = _head_sum(rr * (kd_s[2 * pp] + kd_s[2 * pp + 1]), ones_blk) * v_ref[:, cols(pp)]
        g = _dot(sig_lg, wg_ref[:, cols(pp)].astype(BF))
        y_ref[:, cols(pp)] = ((y + bonus) * g).astype(BF)


def _wkv(proj, row0, n_seq, T, wts, s0, n_pairs):
    (w0, wl, a0, wa, wg, kkw, ka, rk, lnw, lnb) = wts
    sb = row0 // T
    nc = T // WKV_CHUNK
    gw = n_pairs * PAIR
    cb = lambda col: col // gw
    vec = pl.BlockSpec((1, gw), lambda s, p: (0, p))
    in_specs = [pl.BlockSpec((T, gw), lambda s, p: (sb + s, cb(C_R) + p)),
                pl.BlockSpec((T, gw), lambda s, p: (sb + s, cb(C_K) + p)),
                pl.BlockSpec((T, gw), lambda s, p: (sb + s, cb(C_V) + p)),
                pl.BlockSpec((T, 4 * R_LORA), lambda s, p: (sb + s, C_LORA // (4 * R_LORA))),
                pl.BlockSpec((T, R_G_PAD), lambda s, p: (sb + s, C_LG // R_G_PAD)),
                pl.BlockSpec((2, gw), lambda s, p: (0, p)),
                pl.BlockSpec((2, R_LORA, gw), lambda s, p: (0, 0, p)),
                pl.BlockSpec((2, gw), lambda s, p: (0, p)),
                pl.BlockSpec((2, R_LORA, gw), lambda s, p: (0, 0, p)),
                pl.BlockSpec((R_G_PAD, gw), lambda s, p: (0, p)),
                vec, vec, vec, vec, vec]
    args = [proj, proj, proj, proj, proj, w0, wl, a0, wa, wg, kkw, ka, rk, lnw, lnb]
    has_s0 = s0 is not None
    if has_s0:
        in_specs.append(pl.BlockSpec((None, 2, n_pairs, PAIR, PAIR), lambda s, p: (s, 0, p, 0, 0)))
        args.append(s0)
    out_shape = [jax.ShapeDtypeStruct((n_seq * T, D_A), BF)]
    out_specs = [pl.BlockSpec((T, gw), lambda s, p: (s, p))]
    out_state = not has_s0
    if out_state:
        out_shape.append(jax.ShapeDtypeStruct((n_seq, 2, H_A, HEAD_A, HEAD_A), F32))
        out_specs.append(pl.BlockSpec((None, 2, 2 * n_pairs, HEAD_A, HEAD_A), lambda s, p: (s, 0, p, 0, 0)))
    nch = 2 * n_pairs
    return pl.pallas_call(
        functools.partial(_wkv_kernel, T=T, n_pairs=n_pairs, has_s0=has_s0, out_state=out_state),
        out_shape=out_shape,
        grid=(n_seq, N_PAIR // n_pairs),
        in_specs=in_specs,
        out_specs=out_specs,
        scratch_shapes=[pltpu.VMEM((nch, T, PAIR), F32), pltpu.VMEM((nch, T, PAIR), F32),
                        pltpu.VMEM((nch, T, PAIR), F32), pltpu.VMEM((n_pairs, T, PAIR), F32),
                        pltpu.VMEM((nch, T, PAIR), F32),
                        pltpu.VMEM((nch, nc, 2 * PAIR, PAIR), BF),
                        pltpu.VMEM((nch, nc, PAIR, PAIR), F32), pltpu.VMEM((nch, nc, PAIR, PAIR), F32)],
        compiler_params=_cparams(("parallel", "arbitrary")),
    )(*args)


W_IN_BLK = 512
W_IN_SIDE = 128
W_IN_B0 = C_QLAT // W_IN_BLK
W_IN_C0 = C_GLA // W_IN_BLK
W_IN_SHIFT = R_G_PAD - R_G


def _w_in_relayout_kernel(main_ref, side_ref, o_ref):
    j = pl.program_id(0)
    row = lax.broadcasted_iota(jnp.int32, o_ref.shape, 0)
    sh = W_IN_SHIFT

    @pl.when(j < W_IN_B0)
    def _():
        valid = jnp.where(j == W_IN_B0 - 1, R_G, W_IN_BLK)
        o_ref[...] = jnp.where(row < valid, main_ref[...], 0.0).astype(BF)

    @pl.when((j >= W_IN_B0) & (j < W_IN_C0))
    def _():
        valid = jnp.where(j == W_IN_C0 - 1, ROPE_DIM, W_IN_BLK)
        x = jnp.concatenate([side_ref[W_IN_SIDE - sh:, :], main_ref[:W_IN_BLK - sh, :]], axis=0)
        o_ref[...] = jnp.where(row < valid, x, 0.0).astype(BF)

    @pl.when(j >= W_IN_C0)
    def _():
        x = jnp.concatenate([main_ref[sh:, :], side_ref[:sh, :]], axis=0)
        o_ref[...] = x.astype(BF)


def _w_in_relayout(w_in_t):
    tc = 2048
    per = W_IN_BLK // W_IN_SIDE

    def main_idx(j, c):
        return jnp.where(j >= W_IN_C0, j - 1, j), c

    def side_idx(j, c):
        return jnp.where(j >= W_IN_C0, per * j, per * jnp.maximum(j, W_IN_B0) - 1), c

    return pl.pallas_call(
        _w_in_relayout_kernel,
        out_shape=jax.ShapeDtypeStruct((D_IN_PAD, D_MODEL), BF),
        grid=(D_IN_PAD // W_IN_BLK, D_MODEL // tc),
        in_specs=[pl.BlockSpec((W_IN_BLK, tc), main_idx), pl.BlockSpec((W_IN_SIDE, tc), side_idx)],
        out_specs=pl.BlockSpec((W_IN_BLK, tc), lambda j, c: (j, c)),
        compiler_params=_cparams(("arbitrary", "arbitrary")),
    )(w_in_t, w_in_t)


def _rope_tables():
    rows = DEC_SEQ // GRID_W
    row = jnp.repeat(jnp.arange(rows), GRID_W).astype(F32)
    col = jnp.tile(jnp.arange(GRID_W), rows).astype(F32)
    n_freq = ROPE_DIM // 4
    inv = ROPE_THETA ** (-jnp.arange(n_freq, dtype=F32) / n_freq)
    ang = jnp.concatenate([row[:, None] * inv, col[:, None] * inv], axis=-1)
    cos, sin = jnp.cos(ang), jnp.sin(ang)
    pad = jnp.zeros((DEC_SEQ, LANES - ROPE_DIM), F32)
    cos_t = jnp.concatenate([cos, cos, pad], axis=-1)
    sin_t = jnp.concatenate([-sin, sin, pad], axis=-1)
    cos_t = jnp.concatenate([cos_t, jnp.ones((QKV_TM, LANES), F32)], axis=0)
    sin_t = jnp.concatenate([sin_t, jnp.zeros((QKV_TM, LANES), F32)], axis=0)
    return cos_t, sin_t


def _pair_state_in(state):
    b = state.shape[0]
    s = jnp.swapaxes(state, -1, -2).reshape(b, 2, N_PAIR, 2, HEAD_A, HEAD_A)
    z = jnp.zeros_like(s[:, :, :, 0])
    top = jnp.concatenate([s[:, :, :, 0], z], axis=-1)
    bot = jnp.concatenate([z, s[:, :, :, 1]], axis=-1)
    return jnp.concatenate([top, bot], axis=-2)


def kernel(x_prompt, x_sample, c, state_rwkv, cache_mla_ckv, cache_mla_kpe, c_ctx, w_mod, b_mod, norm_mix_w, w_in, rwkv_w0, rwkv_w_lora_b, rwkv_a0, rwkv_a_lora_b, rwkv_g_lora_b, rwkv_k_k, rwkv_k_a, rwkv_r_k, rwkv_ln_w, rwkv_ln_b, mla_q_norm_w, mla_w_uq, mla_kv_norm_w, mla_w_ukv, mla_q_head_norm, mla_k_head_norm, w_o_rwkv, w_o_mla, w_out, norm_ffn_w, w_ffn_in, w_ffn_out):
    l = 0
    xp = x_prompt.reshape(N_PROMPT, D_MODEL)
    xs = x_sample.reshape(N_SAMPLE, D_MODEL)

    w_in_t = _w_in_relayout(jnp.swapaxes(w_in[l], 0, 1))
    w_uq_p = jnp.pad(mla_w_uq[l].reshape(Q_LORA, H_B, QK_DIM), ((0, 0), (0, 0), (0, QK_PAD - QK_DIM)))
    w_uq_p = w_uq_p.reshape(Q_LORA, H_B * QK_PAD).astype(BF)
    q_head_w_p = jnp.pad(mla_q_head_norm[l], (0, QK_PAD - QK_DIM)).reshape(1, QK_PAD)
    wkv3 = mla_w_ukv[l].reshape(KV_LORA, H_B, NOPE_DIM + V_DIM)
    w_ukv_p = jnp.concatenate([wkv3[:, :, :NOPE_DIM].reshape(KV_LORA, H_B * NOPE_DIM),
                               wkv3[:, :, NOPE_DIM:].reshape(KV_LORA, H_B * V_DIM)], axis=1).astype(BF)
    wkn = mla_k_head_norm[l, :NOPE_DIM].reshape(1, NOPE_DIM)
    wkr = jnp.pad(mla_k_head_norm[l, NOPE_DIM:], (0, LANES - ROPE_DIM)).reshape(1, LANES)
    w_oa = w_o_rwkv[l].astype(BF)
    w_ob = w_o_mla[l].astype(BF)
    w_out_b = w_out[l].astype(BF)
    wg_p = jnp.pad(rwkv_g_lora_b[l], ((0, R_G_PAD - R_G), (0, 0)))
    row = lambda a: a.reshape(1, D_A)
    wkv_w = (rwkv_w0[l], rwkv_w_lora_b[l], rwkv_a0[l], rwkv_a_lora_b[l], wg_p, row(rwkv_k_k[l]), row(rwkv_k_a[l]),
             row(rwkv_r_k[l]), row(rwkv_ln_w[l]), row(rwkv_ln_b[l]))
    cos_t, sin_t = _rope_tables()

    cond8 = jnp.concatenate([c_ctx[None, :], c, jnp.zeros((8 - 1 - DEC_BATCH, D_MODEL), F32)], axis=0)
    mod = _modulation(cond8, w_mod[l], b_mod[l]).reshape(8, 6, D_MODEL)

    proj, w_ffn_in_p = _in_proj(_norm_mod((xp, xs), norm_mix_w[l], mod, 0, 1), w_in_t, w_ffn_in[l])

    ya_p, s_new = _wkv(proj, 0, BATCH, SEQ, wkv_w, None, WKV_PAIRS_PROMPT)
    ya_s = _wkv(proj, N_PROMPT, DEC_BATCH, DEC_SEQ, wkv_w, _pair_state_in(state_rwkv[:, l]), WKV_PAIRS_SAMPLE)[0]

    q = _q_proj(proj, mla_q_norm_w[l], w_uq_p, q_head_w_p, cos_t, sin_t)
    k, v, ckv = _kv_proj(proj, C_KVLAT, proj, C_KPE, N_TOK, mla_kv_norm_w[l], w_ukv_p, wkn, wkr, cos_t, sin_t,
                         _rope_block, True)
    ctx_ckv = cache_mla_ckv[:, l].reshape(DEC_BATCH * PAST_LEN, KV_LORA)
    ctx_kpe = jnp.pad(cache_mla_kpe[:, l].reshape(DEC_BATCH * PAST_LEN, ROPE_DIM), ((0, 0), (0, LANES - ROPE_DIM)))
    k_ctx, v_ctx = _kv_proj(ctx_ckv, 0, ctx_kpe, 0, DEC_BATCH * PAST_LEN, mla_kv_norm_w[l], w_ukv_p, wkn, wkr,
                            cos_t, sin_t, lambda i, tm: DEC_SEQ // tm, False)
    yb_p = _attn_prompt(q, k, v)
    yb_s = _attn_sample(q, k, v, k_ctx, v_ctx)

    mixed = _merge((ya_p, ya_s), (yb_p, yb_s), w_oa, w_ob, proj)
    x1 = _resid_proj(mixed, w_out_b, (xp, xs), mod, 2, 1024, 512, D_MODEL, 0, N_TOK)
    act, w_ffn_out_p = _ffn_in(_norm_mod((x1,), norm_ffn_w[l], mod, 3, 4), w_ffn_in_p, w_ffn_out[l])
    y_p = _resid_proj(act, w_ffn_out_p, (x1,), mod, 5, 512, 512, D_FF_PAD, 0, N_PROMPT).reshape(BATCH, SEQ, D_MODEL)
    y_s = _resid_proj(act, w_ffn_out_p, (x1,), mod, 5, 512, 512, D_FF_PAD, N_PROMPT, N_SAMPLE)
    y_s = y_s.reshape(DEC_BATCH, DEC_SEQ, D_MODEL)
    new_state = s_new[:, None]
    new_ckv = ckv[:N_PROMPT].reshape(BATCH, 1, SEQ, KV_LORA)
    new_kpe = proj[:N_PROMPT, C_KPE:C_KPE + ROPE_DIM].reshape(BATCH, 1, SEQ, ROPE_DIM)
    return y_p, y_s, new_state, new_ckv, new_kpe
```

```python
import functools

import numpy as np
import jax
import jax.numpy as jnp
from jax import lax
from jax.experimental import pallas as pl
from jax.experimental.pallas import tpu as pltpu

F32 = jnp.float32
BF = jnp.bfloat16

D_MODEL = 4096
BATCH, SEQ = 32, 256
DEC_BATCH, DEC_SEQ, PAST_LEN = 4, 1024, 512
GRID_W = 64
D_A, HEAD_A = 2048, 64
H_A = D_A // HEAD_A
R_LORA, R_G = 128, 480
LN_X_EPS = 64e-5
H_B, Q_LORA, KV_LORA = 16, 1024, 512
NOPE_DIM, ROPE_DIM, V_DIM = 128, 64, 128
QK_DIM = NOPE_DIM + ROPE_DIM
ROPE_THETA = 10000.0
D_FF = 11008
NORM_EPS = 1e-6

N_PROMPT = BATCH * SEQ
N_SAMPLE = DEC_BATCH * DEC_SEQ
N_TOK = N_PROMPT + N_SAMPLE

LANES = 128
R_G_PAD = 512
QK_PAD = 256
D_FF_PAD = 11264
FF_BLK = D_FF_PAD - D_FF
C_R, C_K, C_V = 0, 2048, 4096
C_LORA, C_LG, C_QLAT, C_KVLAT, C_KPE = 6144, 6656, 7168, 8192, 8704
C_GLA, C_GLB, D_IN_PAD = 9216, 13312, 17408
QKV_TM = 256
WKV_CHUNK = 64
WKV_UNROLL = 16
WKV_PAIRS_PROMPT = 8
WKV_PAIRS_SAMPLE = 2
DECAY_SCALE = float(np.exp(-0.5))
PAIR = 2 * HEAD_A
N_PAIR = D_A // PAIR

VMEM_LIMIT = 56 * 2**20


def _cparams(sem):
    return pltpu.CompilerParams(dimension_semantics=sem, vmem_limit_bytes=VMEM_LIMIT)


def _dot(a, b):
    return jnp.dot(a, b, preferred_element_type=F32)


def _dot_nt(a, b):
    return lax.dot_general(a, b, (((1,), (1,)), ((), ())), preferred_element_type=F32)


def _dot_tn(a, b):
    return lax.dot_general(a, b, (((0,), (0,)), ((), ())), preferred_element_type=F32)


def _mod_row(i, tm):
    n_prompt_tiles = N_PROMPT // tm
    tiles_per_seq = DEC_SEQ // tm
    return jnp.where(i < n_prompt_tiles, 0, 1 + (i - n_prompt_tiles) // tiles_per_seq)


def _mod_kernel(c_ref, w_ref, b_ref, o_ref):
    c = c_ref[...]
    s = c * jax.nn.sigmoid(c)
    o_ref[...] = _dot(s.astype(BF), w_ref[...].astype(BF)) + b_ref[...]


def _modulation(cond8, w_mod, b_mod):
    tn = 1024
    n = w_mod.shape[1]
    return pl.pallas_call(
        _mod_kernel,
        out_shape=jax.ShapeDtypeStruct((8, n), F32),
        grid=(n // tn,),
        in_specs=[pl.BlockSpec((8, D_MODEL), lambda j: (0, 0)),
                  pl.BlockSpec((D_MODEL, tn), lambda j: (0, j)),
                  pl.BlockSpec((1, tn), lambda j: (0, j))],
        out_specs=pl.BlockSpec((8, tn), lambda j: (0, j)),
        compiler_params=_cparams(("arbitrary",)),
    )(cond8, w_mod, b_mod.reshape(1, n))


NORM_ROWS = 16


NORM_COLS = 512


def _normmod(x_ref, nw_ref, mod_ref, o_ref, gs_ref, shift_idx, scale_idx):
    gs_ref[0] = jnp.broadcast_to(nw_ref[...] * (1.0 + mod_ref[scale_idx:scale_idx + 1, :]), (8, D_MODEL))
    gs_ref[1] = jnp.broadcast_to(mod_ref[shift_idx:shift_idx + 1, :], (8, D_MODEL))
    reps = NORM_ROWS // 8

    def body(r, carry):
        rows = pl.ds(pl.multiple_of(r * NORM_ROWS, NORM_ROWS), NORM_ROWS)
        accs = [jnp.zeros((NORM_ROWS, LANES), F32) for _ in range(4)]
        for c in range(D_MODEL // LANES):
            xc = x_ref[rows, c * LANES:(c + 1) * LANES]
            accs[c % 4] = accs[c % 4] + xc * xc
        acc = (accs[0] + accs[1]) + (accs[2] + accs[3])
        inv = lax.rsqrt(jnp.sum(acc, axis=-1, keepdims=True) * (1.0 / D_MODEL) + NORM_EPS)
        for c in range(D_MODEL // NORM_COLS):
            cs = slice(c * NORM_COLS, (c + 1) * NORM_COLS)
            gain = jnp.concatenate([gs_ref[0, :, cs]] * reps, axis=0)
            shift = jnp.concatenate([gs_ref[1, :, cs]] * reps, axis=0)
            o_ref[rows, cs] = (x_ref[rows, cs] * inv * gain + shift).astype(BF)
        return carry

    lax.fori_loop(0, x_ref.shape[0] // NORM_ROWS, body, 0, unroll=4)


def _split_specs(tm, width, col):
    npt = N_PROMPT // tm
    return [pl.BlockSpec((tm, width), lambda i, *_: (jnp.minimum(i, npt - 1), col(*_))),
            pl.BlockSpec((tm, width), lambda i, *_: (jnp.maximum(i - npt, 0), col(*_)))]


def _norm_mod_kernel(*refs, shift_idx, scale_idx, n_prompt_tiles):
    *x_refs, nw_ref, mod_ref, o_ref, gs_ref = refs

    def run(x_ref):
        _normmod(x_ref, nw_ref, mod_ref, o_ref, gs_ref, shift_idx, scale_idx)

    if len(x_refs) == 1:
        run(x_refs[0])
    else:
        is_prompt = pl.program_id(0) < n_prompt_tiles
        pl.when(is_prompt)(lambda: run(x_refs[0]))
        pl.when(jnp.logical_not(is_prompt))(lambda: run(x_refs[1]))


def _norm_mod(xs, norm_w, mod, shift_idx, scale_idx):
    tm = 256
    if len(xs) == 1:
        x_specs = [pl.BlockSpec((tm, D_MODEL), lambda i: (i, 0))]
    else:
        x_specs = _split_specs(tm, D_MODEL, lambda: 0)
    return pl.pallas_call(
        functools.partial(_norm_mod_kernel, shift_idx=shift_idx, scale_idx=scale_idx, n_prompt_tiles=N_PROMPT // tm),
        out_shape=jax.ShapeDtypeStruct((N_TOK, D_MODEL), BF),
        grid=(N_TOK // tm,),
        in_specs=x_specs + [pl.BlockSpec((1, D_MODEL), lambda i: (0, 0)),
                            pl.BlockSpec((None, 6, D_MODEL), lambda i: (_mod_row(i, tm), 0, 0))],
        out_specs=pl.BlockSpec((tm, D_MODEL), lambda i: (i, 0)),
        scratch_shapes=[pltpu.VMEM((2, 8, D_MODEL), F32)],
        compiler_params=_cparams(("parallel",)),
    )(*xs, norm_w.reshape(1, D_MODEL), mod)


CAST_SPLIT = 2


class _CastJob:
    def __init__(self, w, axis, blk, n_real, n_seg, nj):
        self.w, self.axis, self.blk, self.n_real, self.nj = w, axis, blk, n_real, nj
        self.n_parts = n_seg * (n_real + 1) * CAST_SPLIT
        other = w.shape[1 - axis]
        part = other // CAST_SPLIT
        self.block = (blk, part) if axis == 0 else (part, blk)
        padded = n_seg * (n_real + 1) * blk
        self.out_shape = jax.ShapeDtypeStruct((padded, other) if axis == 0 else (other, padded), BF)

    def _place(self, b, part):
        return (b, part) if self.axis == 0 else (part, b)

    def _block_of(self, i, j):
        t = jnp.minimum(i * self.nj + j, self.n_parts - 1)
        return t // CAST_SPLIT, t % CAST_SPLIT

    def in_spec(self):
        def idx(i, j):
            b, part = self._block_of(i, j)
            seg, jj = b // (self.n_real + 1), b % (self.n_real + 1)
            return self._place(seg * self.n_real + jnp.minimum(jj, self.n_real - 1), part)
        return pl.BlockSpec(self.block, idx)

    def out_spec(self):
        return pl.BlockSpec(self.block, lambda i, j: self._place(*self._block_of(i, j)))

    def step(self, x_ref, o_ref):
        t = pl.program_id(0) * self.nj + pl.program_id(1)
        is_pad = (t // CAST_SPLIT) % (self.n_real + 1) == self.n_real
        active = t < self.n_parts

        @pl.when(active & is_pad)
        def _():
            o_ref[...] = jnp.zeros_like(o_ref)

        @pl.when(active & jnp.logical_not(is_pad))
        def _():
            o_ref[...] = x_ref[...].astype(BF)


def _in_proj_kernel(h_ref, wt_ref, wc_ref, o_ref, oc_ref, *, job):
    o_ref[...] = _dot_nt(h_ref[...], wt_ref[...])
    job.step(wc_ref, oc_ref)


def _in_proj(h, w_in_t, w_ffn_in):
    tm, tn = 1024, 1024
    nj = D_IN_PAD // tn
    job = _CastJob(w_ffn_in, 1, FF_BLK, D_FF // FF_BLK, 2, nj)
    return pl.pallas_call(
        functools.partial(_in_proj_kernel, job=job),
        out_shape=[jax.ShapeDtypeStruct((N_TOK, D_IN_PAD), F32), job.out_shape],
        grid=(N_TOK // tm, nj),
        in_specs=[pl.BlockSpec((tm, D_MODEL), lambda i, j: (i, 0)),
                  pl.BlockSpec((tn, D_MODEL), lambda i, j: (j, 0)),
                  job.in_spec()],
        out_specs=[pl.BlockSpec((tm, tn), lambda i, j: (i, j)), job.out_spec()],
        compiler_params=_cparams(("arbitrary", "arbitrary")),
    )(h, w_in_t, w_ffn_in)


def _ffn_in_kernel(h_ref, wg_ref, wu_ref, wc_ref, o_ref, oc_ref, *, job):
    h = h_ref[...]
    g = _dot(h, wg_ref[...])
    u = _dot(h, wu_ref[...])
    o_ref[...] = (g * jax.nn.sigmoid(g) * u).astype(BF)
    job.step(wc_ref, oc_ref)


def _ffn_in(h, w_ffn_in_p, w_ffn_out):
    tm, tn = 1024, 512
    nj = D_FF_PAD // tn
    job = _CastJob(w_ffn_out, 0, FF_BLK, D_FF // FF_BLK, 1, nj)
    return pl.pallas_call(
        functools.partial(_ffn_in_kernel, job=job),
        out_shape=[jax.ShapeDtypeStruct((N_TOK, D_FF_PAD), BF), job.out_shape],
        grid=(N_TOK // tm, nj),
        in_specs=[pl.BlockSpec((tm, D_MODEL), lambda i, j: (i, 0)),
                  pl.BlockSpec((D_MODEL, tn), lambda i, j: (0, j)),
                  pl.BlockSpec((D_MODEL, tn), lambda i, j: (0, j + nj)),
                  job.in_spec()],
        out_specs=[pl.BlockSpec((tm, tn), lambda i, j: (i, j)), job.out_spec()],
        compiler_params=_cparams(("arbitrary", "arbitrary")),
    )(h, w_ffn_in_p, w_ffn_in_p, w_ffn_out)


def _resid_proj_kernel(a_ref, w_ref, *refs, gate_idx, n_prompt_tiles):
    *x_refs, mod_ref, o_ref = refs
    gated = mod_ref[gate_idx:gate_idx + 1, :] * _dot(a_ref[...], w_ref[...])
    if len(x_refs) == 1:
        o_ref[...] = x_refs[0][...] + gated
    else:
        is_prompt = pl.program_id(0) < n_prompt_tiles

        @pl.when(is_prompt)
        def _():
            o_ref[...] = x_refs[0][...] + gated

        @pl.when(jnp.logical_not(is_prompt))
        def _():
            o_ref[...] = x_refs[1][...] + gated


def _resid_proj(a, w, xs, mod, gate_idx, tm, tn, row0, n_rows):
    kdim = a.shape[1]
    r0 = row0 // tm
    if len(xs) == 1:
        x_specs = [pl.BlockSpec((tm, tn), lambda i, j: (r0 + i, j))]
    else:
        x_specs = _split_specs(tm, tn, lambda j: j)
    return pl.pallas_call(
        functools.partial(_resid_proj_kernel, gate_idx=gate_idx, n_prompt_tiles=N_PROMPT // tm),
        out_shape=jax.ShapeDtypeStruct((n_rows, D_MODEL), F32),
        grid=(n_rows // tm, D_MODEL // tn),
        in_specs=[pl.BlockSpec((tm, kdim), lambda i, j: (r0 + i, 0)),
                  pl.BlockSpec((kdim, tn), lambda i, j: (0, j))] + x_specs + [
                  pl.BlockSpec((None, 6, tn), lambda i, j: (_mod_row(r0 + i, tm), 0, j))],
        out_specs=pl.BlockSpec((tm, tn), lambda i, j: (i, j)),
        compiler_params=_cparams(("parallel", "arbitrary")),
    )(a, w, *xs, mod)


def _merge_kernel(yap_ref, yas_ref, ybp_ref, ybs_ref, wa_ref, wb_ref, gla_ref, glb_ref, o_ref, *, n_prompt_tiles):
    is_prompt = pl.program_id(1) < n_prompt_tiles
    ya = jnp.where(is_prompt, yap_ref[...], yas_ref[...])
    yb = jnp.where(is_prompt, ybp_ref[...], ybs_ref[...])
    pa = _dot(ya, wa_ref[...])
    pb = _dot(yb, wb_ref[...])
    o_ref[...] = (jax.nn.sigmoid(gla_ref[...]) * pa + jax.nn.sigmoid(glb_ref[...]) * pb).astype(BF)


def _merge(y_a, y_b, w_oa, w_ob, proj):
    tm, tn = 512, 1024
    npt = N_PROMPT // tm
    y_specs = [pl.BlockSpec((tm, D_A), lambda j, i: (jnp.minimum(i, npt - 1), 0)),
               pl.BlockSpec((tm, D_A), lambda j, i: (jnp.maximum(i - npt, 0), 0))]
    return pl.pallas_call(
        functools.partial(_merge_kernel, n_prompt_tiles=npt),
        out_shape=jax.ShapeDtypeStruct((N_TOK, D_MODEL), BF),
        grid=(D_MODEL // tn, N_TOK // tm),
        in_specs=y_specs + y_specs + [
                  pl.BlockSpec((D_A, tn), lambda j, i: (0, j)),
                  pl.BlockSpec((D_A, tn), lambda j, i: (0, j)),
                  pl.BlockSpec((tm, tn), lambda j, i: (i, C_GLA // tn + j)),
                  pl.BlockSpec((tm, tn), lambda j, i: (i, C_GLB // tn + j))],
        out_specs=pl.BlockSpec((tm, tn), lambda j, i: (i, j)),
        compiler_params=_cparams(("parallel", "arbitrary")),
    )(*y_a, *y_b, w_oa, w_ob, proj, proj)


def _rope(x, cos_ref, sin_ref):
    swapped = pltpu.roll(x, ROPE_DIM // 2, axis=1) + pltpu.roll(x, LANES - ROPE_DIM // 2, axis=1)
    return x * cos_ref[...] + swapped * sin_ref[...]


def _rope_block(i, tm):
    n_prompt_tiles = N_PROMPT // tm
    tiles_per_seq = DEC_SEQ // tm
    return jnp.where(i < n_prompt_tiles, tiles_per_seq, (i - n_prompt_tiles) % tiles_per_seq)


def _q_kernel(ql_ref, nw_ref, w_ref, hw_ref, cos_ref, sin_ref, o_ref):
    x = ql_ref[...]
    y = x * lax.rsqrt(jnp.mean(x * x, axis=-1, keepdims=True) + NORM_EPS) * nw_ref[...]
    q = _dot(y.astype(BF), w_ref[...])
    hw = hw_ref[...]
    for h in range(H_B):
        qh = q[:, h * QK_PAD:(h + 1) * QK_PAD]
        inv = lax.rsqrt(jnp.sum(qh * qh, axis=-1, keepdims=True) * (1.0 / QK_DIM) + NORM_EPS)
        qn = qh * inv * hw
        o_ref[:, h * QK_PAD:h * QK_PAD + NOPE_DIM] = qn[:, :NOPE_DIM].astype(BF)
        o_ref[:, h * QK_PAD + NOPE_DIM:(h + 1) * QK_PAD] = _rope(qn[:, NOPE_DIM:], cos_ref, sin_ref).astype(BF)


def _q_proj(proj, q_norm_w, w_uq_p, q_head_w_p, cos_t, sin_t):
    tm = QKV_TM
    return pl.pallas_call(
        _q_kernel,
        out_shape=jax.ShapeDtypeStruct((N_TOK, H_B * QK_PAD), BF),
        grid=(N_TOK // tm,),
        in_specs=[pl.BlockSpec((tm, Q_LORA), lambda i: (i, C_QLAT // Q_LORA)),
                  pl.BlockSpec((1, Q_LORA), lambda i: (0, 0)),
                  pl.BlockSpec((Q_LORA, H_B * QK_PAD), lambda i: (0, 0)),
                  pl.BlockSpec((1, QK_PAD), lambda i: (0, 0)),
                  pl.BlockSpec((tm, LANES), lambda i: (_rope_block(i, tm), 0)),
                  pl.BlockSpec((tm, LANES), lambda i: (_rope_block(i, tm), 0))],
        out_specs=pl.BlockSpec((tm, H_B * QK_PAD), lambda i: (i, 0)),
        compiler_params=_cparams(("parallel",)),
    )(proj, q_norm_w.reshape(1, Q_LORA), w_uq_p, q_head_w_p, cos_t, sin_t)


def _kv_kernel(lat_ref, kpe_ref, nw_ref, w_ref, wkn_ref, wkr_ref, cos_ref, sin_ref, *out_refs, pre_norm):
    if pre_norm:
        k_ref, v_ref, ckv_ref = out_refs
        x = lat_ref[...]
        ckv = x * lax.rsqrt(jnp.mean(x * x, axis=-1, keepdims=True) + NORM_EPS) * nw_ref[...]
        ckv_ref[...] = ckv
    else:
        k_ref, v_ref = out_refs
        ckv = lat_ref[...]
    kv = _dot(ckv.astype(BF), w_ref[...])
    v_ref[...] = kv[:, H_B * NOPE_DIM:].astype(BF)
    kpe = kpe_ref[...]
    kpe_ss = jnp.sum(kpe * kpe, axis=-1, keepdims=True)
    kpe_rot = _rope(kpe * wkr_ref[...], cos_ref, sin_ref)
    wkn = wkn_ref[...]
    for h in range(H_B):
        kn = kv[:, h * NOPE_DIM:(h + 1) * NOPE_DIM]
        inv = lax.rsqrt((jnp.sum(kn * kn, axis=-1, keepdims=True) + kpe_ss) * (1.0 / QK_DIM) + NORM_EPS)
        k_ref[:, h * QK_PAD:h * QK_PAD + NOPE_DIM] = (kn * inv * wkn).astype(BF)
        k_ref[:, h * QK_PAD + NOPE_DIM:(h + 1) * QK_PAD] = (kpe_rot * inv).astype(BF)


def _kv_proj(lat, lat_col, kpe, kpe_col, n_rows, kv_norm_w, w_ukv_p, wkn, wkr, cos_t, sin_t, rope_block, pre_norm):
    tm = QKV_TM
    out_shape = [jax.ShapeDtypeStruct((n_rows, H_B * QK_PAD), BF),
                 jax.ShapeDtypeStruct((n_rows, H_B * V_DIM), BF)]
    out_specs = [pl.BlockSpec((tm, H_B * QK_PAD), lambda i: (i, 0)),
                 pl.BlockSpec((tm, H_B * V_DIM), lambda i: (i, 0))]
    if pre_norm:
        out_shape.append(jax.ShapeDtypeStruct((n_rows, KV_LORA), F32))
        out_specs.append(pl.BlockSpec((tm, KV_LORA), lambda i: (i, 0)))
    return pl.pallas_call(
        functools.partial(_kv_kernel, pre_norm=pre_norm),
        out_shape=out_shape,
        grid=(n_rows // tm,),
        in_specs=[pl.BlockSpec((tm, KV_LORA), lambda i: (i, lat_col // KV_LORA)),
                  pl.BlockSpec((tm, LANES), lambda i: (i, kpe_col // LANES)),
                  pl.BlockSpec((1, KV_LORA), lambda i: (0, 0)),
                  pl.BlockSpec((KV_LORA, H_B * (NOPE_DIM + V_DIM)), lambda i: (0, 0)),
                  pl.BlockSpec((1, NOPE_DIM), lambda i: (0, 0)),
                  pl.BlockSpec((1, LANES), lambda i: (0, 0)),
                  pl.BlockSpec((tm, LANES), lambda i: (rope_block(i, tm), 0)),
                  pl.BlockSpec((tm, LANES), lambda i: (rope_block(i, tm), 0))],
        out_specs=out_specs,
        compiler_params=_cparams(("parallel",)),
    )(lat, kpe, kv_norm_w.reshape(1, KV_LORA), w_ukv_p, wkn, wkr, cos_t, sin_t)


_ATTN_SCALE = QK_DIM ** -0.5


def _attn_prompt_kernel(q_ref, k_ref, v_ref, o_ref, *, heads):
    for h in range(heads):
        q = q_ref[:, h * QK_PAD:(h + 1) * QK_PAD]
        k = k_ref[:, h * QK_PAD:(h + 1) * QK_PAD]
        s = _dot_nt(q, k) * _ATTN_SCALE
        e = jnp.exp(s - jnp.max(s, axis=-1, keepdims=True))
        p = e * (1.0 / jnp.sum(e, axis=-1, keepdims=True))
        o_ref[:, h * V_DIM:(h + 1) * V_DIM] = _dot(p.astype(BF), v_ref[:, h * V_DIM:(h + 1) * V_DIM]).astype(BF)


def _attn_prompt(q, k, v):
    heads = 8
    return pl.pallas_call(
        functools.partial(_attn_prompt_kernel, heads=heads),
        out_shape=jax.ShapeDtypeStruct((N_PROMPT, H_B * V_DIM), BF),
        grid=(BATCH, H_B // heads),
        in_specs=[pl.BlockSpec((SEQ, heads * QK_PAD), lambda b, g: (b, g)),
                  pl.BlockSpec((SEQ, heads * QK_PAD), lambda b, g: (b, g)),
                  pl.BlockSpec((SEQ, heads * V_DIM), lambda b, g: (b, g))],
        out_specs=pl.BlockSpec((SEQ, heads * V_DIM), lambda b, g: (b, g)),
        compiler_params=_cparams(("parallel", "parallel")),
    )(q, k, v)


def _attn_sample_kernel(q_ref, kc_ref, vc_ref, ks_ref, vs_ref, o_ref, *, heads):
    for h in range(heads):
        qk = slice(h * QK_PAD, (h + 1) * QK_PAD)
        vv = slice(h * V_DIM, (h + 1) * V_DIM)
        q = q_ref[:, qk]
        s1 = _dot_nt(q, kc_ref[:, qk]) * _ATTN_SCALE
        s2 = _dot_nt(q, ks_ref[:, qk]) * _ATTN_SCALE
        m = jnp.maximum(jnp.max(s1, axis=-1, keepdims=True), jnp.max(s2, axis=-1, keepdims=True))
        e1 = jnp.exp(s1 - m)
        e2 = jnp.exp(s2 - m)
        inv = 1.0 / (jnp.sum(e1, axis=-1, keepdims=True) + jnp.sum(e2, axis=-1, keepdims=True))
        o = _dot((e1 * inv).astype(BF), vc_ref[:, vv]) + _dot((e2 * inv).astype(BF), vs_ref[:, vv])
        o_ref[:, vv] = o.astype(BF)


def _attn_sample(q, k, v, k_ctx, v_ctx):
    tq = 512
    heads = 4
    qt = DEC_SEQ // tq
    q0 = N_PROMPT // tq
    s0 = N_PROMPT // DEC_SEQ
    return pl.pallas_call(
        functools.partial(_attn_sample_kernel, heads=heads),
        out_shape=jax.ShapeDtypeStruct((N_SAMPLE, H_B * V_DIM), BF),
        grid=(DEC_BATCH, H_B // heads, qt),
        in_specs=[pl.BlockSpec((tq, heads * QK_PAD), lambda b, g, t: (q0 + b * qt + t, g)),
                  pl.BlockSpec((PAST_LEN, heads * QK_PAD), lambda b, g, t: (b, g)),
                  pl.BlockSpec((PAST_LEN, heads * V_DIM), lambda b, g, t: (b, g)),
                  pl.BlockSpec((DEC_SEQ, heads * QK_PAD), lambda b, g, t: (s0 + b, g)),
                  pl.BlockSpec((DEC_SEQ, heads * V_DIM), lambda b, g, t: (s0 + b, g))],
        out_specs=pl.BlockSpec((tq, heads * V_DIM), lambda b, g, t: (b * qt + t, g)),
        compiler_params=_cparams(("parallel", "parallel", "arbitrary")),
    )(q, k_ctx, v_ctx, k, v)


def _head_sum(x, ones_blk):
    hi = x.astype(BF)
    lo = (x - hi.astype(F32)).astype(BF)
    return _dot(hi, ones_blk) + _dot(lo, ones_blk)


def _wkv_chunks_a(items, strict, incl, lane_lo, eye):
    C = WKV_CHUNK
    n2 = 2 * C

    def stack(x):
        return jnp.concatenate([jnp.where(lane_lo, x, 0.0), jnp.where(lane_lo, 0.0, x)], axis=0)

    row_c = lax.broadcasted_iota(jnp.int32, (C, PAIR), 0)

    def cumsum(it):
        x = it["lw"]
        s = 1
        while s < C:
            if it["d"] == 0:
                x = x + jnp.where(row_c >= s, pltpu.roll(x, s, axis=0), 0.0)
            else:
                x = x + jnp.where(row_c < C - s, pltpu.roll(x, C - s, axis=0), 0.0)
            s *= 2
        return x

    cs = [cumsum(it) for it in items]

    def operands(it, c):
        cl = c[C - 1:C, :] if it["d"] == 0 else c[0:1, :]
        e_inc = jnp.exp(c)
        e_neg = jnp.exp(-c)
        e_exc = jnp.exp(c - it["lw"])
        e_end = jnp.exp(cl - c)
        b, kd = it["b"], it["kd"]
        return dict(d=it["d"], e=jnp.exp(cl),
                    a_t=stack(-it["kk"] * e_exc).astype(BF), r_t=stack(it["r"] * e_inc),
                    b_t=stack(b * e_neg).astype(BF), k_t=stack(kd * e_neg).astype(BF),
                    bk_h=jnp.concatenate([stack(b * e_end), stack(kd * e_end)], axis=0).astype(BF),
                    v_s=stack(it["v"]).astype(BF))

    ops = [operands(it, c) for it, c in zip(items, cs)]
    scs = [_dot_nt(jnp.concatenate([o["a_t"], o["r_t"].astype(BF)], axis=0),
                   jnp.concatenate([o["b_t"], o["k_t"]], axis=0)) for o in ops]
    for o, sc in zip(ops, scs):
        d = o["d"]
        o["lp"] = jnp.where(strict[d], sc[:n2, :n2], 0.0).astype(BF)
        o["l_ak"] = jnp.where(strict[d], sc[:n2, n2:], 0.0).astype(BF)
        o["a_r"] = jnp.concatenate([jnp.where(incl[d], sc[n2:, :n2], 0.0),
                                    jnp.where(incl[d], sc[n2:, n2:], 0.0)], axis=1).astype(BF)

    ts = [eye + o["lp"].astype(F32) for o in ops]
    lps = [_dot(o["lp"], o["lp"]).astype(BF) for o in ops]
    span = 2
    while 2 * span < C:
        rs = [_dot(lp, jnp.concatenate([t.astype(BF), lp], axis=1)) for t, lp in zip(ts, lps)]
        ts = [t + r[:, :n2] for t, r in zip(ts, rs)]
        lps = [r[:, n2:].astype(BF) for r in rs]
        span *= 2
    ts = [t + _dot(lp, t.astype(BF)) for t, lp in zip(ts, lps)]

    lakvs = [_dot(o["l_ak"], o["v_s"]) for o in ops]
    wus = [_dot(t.astype(BF), jnp.concatenate([o["a_t"], lakv.astype(BF)], axis=1))
           for t, o, lakv in zip(ts, ops, lakvs)]
    zero = jnp.zeros((n2, n2), BF)
    outs = []
    for o, wu in zip(ops, wus):
        rhs = jnp.concatenate([wu.astype(BF), jnp.concatenate([zero, o["v_s"]], axis=1)], axis=0)
        qy = _dot(o["a_r"], rhs)
        mn = _dot_tn(o["bk_h"], rhs)
        q = (o["r_t"] + qy[:, :n2]).astype(BF)
        y0 = qy[:C, n2:] + qy[C:, n2:]
        qm = jnp.concatenate([q, mn[:, :n2].astype(BF)], axis=0)
        e_rows = jnp.broadcast_to(o["e"], (n2, n2)).T
        outs.append((qm, y0, mn[:, n2:], e_rows))
    return outs


def _wkv_kernel(*refs, T, n_pairs, has_s0, out_state):
    (r_ref, k_ref, v_ref, lora_ref, lg_ref, w0_ref, wl_ref, a0_ref, wa_ref, wg_ref,
     kkw_ref, ka_ref, rk_ref, lnw_ref, lnb_ref) = refs[:15]
    pos = 15
    if has_s0:
        s0_ref = refs[pos]
        pos += 1
    y_ref = refs[pos]
    pos += 1
    if out_state:
        so_ref = refs[pos]
        pos += 1
    lw_s, b_s, kd_s, kk_s, y_s, qm_s, n_s, e_s = refs[pos:]

    C = WKV_CHUNK
    nc = T // C
    lane = lax.broadcasted_iota(jnp.int32, (PAIR, PAIR), 1)
    row = lax.broadcasted_iota(jnp.int32, (PAIR, PAIR), 0)
    ones_blk = jnp.where((lane // HEAD_A) == (row // HEAD_A), 1.0, 0.0).astype(BF)
    eye = jnp.where(lane == row, 1.0, 0.0)
    lane_lo = lax.broadcasted_iota(jnp.int32, (C, PAIR), 1) < HEAD_A
    strict = (lane < row, lane > row)
    incl = (lane <= row, lane >= row)
    pairs = range(n_pairs)

    def cols(pp):
        return slice(pp * PAIR, (pp + 1) * PAIR)

    for pp in pairs:
        k = k_ref[:, cols(pp)]
        kk = k * kkw_ref[:, cols(pp)]
        kk = kk * lax.rsqrt(_head_sum(kk * kk, ones_blk) + 1e-12)
        kk_s[pp] = kk
        for d in range(2):
            lw_in = lora_ref[:, d * R_LORA:(d + 1) * R_LORA]
            la_in = lora_ref[:, (2 + d) * R_LORA:(3 + d) * R_LORA]
            x = w0_ref[d:d + 1, cols(pp)] + _dot(jnp.tanh(lw_in).astype(BF), wl_ref[d, :, cols(pp)].astype(BF))
            lw_s[2 * pp + d] = -DECAY_SCALE * jax.nn.sigmoid(x)
            a = jax.nn.sigmoid(a0_ref[d:d + 1, cols(pp)] + _dot(la_in.astype(BF), wa_ref[d, :, cols(pp)].astype(BF)))
            kd_s[2 * pp + d] = k * (1.0 + (a - 1.0) * ka_ref[:, cols(pp)])
            b_s[2 * pp + d] = kk * a

    def chunk_rows(idx):
        return pl.ds(idx * C, C) if isinstance(idx, int) else pl.ds(pl.multiple_of(idx * C, C), C)

    def phase_a(keys):
        items = []
        for pp, d, idx in keys:
            sl = chunk_rows(idx)
            items.append(dict(d=d, r=r_ref[sl, cols(pp)], v=v_ref[sl, cols(pp)], kk=kk_s[pp, sl, :],
                              lw=lw_s[2 * pp + d, sl, :], b=b_s[2 * pp + d, sl, :], kd=kd_s[2 * pp + d, sl, :]))
        outs = _wkv_chunks_a(items, strict, incl, lane_lo, eye)
        for (pp, d, idx), (qm, y0, nt, e_rows) in zip(keys, outs):
            qm_s[2 * pp + d, idx] = qm
            y_s[2 * pp + d, chunk_rows(idx), :] = y0
            n_s[2 * pp + d, idx] = nt
            e_s[2 * pp + d, idx] = e_rows

    if 2 * nc <= WKV_UNROLL:
        group = min(WKV_UNROLL // (2 * nc), n_pairs)
        for g in range(0, n_pairs, group):
            phase_a([(pp, d, t) for pp in range(g, g + group) for t in range(nc) for d in range(2)])
    else:
        per_iter = WKV_UNROLL // 2
        for pp in pairs:
            def body_a(j, carry, pp=pp):
                phase_a([(pp, d, j * per_iter + t) for t in range(per_iter) for d in range(2)])
                return carry

            lax.fori_loop(0, nc // per_iter, body_a, 0)

    chains = [(pp, d) for pp in pairs for d in range(2)]

    def body_b(i, carry):
        nxt = []
        for (pp, d), s in zip(chains, carry):
            idx = i if d == 0 else nc - 1 - i
            sl = chunk_rows(idx)
            prod = _dot(qm_s[2 * pp + d, idx], s.astype(BF))
            y_s[2 * pp + d, sl, :] += prod[:C] + prod[C:PAIR]
            nxt.append(s * e_s[2 * pp + d, idx] + prod[PAIR:] + n_s[2 * pp + d, idx])
        return tuple(nxt)

    if has_s0:
        init = tuple(s0_ref[d, pp] for pp, d in chains)
    else:
        init = tuple(jnp.zeros((PAIR, PAIR), F32) for _ in chains)
    s_fin = lax.fori_loop(0, nc, body_b, init)
    if out_state:
        for (pp, d), st in zip(chains, s_fin):
            s = st.T
            so_ref[d, 2 * pp] = s[:HEAD_A, :HEAD_A]
            so_ref[d, 2 * pp + 1] = s[HEAD_A:, HEAD_A:]

    sig_lg = jax.nn.sigmoid(lg_ref[...]).astype(BF)
    inv_n = 1.0 / HEAD_A
    for pp in pairs:
        y = y_s[2 * pp] + y_s[2 * pp + 1]
        mu = _head_sum(y, ones_blk) * inv_n
        yc = y - mu
        var = _head_sum(yc * yc, ones_blk) * inv_n
        y = yc * lax.rsqrt(var + LN_X_EPS) * lnw_ref[:, cols(pp)] + lnb_ref[:, cols(pp)]
        rr = r_ref[:, cols(pp)] * rk_ref[:, cols(pp)]
        bonus = _head_sum(rr * (kd_s[2 * pp] + kd_s[2 * pp + 1]), ones_blk) * v_ref[:, cols(pp)]
        g = _dot(sig_lg, wg_ref[:, cols(pp)].astype(BF))
        y_ref[:, cols(pp)] = ((y + bonus) * g).astype(BF)


def _wkv(proj, row0, n_seq, T, wts, s0, n_pairs):
    (w0, wl, a0, wa, wg, kkw, ka, rk, lnw, lnb) = wts
    sb = row0 // T
    nc = T // WKV_CHUNK
    gw = n_pairs * PAIR
    cb = lambda col: col // gw
    vec = pl.BlockSpec((1, gw), lambda s, p: (0, p))
    in_specs = [pl.BlockSpec((T, gw), lambda s, p: (sb + s, cb(C_R) + p)),
                pl.BlockSpec((T, gw), lambda s, p: (sb + s, cb(C_K) + p)),
                pl.BlockSpec((T, gw), lambda s, p: (sb + s, cb(C_V) + p)),
                pl.BlockSpec((T, 4 * R_LORA), lambda s, p: (sb + s, C_LORA // (4 * R_LORA))),
                pl.BlockSpec((T, R_G_PAD), lambda s, p: (sb + s, C_LG // R_G_PAD)),
                pl.BlockSpec((2, gw), lambda s, p: (0, p)),
                pl.BlockSpec((2, R_LORA, gw), lambda s, p: (0, 0, p)),
                pl.BlockSpec((2, gw), lambda s, p: (0, p)),
                pl.BlockSpec((2, R_LORA, gw), lambda s, p: (0, 0, p)),
                pl.BlockSpec((R_G_PAD, gw), lambda s, p: (0, p)),
                vec, vec, vec, vec, vec]
    args = [proj, proj, proj, proj, proj, w0, wl, a0, wa, wg, kkw, ka, rk, lnw, lnb]
    has_s0 = s0 is not None
    if has_s0:
        in_specs.append(pl.BlockSpec((None, 2, n_pairs, PAIR, PAIR), lambda s, p: (s, 0, p, 0, 0)))
        args.append(s0)
    out_shape = [jax.ShapeDtypeStruct((n_seq * T, D_A), BF)]
    out_specs = [pl.BlockSpec((T, gw), lambda s, p: (s, p))]
    out_state = not has_s0
    if out_state:
        out_shape.append(jax.ShapeDtypeStruct((n_seq, 2, H_A, HEAD_A, HEAD_A), F32))
        out_specs.append(pl.BlockSpec((None, 2, 2 * n_pairs, HEAD_A, HEAD_A), lambda s, p: (s, 0, p, 0, 0)))
    nch = 2 * n_pairs
    return pl.pallas_call(
        functools.partial(_wkv_kernel, T=T, n_pairs=n_pairs, has_s0=has_s0, out_state=out_state),
        out_shape=out_shape,
        grid=(n_seq, N_PAIR // n_pairs),
        in_specs=in_specs,
        out_specs=out_specs,
        scratch_shapes=[pltpu.VMEM((nch, T, PAIR), F32), pltpu.VMEM((nch, T, PAIR), F32),
                        pltpu.VMEM((nch, T, PAIR), F32), pltpu.VMEM((n_pairs, T, PAIR), F32),
                        pltpu.VMEM((nch, T, PAIR), F32),
                        pltpu.VMEM((nch, nc, 2 * PAIR, PAIR), BF),
                        pltpu.VMEM((nch, nc, PAIR, PAIR), F32), pltpu.VMEM((nch, nc, PAIR, PAIR), F32)],
        compiler_params=_cparams(("parallel", "arbitrary")),
    )(*args)


W_IN_BLK = 512
W_IN_SIDE = 128
W_IN_B0 = C_QLAT // W_IN_BLK
W_IN_C0 = C_GLA // W_IN_BLK
W_IN_SHIFT = R_G_PAD - R_G


def _w_in_relayout_kernel(main_ref, side_ref, o_ref):
    j = pl.program_id(0)
    row = lax.broadcasted_iota(jnp.int32, o_ref.shape, 0)
    sh = W_IN_SHIFT

    @pl.when(j < W_IN_B0)
    def _():
        valid = jnp.where(j == W_IN_B0 - 1, R_G, W_IN_BLK)
        o_ref[...] = jnp.where(row < valid, main_ref[...], 0.0).astype(BF)

    @pl.when((j >= W_IN_B0) & (j < W_IN_C0))
    def _():
        valid = jnp.where(j == W_IN_C0 - 1, ROPE_DIM, W_IN_BLK)
        x = jnp.concatenate([side_ref[W_IN_SIDE - sh:, :], main_ref[:W_IN_BLK - sh, :]], axis=0)
        o_ref[...] = jnp.where(row < valid, x, 0.0).astype(BF)

    @pl.when(j >= W_IN_C0)
    def _():
        x = jnp.concatenate([main_ref[sh:, :], side_ref[:sh, :]], axis=0)
        o_ref[...] = x.astype(BF)


def _w_in_relayout(w_in_t):
    tc = 2048
    per = W_IN_BLK // W_IN_SIDE

    def main_idx(j, c):
        return jnp.where(j >= W_IN_C0, j - 1, j), c

    def side_idx(j, c):
        return jnp.where(j >= W_IN_C0, per * j, per * jnp.maximum(j, W_IN_B0) - 1), c

    return pl.pallas_call(
        _w_in_relayout_kernel,
        out_shape=jax.ShapeDtypeStruct((D_IN_PAD, D_MODEL), BF),
        grid=(D_IN_PAD // W_IN_BLK, D_MODEL // tc),
        in_specs=[pl.BlockSpec((W_IN_BLK, tc), main_idx), pl.BlockSpec((W_IN_SIDE, tc), side_idx)],
        out_specs=pl.BlockSpec((W_IN_BLK, tc), lambda j, c: (j, c)),
        compiler_params=_cparams(("arbitrary", "arbitrary")),
    )(w_in_t, w_in_t)


def _rope_tables():
    rows = DEC_SEQ // GRID_W
    row = jnp.repeat(jnp.arange(rows), GRID_W).astype(F32)
    col = jnp.tile(jnp.arange(GRID_W), rows).astype(F32)
    n_freq = ROPE_DIM // 4
    inv = ROPE_THETA ** (-jnp.arange(n_freq, dtype=F32) / n_freq)
    ang = jnp.concatenate([row[:, None] * inv, col[:, None] * inv], axis=-1)
    cos, sin = jnp.cos(ang), jnp.sin(ang)
    pad = jnp.zeros((DEC_SEQ, LANES - ROPE_DIM), F32)
    cos_t = jnp.concatenate([cos, cos, pad], axis=-1)
    sin_t = jnp.concatenate([-sin, sin, pad], axis=-1)
    cos_t = jnp.concatenate([cos_t, jnp.ones((QKV_TM, LANES), F32)], axis=0)
    sin_t = jnp.concatenate([sin_t, jnp.zeros((QKV_TM, LANES), F32)], axis=0)
    return cos_t, sin_t


def _pair_state_in(state):
    b = state.shape[0]
    s = jnp.swapaxes(state, -1, -2).reshape(b, 2, N_PAIR, 2, HEAD_A, HEAD_A)
    z = jnp.zeros_like(s[:, :, :, 0])
    top = jnp.concatenate([s[:, :, :, 0], z], axis=-1)
    bot = jnp.concatenate([z, s[:, :, :, 1]], axis=-1)
    return jnp.concatenate([top, bot], axis=-2)


def kernel(x_prompt, x_sample, c, state_rwkv, cache_mla_ckv, cache_mla_kpe, c_ctx, w_mod, b_mod, norm_mix_w, w_in, rwkv_w0, rwkv_w_lora_b, rwkv_a0, rwkv_a_lora_b, rwkv_g_lora_b, rwkv_k_k, rwkv_k_a, rwkv_r_k, rwkv_ln_w, rwkv_ln_b, mla_q_norm_w, mla_w_uq, mla_kv_norm_w, mla_w_ukv, mla_q_head_norm, mla_k_head_norm, w_o_rwkv, w_o_mla, w_out, norm_ffn_w, w_ffn_in, w_ffn_out):
    l = 0
    xp = x_prompt.reshape(N_PROMPT, D_MODEL)
    xs = x_sample.reshape(N_SAMPLE, D_MODEL)

    w_in_t = _w_in_relayout(jnp.swapaxes(w_in[l], 0, 1))
    w_uq_p = jnp.pad(mla_w_uq[l].reshape(Q_LORA, H_B, QK_DIM), ((0, 0), (0, 0), (0, QK_PAD - QK_DIM)))
    w_uq_p = w_uq_p.reshape(Q_LORA, H_B * QK_PAD).astype(BF)
    q_head_w_p = jnp.pad(mla_q_head_norm[l], (0, QK_PAD - QK_DIM)).reshape(1, QK_PAD)
    wkv3 = mla_w_ukv[l].reshape(KV_LORA, H_B, NOPE_DIM + V_DIM)
    w_ukv_p = jnp.concatenate([wkv3[:, :, :NOPE_DIM].reshape(KV_LORA, H_B * NOPE_DIM),
                               wkv3[:, :, NOPE_DIM:].reshape(KV_LORA, H_B * V_DIM)], axis=1).astype(BF)
    wkn = mla_k_head_norm[l, :NOPE_DIM].reshape(1, NOPE_DIM)
    wkr = jnp.pad(mla_k_head_norm[l, NOPE_DIM:], (0, LANES - ROPE_DIM)).reshape(1, LANES)
    w_oa = w_o_rwkv[l].astype(BF)
    w_ob = w_o_mla[l].astype(BF)
    w_out_b = w_out[l].astype(BF)
    wg_p = jnp.pad(rwkv_g_lora_b[l], ((0, R_G_PAD - R_G), (0, 0)))
    row = lambda a: a.reshape(1, D_A)
    wkv_w = (rwkv_w0[l], rwkv_w_lora_b[l], rwkv_a0[l], rwkv_a_lora_b[l], wg_p, row(rwkv_k_k[l]), row(rwkv_k_a[l]),
             row(rwkv_r_k[l]), row(rwkv_ln_w[l]), row(rwkv_ln_b[l]))
    cos_t, sin_t = _rope_tables()

    cond8 = jnp.concatenate([c_ctx[None, :], c, jnp.zeros((8 - 1 - DEC_BATCH, D_MODEL), F32)], axis=0)
    mod = _modulation(cond8, w_mod[l], b_mod[l]).reshape(8, 6, D_MODEL)

    proj, w_ffn_in_p = _in_proj(_norm_mod((xp, xs), norm_mix_w[l], mod, 0, 1), w_in_t, w_ffn_in[l])

    ya_p, s_new = _wkv(proj, 0, BATCH, SEQ, wkv_w, None, WKV_PAIRS_PROMPT)
    ya_s = _wkv(proj, N_PROMPT, DEC_BATCH, DEC_SEQ, wkv_w, _pair_state_in(state_rwkv[:, l]), WKV_PAIRS_SAMPLE)[0]

    q = _q_proj(proj, mla_q_norm_w[l], w_uq_p, q_head_w_p, cos_t, sin_t)
    k, v, ckv = _kv_proj(proj, C_KVLAT, proj, C_KPE, N_TOK, mla_kv_norm_w[l], w_ukv_p, wkn, wkr, cos_t, sin_t,
                         _rope_block, True)
    ctx_ckv = cache_mla_ckv[:, l].reshape(DEC_BATCH * PAST_LEN, KV_LORA)
    ctx_kpe = jnp.pad(cache_mla_kpe[:, l].reshape(DEC_BATCH * PAST_LEN, ROPE_DIM), ((0, 0), (0, LANES - ROPE_DIM)))
    k_ctx, v_ctx = _kv_proj(ctx_ckv, 0, ctx_kpe, 0, DEC_BATCH * PAST_LEN, mla_kv_norm_w[l], w_ukv_p, wkn, wkr,
                            cos_t, sin_t, lambda i, tm: DEC_SEQ // tm, False)
    yb_p = _attn_prompt(q, k, v)
    yb_s = _attn_sample(q, k, v, k_ctx, v_ctx)

    mixed = _merge((ya_p, ya_s), (yb_p, yb_s), w_oa, w_ob, proj)
    x1 = _resid_proj(mixed, w_out_b, (xp, xs), mod, 2, 1024, 512, 0, N_TOK)
    act, w_ffn_out_p = _ffn_in(_norm_mod((x1,), norm_ffn_w[l], mod, 3, 4), w_ffn_in_p, w_ffn_out[l])
    y_p = _resid_proj(act, w_ffn_out_p, (x1,), mod, 5, 512, 512, 0, N_PROMPT).reshape(BATCH, SEQ, D_MODEL)
    y_s = _resid_proj(act, w_ffn_out_p, (x1,), mod, 5, 512, 512, N_PROMPT, N_SAMPLE)
    y_s = y_s.reshape(DEC_BATCH, DEC_SEQ, D_MODEL)
    new_state = s_new[:, None]
    new_ckv = ckv[:N_PROMPT].reshape(BATCH, 1, SEQ, KV_LORA)
    new_kpe = proj[:N_PROMPT, C_KPE:C_KPE + ROPE_DIM].reshape(BATCH, 1, SEQ, ROPE_DIM)
    return y_p, y_s, new_state, new_ckv, new_kpe
```

```python
import functools

import numpy as np
import jax
import jax.numpy as jnp
from jax import lax
from jax.experimental import pallas as pl
from jax.experimental.pallas import tpu as pltpu

F32 = jnp.float32
BF = jnp.bfloat16

D_MODEL = 4096
BATCH, SEQ = 32, 256
DEC_BATCH, DEC_SEQ, PAST_LEN = 4, 1024, 512
GRID_W = 64
D_A, HEAD_A = 2048, 64
H_A = D_A // HEAD_A
R_LORA, R_G = 128, 480
LN_X_EPS = 64e-5
H_B, Q_LORA, KV_LORA = 16, 1024, 512
NOPE_DIM, ROPE_DIM, V_DIM = 128, 64, 128
QK_DIM = NOPE_DIM + ROPE_DIM
ROPE_THETA = 10000.0
D_FF = 11008
NORM_EPS = 1e-6

N_PROMPT = BATCH * SEQ
N_SAMPLE = DEC_BATCH * DEC_SEQ
N_TOK = N_PROMPT + N_SAMPLE

LANES = 128
R_G_PAD = 512
QK_PAD = 256
D_FF_PAD = 11264
FF_BLK = D_FF_PAD - D_FF
C_R, C_K, C_V = 0, 2048, 4096
C_LORA, C_LG, C_QLAT, C_KVLAT, C_KPE = 6144, 6656, 7168, 8192, 8704
C_GLA, C_GLB, D_IN_PAD = 9216, 13312, 17408
QKV_TM = 256
WKV_CHUNK = 64
WKV_UNROLL = 16
WKV_PAIRS_PROMPT = 8
WKV_PAIRS_SAMPLE = 2
DECAY_SCALE = float(np.exp(-0.5))
PAIR = 2 * HEAD_A
N_PAIR = D_A // PAIR

V7X_VMEM_BYTES = 64 * 2**20
VMEM_LIMIT = V7X_VMEM_BYTES - 8 * 2**20


def _cparams(sem):
    return pltpu.CompilerParams(dimension_semantics=sem, vmem_limit_bytes=VMEM_LIMIT)


def _dot(a, b):
    return jnp.dot(a, b, preferred_element_type=F32)


def _dot_nt(a, b):
    return lax.dot_general(a, b, (((1,), (1,)), ((), ())), preferred_element_type=F32)


def _dot_tn(a, b):
    return lax.dot_general(a, b, (((0,), (0,)), ((), ())), preferred_element_type=F32)


def _mod_row(i, tm):
    n_prompt_tiles = N_PROMPT // tm
    tiles_per_seq = DEC_SEQ // tm
    return jnp.where(i < n_prompt_tiles, 0, 1 + (i - n_prompt_tiles) // tiles_per_seq)


def _mod_kernel(c_ref, w_ref, b_ref, o_ref):
    c = c_ref[...]
    s = c * jax.nn.sigmoid(c)
    o_ref[...] = _dot(s.astype(BF), w_ref[...].astype(BF)) + b_ref[...]


def _modulation(cond8, w_mod, b_mod):
    tn = 1024
    n = w_mod.shape[1]
    return pl.pallas_call(
        _mod_kernel,
        out_shape=jax.ShapeDtypeStruct((8, n), F32),
        grid=(n // tn,),
        in_specs=[pl.BlockSpec((8, D_MODEL), lambda j: (0, 0)),
                  pl.BlockSpec((D_MODEL, tn), lambda j: (0, j)),
                  pl.BlockSpec((1, tn), lambda j: (0, j))],
        out_specs=pl.BlockSpec((8, tn), lambda j: (0, j)),
        compiler_params=_cparams(("arbitrary",)),
    )(cond8, w_mod, b_mod.reshape(1, n))


NORM_ROWS = 16


NORM_COLS = 512


def _normmod(x_ref, nw_ref, mod_ref, o_ref, gs_ref, shift_idx, scale_idx):
    gs_ref[0] = jnp.broadcast_to(nw_ref[...] * (1.0 + mod_ref[scale_idx:scale_idx + 1, :]), (8, D_MODEL))
    gs_ref[1] = jnp.broadcast_to(mod_ref[shift_idx:shift_idx + 1, :], (8, D_MODEL))
    reps = NORM_ROWS // 8

    def body(r, carry):
        rows = pl.ds(pl.multiple_of(r * NORM_ROWS, NORM_ROWS), NORM_ROWS)
        accs = [jnp.zeros((NORM_ROWS, LANES), F32) for _ in range(4)]
        for c in range(D_MODEL // LANES):
            xc = x_ref[rows, c * LANES:(c + 1) * LANES]
            accs[c % 4] = accs[c % 4] + xc * xc
        acc = (accs[0] + accs[1]) + (accs[2] + accs[3])
        inv = lax.rsqrt(jnp.sum(acc, axis=-1, keepdims=True) * (1.0 / D_MODEL) + NORM_EPS)
        for c in range(D_MODEL // NORM_COLS):
            cs = slice(c * NORM_COLS, (c + 1) * NORM_COLS)
            gain = jnp.concatenate([gs_ref[0, :, cs]] * reps, axis=0)
            shift = jnp.concatenate([gs_ref[1, :, cs]] * reps, axis=0)
            o_ref[rows, cs] = (x_ref[rows, cs] * inv * gain + shift).astype(BF)
        return carry

    lax.fori_loop(0, x_ref.shape[0] // NORM_ROWS, body, 0, unroll=4)


def _split_specs(tm, width, col):
    npt = N_PROMPT // tm
    return [pl.BlockSpec((tm, width), lambda i, *_: (jnp.minimum(i, npt - 1), col(*_))),
            pl.BlockSpec((tm, width), lambda i, *_: (jnp.maximum(i - npt, 0), col(*_)))]


def _norm_mod_kernel(*refs, shift_idx, scale_idx, n_prompt_tiles):
    *x_refs, nw_ref, mod_ref, o_ref, gs_ref = refs

    def run(x_ref):
        _normmod(x_ref, nw_ref, mod_ref, o_ref, gs_ref, shift_idx, scale_idx)

    if len(x_refs) == 1:
        run(x_refs[0])
    else:
        is_prompt = pl.program_id(0) < n_prompt_tiles
        pl.when(is_prompt)(lambda: run(x_refs[0]))
        pl.when(jnp.logical_not(is_prompt))(lambda: run(x_refs[1]))


def _norm_mod(xs, norm_w, mod, shift_idx, scale_idx):
    tm = 256
    if len(xs) == 1:
        x_specs = [pl.BlockSpec((tm, D_MODEL), lambda i: (i, 0))]
    else:
        x_specs = _split_specs(tm, D_MODEL, lambda: 0)
    return pl.pallas_call(
        functools.partial(_norm_mod_kernel, shift_idx=shift_idx, scale_idx=scale_idx, n_prompt_tiles=N_PROMPT // tm),
        out_shape=jax.ShapeDtypeStruct((N_TOK, D_MODEL), BF),
        grid=(N_TOK // tm,),
        in_specs=x_specs + [pl.BlockSpec((1, D_MODEL), lambda i: (0, 0)),
                            pl.BlockSpec((None, 6, D_MODEL), lambda i: (_mod_row(i, tm), 0, 0))],
        out_specs=pl.BlockSpec((tm, D_MODEL), lambda i: (i, 0)),
        scratch_shapes=[pltpu.VMEM((2, 8, D_MODEL), F32)],
        compiler_params=_cparams(("parallel",)),
    )(*xs, norm_w.reshape(1, D_MODEL), mod)


CAST_SPLIT = 2


class _CastJob:
    def __init__(self, w, axis, blk, n_real, n_seg, ni, nj):
        self.w, self.axis, self.blk, self.n_real, self.nj = w, axis, blk, n_real, nj
        self.n_parts = n_seg * (n_real + 1) * CAST_SPLIT
        assert self.n_parts <= ni * nj, "the host grid has too few steps to cast every block"
        other = w.shape[1 - axis]
        part = other // CAST_SPLIT
        self.block = (blk, part) if axis == 0 else (part, blk)
        padded = n_seg * (n_real + 1) * blk
        self.out_shape = jax.ShapeDtypeStruct((padded, other) if axis == 0 else (other, padded), BF)

    def _place(self, b, part):
        return (b, part) if self.axis == 0 else (part, b)

    def _block_of(self, i, j):
        t = jnp.minimum(i * self.nj + j, self.n_parts - 1)
        return t // CAST_SPLIT, t % CAST_SPLIT

    def in_spec(self):
        def idx(i, j):
            b, part = self._block_of(i, j)
            seg, jj = b // (self.n_real + 1), b % (self.n_real + 1)
            return self._place(seg * self.n_real + jnp.minimum(jj, self.n_real - 1), part)
        return pl.BlockSpec(self.block, idx)

    def out_spec(self):
        return pl.BlockSpec(self.block, lambda i, j: self._place(*self._block_of(i, j)))

    def step(self, x_ref, o_ref):
        t = pl.program_id(0) * self.nj + pl.program_id(1)
        is_pad = (t // CAST_SPLIT) % (self.n_real + 1) == self.n_real
        active = t < self.n_parts

        @pl.when(active & is_pad)
        def _():
            o_ref[...] = jnp.zeros_like(o_ref)

        @pl.when(active & jnp.logical_not(is_pad))
        def _():
            o_ref[...] = x_ref[...].astype(BF)


def _in_proj_kernel(h_ref, wt_ref, wc_ref, o_ref, oc_ref, *, job):
    o_ref[...] = _dot_nt(h_ref[...], wt_ref[...])
    job.step(wc_ref, oc_ref)


def _in_proj(h, w_in_t, w_ffn_in):
    tm, tn = 1024, 1024
    nj = D_IN_PAD // tn
    job = _CastJob(w_ffn_in, 1, FF_BLK, D_FF // FF_BLK, 2, N_TOK // tm, nj)
    return pl.pallas_call(
        functools.partial(_in_proj_kernel, job=job),
        out_shape=[jax.ShapeDtypeStruct((N_TOK, D_IN_PAD), F32), job.out_shape],
        grid=(N_TOK // tm, nj),
        in_specs=[pl.BlockSpec((tm, D_MODEL), lambda i, j: (i, 0)),
                  pl.BlockSpec((tn, D_MODEL), lambda i, j: (j, 0)),
                  job.in_spec()],
        out_specs=[pl.BlockSpec((tm, tn), lambda i, j: (i, j)), job.out_spec()],
        compiler_params=_cparams(("arbitrary", "arbitrary")),
    )(h, w_in_t, w_ffn_in)


def _ffn_in_kernel(h_ref, wg_ref, wu_ref, wc_ref, o_ref, oc_ref, *, job):
    h = h_ref[...]
    g = _dot(h, wg_ref[...])
    u = _dot(h, wu_ref[...])
    o_ref[...] = (g * jax.nn.sigmoid(g) * u).astype(BF)
    job.step(wc_ref, oc_ref)


def _ffn_in(h, w_ffn_in_p, w_ffn_out):
    tm, tn = 1024, 512
    nj = D_FF_PAD // tn
    job = _CastJob(w_ffn_out, 0, FF_BLK, D_FF // FF_BLK, 1, N_TOK // tm, nj)
    return pl.pallas_call(
        functools.partial(_ffn_in_kernel, job=job),
        out_shape=[jax.ShapeDtypeStruct((N_TOK, D_FF_PAD), BF), job.out_shape],
        grid=(N_TOK // tm, nj),
        in_specs=[pl.BlockSpec((tm, D_MODEL), lambda i, j: (i, 0)),
                  pl.BlockSpec((D_MODEL, tn), lambda i, j: (0, j)),
                  pl.BlockSpec((D_MODEL, tn), lambda i, j: (0, j + nj)),
                  job.in_spec()],
        out_specs=[pl.BlockSpec((tm, tn), lambda i, j: (i, j)), job.out_spec()],
        compiler_params=_cparams(("arbitrary", "arbitrary")),
    )(h, w_ffn_in_p, w_ffn_in_p, w_ffn_out)


def _resid_proj_kernel(a_ref, w_ref, *refs, gate_idx, n_prompt_tiles):
    *x_refs, mod_ref, o_ref = refs
    gated = mod_ref[gate_idx:gate_idx + 1, :] * _dot(a_ref[...], w_ref[...])
    if len(x_refs) == 1:
        o_ref[...] = x_refs[0][...] + gated
    else:
        is_prompt = pl.program_id(0) < n_prompt_tiles

        @pl.when(is_prompt)
        def _():
            o_ref[...] = x_refs[0][...] + gated

        @pl.when(jnp.logical_not(is_prompt))
        def _():
            o_ref[...] = x_refs[1][...] + gated


def _resid_proj(a, w, xs, mod, gate_idx, tm, tn, row0, n_rows):
    kdim = a.shape[1]
    r0 = row0 // tm
    if len(xs) == 1:
        x_specs = [pl.BlockSpec((tm, tn), lambda i, j: (r0 + i, j))]
    else:
        x_specs = _split_specs(tm, tn, lambda j: j)
    return pl.pallas_call(
        functools.partial(_resid_proj_kernel, gate_idx=gate_idx, n_prompt_tiles=N_PROMPT // tm),
        out_shape=jax.ShapeDtypeStruct((n_rows, D_MODEL), F32),
        grid=(n_rows // tm, D_MODEL // tn),
        in_specs=[pl.BlockSpec((tm, kdim), lambda i, j: (r0 + i, 0)),
                  pl.BlockSpec((kdim, tn), lambda i, j: (0, j))] + x_specs + [
                  pl.BlockSpec((None, 6, tn), lambda i, j: (_mod_row(r0 + i, tm), 0, j))],
        out_specs=pl.BlockSpec((tm, tn), lambda i, j: (i, j)),
        compiler_params=_cparams(("parallel", "arbitrary")),
    )(a, w, *xs, mod)


def _merge_kernel(yap_ref, yas_ref, ybp_ref, ybs_ref, wa_ref, wb_ref, gla_ref, glb_ref, o_ref, *, n_prompt_tiles):
    is_prompt = pl.program_id(1) < n_prompt_tiles
    ya = jnp.where(is_prompt, yap_ref[...], yas_ref[...])
    yb = jnp.where(is_prompt, ybp_ref[...], ybs_ref[...])
    pa = _dot(ya, wa_ref[...])
    pb = _dot(yb, wb_ref[...])
    o_ref[...] = (jax.nn.sigmoid(gla_ref[...]) * pa + jax.nn.sigmoid(glb_ref[...]) * pb).astype(BF)


def _merge(y_a, y_b, w_oa, w_ob, proj):
    tm, tn = 512, 1024
    npt = N_PROMPT // tm
    y_specs = [pl.BlockSpec((tm, D_A), lambda j, i: (jnp.minimum(i, npt - 1), 0)),
               pl.BlockSpec((tm, D_A), lambda j, i: (jnp.maximum(i - npt, 0), 0))]
    return pl.pallas_call(
        functools.partial(_merge_kernel, n_prompt_tiles=npt),
        out_shape=jax.ShapeDtypeStruct((N_TOK, D_MODEL), BF),
        grid=(D_MODEL // tn, N_TOK // tm),
        in_specs=y_specs + y_specs + [
                  pl.BlockSpec((D_A, tn), lambda j, i: (0, j)),
                  pl.BlockSpec((D_A, tn), lambda j, i: (0, j)),
                  pl.BlockSpec((tm, tn), lambda j, i: (i, C_GLA // tn + j)),
                  pl.BlockSpec((tm, tn), lambda j, i: (i, C_GLB // tn + j))],
        out_specs=pl.BlockSpec((tm, tn), lambda j, i: (i, j)),
        compiler_params=_cparams(("parallel", "arbitrary")),
    )(*y_a, *y_b, w_oa, w_ob, proj, proj)


def _rope(x, cos_ref, sin_ref):
    swapped = pltpu.roll(x, ROPE_DIM // 2, axis=1) + pltpu.roll(x, LANES - ROPE_DIM // 2, axis=1)
    return x * cos_ref[...] + swapped * sin_ref[...]


def _rope_block(i, tm):
    n_prompt_tiles = N_PROMPT // tm
    tiles_per_seq = DEC_SEQ // tm
    return jnp.where(i < n_prompt_tiles, tiles_per_seq, (i - n_prompt_tiles) % tiles_per_seq)


def _q_kernel(ql_ref, nw_ref, w_ref, hw_ref, cos_ref, sin_ref, o_ref):
    x = ql_ref[...]
    y = x * lax.rsqrt(jnp.mean(x * x, axis=-1, keepdims=True) + NORM_EPS) * nw_ref[...]
    q = _dot(y.astype(BF), w_ref[...])
    hw = hw_ref[...]
    for h in range(H_B):
        qh = q[:, h * QK_PAD:(h + 1) * QK_PAD]
        inv = lax.rsqrt(jnp.sum(qh * qh, axis=-1, keepdims=True) * (1.0 / QK_DIM) + NORM_EPS)
        qn = qh * inv * hw
        o_ref[:, h * QK_PAD:h * QK_PAD + NOPE_DIM] = qn[:, :NOPE_DIM].astype(BF)
        o_ref[:, h * QK_PAD + NOPE_DIM:(h + 1) * QK_PAD] = _rope(qn[:, NOPE_DIM:], cos_ref, sin_ref).astype(BF)


def _q_proj(proj, q_norm_w, w_uq_p, q_head_w_p, cos_t, sin_t):
    tm = QKV_TM
    return pl.pallas_call(
        _q_kernel,
        out_shape=jax.ShapeDtypeStruct((N_TOK, H_B * QK_PAD), BF),
        grid=(N_TOK // tm,),
        in_specs=[pl.BlockSpec((tm, Q_LORA), lambda i: (i, C_QLAT // Q_LORA)),
                  pl.BlockSpec((1, Q_LORA), lambda i: (0, 0)),
                  pl.BlockSpec((Q_LORA, H_B * QK_PAD), lambda i: (0, 0)),
                  pl.BlockSpec((1, QK_PAD), lambda i: (0, 0)),
                  pl.BlockSpec((tm, LANES), lambda i: (_rope_block(i, tm), 0)),
                  pl.BlockSpec((tm, LANES), lambda i: (_rope_block(i, tm), 0))],
        out_specs=pl.BlockSpec((tm, H_B * QK_PAD), lambda i: (i, 0)),
        compiler_params=_cparams(("parallel",)),
    )(proj, q_norm_w.reshape(1, Q_LORA), w_uq_p, q_head_w_p, cos_t, sin_t)


def _kv_kernel(lat_ref, kpe_ref, nw_ref, w_ref, wkn_ref, wkr_ref, cos_ref, sin_ref, *out_refs, pre_norm):
    if pre_norm:
        k_ref, v_ref, ckv_ref = out_refs
        x = lat_ref[...]
        ckv = x * lax.rsqrt(jnp.mean(x * x, axis=-1, keepdims=True) + NORM_EPS) * nw_ref[...]
        ckv_ref[...] = ckv
    else:
        k_ref, v_ref = out_refs
        ckv = lat_ref[...]
    kv = _dot(ckv.astype(BF), w_ref[...])
    v_ref[...] = kv[:, H_B * NOPE_DIM:].astype(BF)
    kpe = kpe_ref[...]
    kpe_ss = jnp.sum(kpe * kpe, axis=-1, keepdims=True)
    kpe_rot = _rope(kpe * wkr_ref[...], cos_ref, sin_ref)
    wkn = wkn_ref[...]
    for h in range(H_B):
        kn = kv[:, h * NOPE_DIM:(h + 1) * NOPE_DIM]
        inv = lax.rsqrt((jnp.sum(kn * kn, axis=-1, keepdims=True) + kpe_ss) * (1.0 / QK_DIM) + NORM_EPS)
        k_ref[:, h * QK_PAD:h * QK_PAD + NOPE_DIM] = (kn * inv * wkn).astype(BF)
        k_ref[:, h * QK_PAD + NOPE_DIM:(h + 1) * QK_PAD] = (kpe_rot * inv).astype(BF)


def _kv_proj(lat, lat_col, kpe, kpe_col, n_rows, kv_norm_w, w_ukv_p, wkn, wkr, cos_t, sin_t, rope_block, pre_norm):
    tm = QKV_TM
    out_shape = [jax.ShapeDtypeStruct((n_rows, H_B * QK_PAD), BF),
                 jax.ShapeDtypeStruct((n_rows, H_B * V_DIM), BF)]
    out_specs = [pl.BlockSpec((tm, H_B * QK_PAD), lambda i: (i, 0)),
                 pl.BlockSpec((tm, H_B * V_DIM), lambda i: (i, 0))]
    if pre_norm:
        out_shape.append(jax.ShapeDtypeStruct((n_rows, KV_LORA), F32))
        out_specs.append(pl.BlockSpec((tm, KV_LORA), lambda i: (i, 0)))
    return pl.pallas_call(
        functools.partial(_kv_kernel, pre_norm=pre_norm),
        out_shape=out_shape,
        grid=(n_rows // tm,),
        in_specs=[pl.BlockSpec((tm, KV_LORA), lambda i: (i, lat_col // KV_LORA)),
                  pl.BlockSpec((tm, LANES), lambda i: (i, kpe_col // LANES)),
                  pl.BlockSpec((1, KV_LORA), lambda i: (0, 0)),
                  pl.BlockSpec((KV_LORA, H_B * (NOPE_DIM + V_DIM)), lambda i: (0, 0)),
                  pl.BlockSpec((1, NOPE_DIM), lambda i: (0, 0)),
                  pl.BlockSpec((1, LANES), lambda i: (0, 0)),
                  pl.BlockSpec((tm, LANES), lambda i: (rope_block(i, tm), 0)),
                  pl.BlockSpec((tm, LANES), lambda i: (rope_block(i, tm), 0))],
        out_specs=out_specs,
        compiler_params=_cparams(("parallel",)),
    )(lat, kpe, kv_norm_w.reshape(1, KV_LORA), w_ukv_p, wkn, wkr, cos_t, sin_t)


_ATTN_SCALE_LOG2 = float(QK_DIM ** -0.5 * np.log2(np.e))


def _attn_prompt_kernel(q_ref, k_ref, v_ref, o_ref, *, heads):
    for h in range(heads):
        q = q_ref[:, h * QK_PAD:(h + 1) * QK_PAD]
        k = k_ref[:, h * QK_PAD:(h + 1) * QK_PAD]
        s = _dot_nt(q, k)
        e = jnp.exp2((s - jnp.max(s, axis=-1, keepdims=True)) * _ATTN_SCALE_LOG2)
        inv = 1.0 / jnp.sum(e, axis=-1, keepdims=True)
        o = _dot(e.astype(BF), v_ref[:, h * V_DIM:(h + 1) * V_DIM]) * inv
        o_ref[:, h * V_DIM:(h + 1) * V_DIM] = o.astype(BF)


def _attn_prompt(q, k, v):
    heads = 8
    return pl.pallas_call(
        functools.partial(_attn_prompt_kernel, heads=heads),
        out_shape=jax.ShapeDtypeStruct((N_PROMPT, H_B * V_DIM), BF),
        grid=(BATCH, H_B // heads),
        in_specs=[pl.BlockSpec((SEQ, heads * QK_PAD), lambda b, g: (b, g)),
                  pl.BlockSpec((SEQ, heads * QK_PAD), lambda b, g: (b, g)),
                  pl.BlockSpec((SEQ, heads * V_DIM), lambda b, g: (b, g))],
        out_specs=pl.BlockSpec((SEQ, heads * V_DIM), lambda b, g: (b, g)),
        compiler_params=_cparams(("parallel", "parallel")),
    )(q, k, v)


def _attn_sample_kernel(q_ref, kc_ref, vc_ref, ks_ref, vs_ref, o_ref, *, heads):
    for h in range(heads):
        qk = slice(h * QK_PAD, (h + 1) * QK_PAD)
        vv = slice(h * V_DIM, (h + 1) * V_DIM)
        q = q_ref[:, qk]
        s1 = _dot_nt(q, kc_ref[:, qk])
        s2 = _dot_nt(q, ks_ref[:, qk])
        m = jnp.maximum(jnp.max(s1, axis=-1, keepdims=True), jnp.max(s2, axis=-1, keepdims=True))
        e1 = jnp.exp2((s1 - m) * _ATTN_SCALE_LOG2)
        e2 = jnp.exp2((s2 - m) * _ATTN_SCALE_LOG2)
        inv = 1.0 / (jnp.sum(e1, axis=-1, keepdims=True) + jnp.sum(e2, axis=-1, keepdims=True))
        o = (_dot(e1.astype(BF), vc_ref[:, vv]) + _dot(e2.astype(BF), vs_ref[:, vv])) * inv
        o_ref[:, vv] = o.astype(BF)


def _attn_sample(q, k, v, k_ctx, v_ctx):
    tq = 512
    heads = 4
    qt = DEC_SEQ // tq
    q0 = N_PROMPT // tq
    s0 = N_PROMPT // DEC_SEQ
    return pl.pallas_call(
        functools.partial(_attn_sample_kernel, heads=heads),
        out_shape=jax.ShapeDtypeStruct((N_SAMPLE, H_B * V_DIM), BF),
        grid=(DEC_BATCH, H_B // heads, qt),
        in_specs=[pl.BlockSpec((tq, heads * QK_PAD), lambda b, g, t: (q0 + b * qt + t, g)),
                  pl.BlockSpec((PAST_LEN, heads * QK_PAD), lambda b, g, t: (b, g)),
                  pl.BlockSpec((PAST_LEN, heads * V_DIM), lambda b, g, t: (b, g)),
                  pl.BlockSpec((DEC_SEQ, heads * QK_PAD), lambda b, g, t: (s0 + b, g)),
                  pl.BlockSpec((DEC_SEQ, heads * V_DIM), lambda b, g, t: (s0 + b, g))],
        out_specs=pl.BlockSpec((tq, heads * V_DIM), lambda b, g, t: (b * qt + t, g)),
        compiler_params=_cparams(("parallel", "parallel", "arbitrary")),
    )(q, k_ctx, v_ctx, k, v)


def _head_sum(x, ones_blk):
    hi = x.astype(BF)
    lo = (x - hi.astype(F32)).astype(BF)
    return _dot(hi, ones_blk) + _dot(lo, ones_blk)


def _wkv_chunks_a(items, strict, incl, lane_lo, eye):
    C = WKV_CHUNK
    n2 = 2 * C

    def stack(x):
        return jnp.concatenate([jnp.where(lane_lo, x, 0.0), jnp.where(lane_lo, 0.0, x)], axis=0)

    row_c = lax.broadcasted_iota(jnp.int32, (C, PAIR), 0)

    def cumsum(it):
        x = it["lw"]
        s = 1
        while s < C:
            if it["d"] == 0:
                x = x + jnp.where(row_c >= s, pltpu.roll(x, s, axis=0), 0.0)
            else:
                x = x + jnp.where(row_c < C - s, pltpu.roll(x, C - s, axis=0), 0.0)
            s *= 2
        return x

    cs = [cumsum(it) for it in items]

    def operands(it, c):
        cl = c[C - 1:C, :] if it["d"] == 0 else c[0:1, :]
        e_inc = jnp.exp(c)
        e_neg = jnp.exp(-c)
        e_exc = jnp.exp(c - it["lw"])
        e_end = jnp.exp(cl - c)
        b, kd = it["b"], it["kd"]
        return dict(d=it["d"], e=jnp.exp(cl),
                    a_t=stack(-it["kk"] * e_exc).astype(BF), r_t=stack(it["r"] * e_inc),
                    b_t=stack(b * e_neg).astype(BF), k_t=stack(kd * e_neg).astype(BF),
                    bk_h=jnp.concatenate([stack(b * e_end), stack(kd * e_end)], axis=0).astype(BF),
                    v_s=stack(it["v"]).astype(BF))

    ops = [operands(it, c) for it, c in zip(items, cs)]
    scs = [_dot_nt(jnp.concatenate([o["a_t"], o["r_t"].astype(BF)], axis=0),
                   jnp.concatenate([o["b_t"], o["k_t"]], axis=0)) for o in ops]
    for o, sc in zip(ops, scs):
        d = o["d"]
        o["lp"] = jnp.where(strict[d], sc[:n2, :n2], 0.0).astype(BF)
        o["l_ak"] = jnp.where(strict[d], sc[:n2, n2:], 0.0).astype(BF)
        o["a_r"] = jnp.concatenate([jnp.where(incl[d], sc[n2:, :n2], 0.0),
                                    jnp.where(incl[d], sc[n2:, n2:], 0.0)], axis=1).astype(BF)

    ts = [eye + o["lp"].astype(F32) for o in ops]
    lps = [_dot(o["lp"], o["lp"]).astype(BF) for o in ops]
    span = 2
    while 2 * span < C:
        rs = [_dot(lp, jnp.concatenate([t.astype(BF), lp], axis=1)) for t, lp in zip(ts, lps)]
        ts = [t + r[:, :n2] for t, r in zip(ts, rs)]
        lps = [r[:, n2:].astype(BF) for r in rs]
        span *= 2
    ts = [t + _dot(lp, t.astype(BF)) for t, lp in zip(ts, lps)]

    lakvs = [_dot(o["l_ak"], o["v_s"]) for o in ops]
    wus = [_dot(t.astype(BF), jnp.concatenate([o["a_t"], lakv.astype(BF)], axis=1))
           for t, o, lakv in zip(ts, ops, lakvs)]
    zero = jnp.zeros((n2, n2), BF)
    outs = []
    for o, wu in zip(ops, wus):
        rhs = jnp.concatenate([wu.astype(BF), jnp.concatenate([zero, o["v_s"]], axis=1)], axis=0)
        qy = _dot(o["a_r"], rhs)
        mn = _dot_tn(o["bk_h"], rhs)
        q = (o["r_t"] + qy[:, :n2]).astype(BF)
        y0 = qy[:C, n2:] + qy[C:, n2:]
        qm = jnp.concatenate([q, mn[:, :n2].astype(BF)], axis=0)
        e_rows = jnp.broadcast_to(o["e"], (n2, n2)).T
        outs.append((qm, y0, mn[:, n2:], e_rows))
    return outs


def _wkv_kernel(*refs, T, n_pairs, has_s0, out_state):
    (r_ref, k_ref, v_ref, lora_ref, lg_ref, w0_ref, wl_ref, a0_ref, wa_ref, wg_ref,
     kkw_ref, ka_ref, rk_ref, lnw_ref, lnb_ref) = refs[:15]
    pos = 15
    if has_s0:
        s0_ref = refs[pos]
        pos += 1
    y_ref = refs[pos]
    pos += 1
    if out_state:
        so_ref = refs[pos]
        pos += 1
    lw_s, b_s, kd_s, kk_s, y_s, qm_s, n_s, e_s = refs[pos:]

    C = WKV_CHUNK
    nc = T // C
    lane = lax.broadcasted_iota(jnp.int32, (PAIR, PAIR), 1)
    row = lax.broadcasted_iota(jnp.int32, (PAIR, PAIR), 0)
    ones_blk = jnp.where((lane // HEAD_A) == (row // HEAD_A), 1.0, 0.0).astype(BF)
    eye = jnp.where(lane == row, 1.0, 0.0)
    lane_lo = lax.broadcasted_iota(jnp.int32, (C, PAIR), 1) < HEAD_A
    strict = (lane < row, lane > row)
    incl = (lane <= row, lane >= row)
    pairs = range(n_pairs)

    def cols(pp):
        return slice(pp * PAIR, (pp + 1) * PAIR)

    for pp in pairs:
        k = k_ref[:, cols(pp)]
        kk = k * kkw_ref[:, cols(pp)]
        kk = kk * lax.rsqrt(_head_sum(kk * kk, ones_blk) + 1e-12)
        kk_s[pp] = kk
        for d in range(2):
            lw_in = lora_ref[:, d * R_LORA:(d + 1) * R_LORA]
            la_in = lora_ref[:, (2 + d) * R_LORA:(3 + d) * R_LORA]
            x = w0_ref[d:d + 1, cols(pp)] + _dot(jnp.tanh(lw_in).astype(BF), wl_ref[d, :, cols(pp)].astype(BF))
            lw_s[2 * pp + d] = -DECAY_SCALE * jax.nn.sigmoid(x)
            a = jax.nn.sigmoid(a0_ref[d:d + 1, cols(pp)] + _dot(la_in.astype(BF), wa_ref[d, :, cols(pp)].astype(BF)))
            kd_s[2 * pp + d] = k * (1.0 + (a - 1.0) * ka_ref[:, cols(pp)])
            b_s[2 * pp + d] = kk * a

    def chunk_rows(idx):
        return pl.ds(idx * C, C) if isinstance(idx, int) else pl.ds(pl.multiple_of(idx * C, C), C)

    def phase_a(keys):
        items = []
        for pp, d, idx in keys:
            sl = chunk_rows(idx)
            items.append(dict(d=d, r=r_ref[sl, cols(pp)], v=v_ref[sl, cols(pp)], kk=kk_s[pp, sl, :],
                              lw=lw_s[2 * pp + d, sl, :], b=b_s[2 * pp + d, sl, :], kd=kd_s[2 * pp + d, sl, :]))
        outs = _wkv_chunks_a(items, strict, incl, lane_lo, eye)
        for (pp, d, idx), (qm, y0, nt, e_rows) in zip(keys, outs):
            qm_s[2 * pp + d, idx] = qm
            y_s[2 * pp + d, chunk_rows(idx), :] = y0
            n_s[2 * pp + d, idx] = nt
            e_s[2 * pp + d, idx] = e_rows

    if 2 * nc <= WKV_UNROLL:
        group = min(WKV_UNROLL // (2 * nc), n_pairs)
        for g in range(0, n_pairs, group):
            phase_a([(pp, d, t) for pp in range(g, g + group) for t in range(nc) for d in range(2)])
    else:
        per_iter = WKV_UNROLL // 2
        for pp in pairs:
            def body_a(j, carry, pp=pp):
                phase_a([(pp, d, j * per_iter + t) for t in range(per_iter) for d in range(2)])
                return carry

            lax.fori_loop(0, nc // per_iter, body_a, 0)

    chains = [(pp, d) for pp in pairs for d in range(2)]

    def body_b(i, carry):
        nxt = []
        for (pp, d), s in zip(chains, carry):
            idx = i if d == 0 else nc - 1 - i
            sl = chunk_rows(idx)
            prod = _dot(qm_s[2 * pp + d, idx], s.astype(BF))
            y_s[2 * pp + d, sl, :] += prod[:C] + prod[C:PAIR]
            nxt.append(s * e_s[2 * pp + d, idx] + prod[PAIR:] + n_s[2 * pp + d, idx])
        return tuple(nxt)

    if has_s0:
        init = tuple(s0_ref[d, pp] for pp, d in chains)
    else:
        init = tuple(jnp.zeros((PAIR, PAIR), F32) for _ in chains)
    s_fin = lax.fori_loop(0, nc, body_b, init)
    if out_state:
        for (pp, d), st in zip(chains, s_fin):
            s = st.T
            so_ref[d, 2 * pp] = s[:HEAD_A, :HEAD_A]
            so_ref[d, 2 * pp + 1] = s[HEAD_A:, HEAD_A:]

    sig_lg = jax.nn.sigmoid(lg_ref[...]).astype(BF)
    inv_n = 1.0 / HEAD_A
    for pp in pairs:
        y = y_s[2 * pp] + y_s[2 * pp + 1]
        mu = _head_sum(y, ones_blk) * inv_n
        yc = y - mu
        var = _head_sum(yc * yc, ones_blk) * inv_n
        y = yc * lax.rsqrt(var + LN_X_EPS) * lnw_ref[:, cols(pp)] + lnb_ref[:, cols(pp)]
        rr = r_ref[:, cols(pp)] * rk_ref[:, cols(pp)]
        bonus = _head_sum(rr * (kd_s[2 * pp] + kd_s[2 * pp + 1]), ones_blk) * v_ref[:, cols(pp)]
        g = _dot(sig_lg, wg_ref[:, cols(pp)].astype(BF))
        y_ref[:, cols(pp)] = ((y + bonus) * g).astype(BF)


def _wkv(proj, row0, n_seq, T, wts, s0, n_pairs):
    (w0, wl, a0, wa, wg, kkw, ka, rk, lnw, lnb) = wts
    sb = row0 // T
    nc = T // WKV_CHUNK
    gw = n_pairs * PAIR
    cb = lambda col: col // gw
    vec = pl.BlockSpec((1, gw), lambda s, p: (0, p))
    in_specs = [pl.BlockSpec((T, gw), lambda s, p: (sb + s, cb(C_R) + p)),
                pl.BlockSpec((T, gw), lambda s, p: (sb + s, cb(C_K) + p)),
                pl.BlockSpec((T, gw), lambda s, p: (sb + s, cb(C_V) + p)),
                pl.BlockSpec((T, 4 * R_LORA), lambda s, p: (sb + s, C_LORA // (4 * R_LORA))),
                pl.BlockSpec((T, R_G_PAD), lambda s, p: (sb + s, C_LG // R_G_PAD)),
                pl.BlockSpec((2, gw), lambda s, p: (0, p)),
                pl.BlockSpec((2, R_LORA, gw), lambda s, p: (0, 0, p)),
                pl.BlockSpec((2, gw), lambda s, p: (0, p)),
                pl.BlockSpec((2, R_LORA, gw), lambda s, p: (0, 0, p)),
                pl.BlockSpec((R_G_PAD, gw), lambda s, p: (0, p)),
                vec, vec, vec, vec, vec]
    args = [proj, proj, proj, proj, proj, w0, wl, a0, wa, wg, kkw, ka, rk, lnw, lnb]
    has_s0 = s0 is not None
    if has_s0:
        in_specs.append(pl.BlockSpec((None, 2, n_pairs, PAIR, PAIR), lambda s, p: (s, 0, p, 0, 0)))
        args.append(s0)
    out_shape = [jax.ShapeDtypeStruct((n_seq * T, D_A), BF)]
    out_specs = [pl.BlockSpec((T, gw), lambda s, p: (s, p))]
    out_state = not has_s0
    if out_state:
        out_shape.append(jax.ShapeDtypeStruct((n_seq, 2, H_A, HEAD_A, HEAD_A), F32))
        out_specs.append(pl.BlockSpec((None, 2, 2 * n_pairs, HEAD_A, HEAD_A), lambda s, p: (s, 0, p, 0, 0)))
    nch = 2 * n_pairs
    return pl.pallas_call(
        functools.partial(_wkv_kernel, T=T, n_pairs=n_pairs, has_s0=has_s0, out_state=out_state),
        out_shape=out_shape,
        grid=(n_seq, N_PAIR // n_pairs),
        in_specs=in_specs,
        out_specs=out_specs,
        scratch_shapes=[pltpu.VMEM((nch, T, PAIR), F32), pltpu.VMEM((nch, T, PAIR), F32),
                        pltpu.VMEM((nch, T, PAIR), F32), pltpu.VMEM((n_pairs, T, PAIR), F32),
                        pltpu.VMEM((nch, T, PAIR), F32),
                        pltpu.VMEM((nch, nc, 2 * PAIR, PAIR), BF),
                        pltpu.VMEM((nch, nc, PAIR, PAIR), F32), pltpu.VMEM((nch, nc, PAIR, PAIR), F32)],
        compiler_params=_cparams(("parallel", "arbitrary")),
    )(*args)


W_IN_BLK = 512
W_IN_SIDE = 128
W_IN_B0 = C_QLAT // W_IN_BLK
W_IN_C0 = C_GLA // W_IN_BLK
W_IN_SHIFT = R_G_PAD - R_G


def _w_in_relayout_kernel(main_ref, side_ref, o_ref):
    j = pl.program_id(0)
    row = lax.broadcasted_iota(jnp.int32, o_ref.shape, 0)
    sh = W_IN_SHIFT

    @pl.when(j < W_IN_B0)
    def _():
        valid = jnp.where(j == W_IN_B0 - 1, R_G, W_IN_BLK)
        o_ref[...] = jnp.where(row < valid, main_ref[...], 0.0).astype(BF)

    @pl.when((j >= W_IN_B0) & (j < W_IN_C0))
    def _():
        valid = jnp.where(j == W_IN_C0 - 1, ROPE_DIM, W_IN_BLK)
        x = jnp.concatenate([side_ref[W_IN_SIDE - sh:, :], main_ref[:W_IN_BLK - sh, :]], axis=0)
        o_ref[...] = jnp.where(row < valid, x, 0.0).astype(BF)

    @pl.when(j >= W_IN_C0)
    def _():
        x = jnp.concatenate([main_ref[sh:, :], side_ref[:sh, :]], axis=0)
        o_ref[...] = x.astype(BF)


def _w_in_relayout(w_in_t):
    tc = 2048
    per = W_IN_BLK // W_IN_SIDE

    def main_idx(j, c):
        return jnp.where(j >= W_IN_C0, j - 1, j), c

    def side_idx(j, c):
        return jnp.where(j >= W_IN_C0, per * j, per * jnp.maximum(j, W_IN_B0) - 1), c

    return pl.pallas_call(
        _w_in_relayout_kernel,
        out_shape=jax.ShapeDtypeStruct((D_IN_PAD, D_MODEL), BF),
        grid=(D_IN_PAD // W_IN_BLK, D_MODEL // tc),
        in_specs=[pl.BlockSpec((W_IN_BLK, tc), main_idx), pl.BlockSpec((W_IN_SIDE, tc), side_idx)],
        out_specs=pl.BlockSpec((W_IN_BLK, tc), lambda j, c: (j, c)),
        compiler_params=_cparams(("arbitrary", "arbitrary")),
    )(w_in_t, w_in_t)


def _rope_tables():
    rows = DEC_SEQ // GRID_W
    row = jnp.repeat(jnp.arange(rows), GRID_W).astype(F32)
    col = jnp.tile(jnp.arange(GRID_W), rows).astype(F32)
    n_freq = ROPE_DIM // 4
    inv = ROPE_THETA ** (-jnp.arange(n_freq, dtype=F32) / n_freq)
    ang = jnp.concatenate([row[:, None] * inv, col[:, None] * inv], axis=-1)
    cos, sin = jnp.cos(ang), jnp.sin(ang)
    pad = jnp.zeros((DEC_SEQ, LANES - ROPE_DIM), F32)
    cos_t = jnp.concatenate([cos, cos, pad], axis=-1)
    sin_t = jnp.concatenate([-sin, sin, pad], axis=-1)
    cos_t = jnp.concatenate([cos_t, jnp.ones((QKV_TM, LANES), F32)], axis=0)
    sin_t = jnp.concatenate([sin_t, jnp.zeros((QKV_TM, LANES), F32)], axis=0)
    return cos_t, sin_t


def _pair_state_in(state):
    b = state.shape[0]
    s = jnp.swapaxes(state, -1, -2).reshape(b, 2, N_PAIR, 2, HEAD_A, HEAD_A)
    z = jnp.zeros_like(s[:, :, :, 0])
    top = jnp.concatenate([s[:, :, :, 0], z], axis=-1)
    bot = jnp.concatenate([z, s[:, :, :, 1]], axis=-1)
    return jnp.concatenate([top, bot], axis=-2)


def kernel(x_prompt, x_sample, c, state_rwkv, cache_mla_ckv, cache_mla_kpe, c_ctx, w_mod, b_mod, norm_mix_w, w_in, rwkv_w0, rwkv_w_lora_b, rwkv_a0, rwkv_a_lora_b, rwkv_g_lora_b, rwkv_k_k, rwkv_k_a, rwkv_r_k, rwkv_ln_w, rwkv_ln_b, mla_q_norm_w, mla_w_uq, mla_kv_norm_w, mla_w_ukv, mla_q_head_norm, mla_k_head_norm, w_o_rwkv, w_o_mla, w_out, norm_ffn_w, w_ffn_in, w_ffn_out):
    l = 0
    xp = x_prompt.reshape(N_PROMPT, D_MODEL)
    xs = x_sample.reshape(N_SAMPLE, D_MODEL)

    w_in_t = _w_in_relayout(jnp.swapaxes(w_in[l], 0, 1))
    w_uq_p = jnp.pad(mla_w_uq[l].reshape(Q_LORA, H_B, QK_DIM), ((0, 0), (0, 0), (0, QK_PAD - QK_DIM)))
    w_uq_p = w_uq_p.reshape(Q_LORA, H_B * QK_PAD).astype(BF)
    q_head_w_p = jnp.pad(mla_q_head_norm[l], (0, QK_PAD - QK_DIM)).reshape(1, QK_PAD)
    wkv3 = mla_w_ukv[l].reshape(KV_LORA, H_B, NOPE_DIM + V_DIM)
    w_ukv_p = jnp.concatenate([wkv3[:, :, :NOPE_DIM].reshape(KV_LORA, H_B * NOPE_DIM),
                               wkv3[:, :, NOPE_DIM:].reshape(KV_LORA, H_B * V_DIM)], axis=1).astype(BF)
    wkn = mla_k_head_norm[l, :NOPE_DIM].reshape(1, NOPE_DIM)
    wkr = jnp.pad(mla_k_head_norm[l, NOPE_DIM:], (0, LANES - ROPE_DIM)).reshape(1, LANES)
    w_oa = w_o_rwkv[l].astype(BF)
    w_ob = w_o_mla[l].astype(BF)
    w_out_b = w_out[l].astype(BF)
    wg_p = jnp.pad(rwkv_g_lora_b[l], ((0, R_G_PAD - R_G), (0, 0)))
    row = lambda a: a.reshape(1, D_A)
    wkv_w = (rwkv_w0[l], rwkv_w_lora_b[l], rwkv_a0[l], rwkv_a_lora_b[l], wg_p, row(rwkv_k_k[l]), row(rwkv_k_a[l]),
             row(rwkv_r_k[l]), row(rwkv_ln_w[l]), row(rwkv_ln_b[l]))
    cos_t, sin_t = _rope_tables()

    cond8 = jnp.concatenate([c_ctx[None, :], c, jnp.zeros((8 - 1 - DEC_BATCH, D_MODEL), F32)], axis=0)
    mod = _modulation(cond8, w_mod[l], b_mod[l]).reshape(8, 6, D_MODEL)

    proj, w_ffn_in_p = _in_proj(_norm_mod((xp, xs), norm_mix_w[l], mod, 0, 1), w_in_t, w_ffn_in[l])

    ya_p, s_new = _wkv(proj, 0, BATCH, SEQ, wkv_w, None, WKV_PAIRS_PROMPT)
    ya_s = _wkv(proj, N_PROMPT, DEC_BATCH, DEC_SEQ, wkv_w, _pair_state_in(state_rwkv[:, l]), WKV_PAIRS_SAMPLE)[0]

    q = _q_proj(proj, mla_q_norm_w[l], w_uq_p, q_head_w_p, cos_t, sin_t)
    k, v, ckv = _kv_proj(proj, C_KVLAT, proj, C_KPE, N_TOK, mla_kv_norm_w[l], w_ukv_p, wkn, wkr, cos_t, sin_t,
                         _rope_block, True)
    ctx_ckv = cache_mla_ckv[:, l].reshape(DEC_BATCH * PAST_LEN, KV_LORA)
    ctx_kpe = jnp.pad(cache_mla_kpe[:, l].reshape(DEC_BATCH * PAST_LEN, ROPE_DIM), ((0, 0), (0, LANES - ROPE_DIM)))
    k_ctx, v_ctx = _kv_proj(ctx_ckv, 0, ctx_kpe, 0, DEC_BATCH * PAST_LEN, mla_kv_norm_w[l], w_ukv_p, wkn, wkr,
                            cos_t, sin_t, lambda i, tm: DEC_SEQ // tm, False)
    yb_p = _attn_prompt(q, k, v)
    yb_s = _attn_sample(q, k, v, k_ctx, v_ctx)

    mixed = _merge((ya_p, ya_s), (yb_p, yb_s), w_oa, w_ob, proj)
    x1 = _resid_proj(mixed, w_out_b, (xp, xs), mod, 2, 1024, 512, 0, N_TOK)
    act, w_ffn_out_p = _ffn_in(_norm_mod((x1,), norm_ffn_w[l], mod, 3, 4), w_ffn_in_p, w_ffn_out[l])
    y_p = _resid_proj(act, w_ffn_out_p, (x1,), mod, 5, 512, 512, 0, N_PROMPT).reshape(BATCH, SEQ, D_MODEL)
    y_s = _resid_proj(act, w_ffn_out_p, (x1,), mod, 5, 512, 512, N_PROMPT, N_SAMPLE)
    y_s = y_s.reshape(DEC_BATCH, DEC_SEQ, D_MODEL)
    new_state = s_new[:, None]
    new_ckv = ckv[:N_PROMPT].reshape(BATCH, 1, SEQ, KV_LORA)
    new_kpe = proj[:N_PROMPT, C_KPE:C_KPE + ROPE_DIM].reshape(BATCH, 1, SEQ, ROPE_DIM)
    return y_p, y_s, new_state, new_ckv, new_kpe
```

```python
import functools

import numpy as np
import jax
import jax.numpy as jnp
from jax import lax
from jax.experimental import pallas as pl
from jax.experimental.pallas import tpu as pltpu

F32 = jnp.float32
BF = jnp.bfloat16

D_MODEL = 4096
BATCH, SEQ = 32, 256
DEC_BATCH, DEC_SEQ, PAST_LEN = 4, 1024, 512
GRID_W = 64
D_A, HEAD_A = 2048, 64
H_A = D_A // HEAD_A
R_LORA, R_G = 128, 480
LN_X_EPS = 64e-5
H_B, Q_LORA, KV_LORA = 16, 1024, 512
NOPE_DIM, ROPE_DIM, V_DIM = 128, 64, 128
QK_DIM = NOPE_DIM + ROPE_DIM
ROPE_THETA = 10000.0
D_FF = 11008
NORM_EPS = 1e-6

N_PROMPT = BATCH * SEQ
N_SAMPLE = DEC_BATCH * DEC_SEQ
N_TOK = N_PROMPT + N_SAMPLE

LANES = 128
R_G_PAD = 512
QK_PAD = 256
D_FF_PAD = 11264
FF_BLK = D_FF_PAD - D_FF
C_R, C_K, C_V = 0, 2048, 4096
C_LORA, C_LG, C_QLAT, C_KVLAT, C_KPE = 6144, 6656, 7168, 8192, 8704
C_GLA, C_GLB, D_IN_PAD = 9216, 13312, 17408
QKV_TM = 256
WKV_CHUNK = 64
WKV_UNROLL = 16
WKV_PAIRS_PROMPT = 8
WKV_PAIRS_SAMPLE = 2
DECAY_SCALE = float(np.exp(-0.5))
PAIR = 2 * HEAD_A
N_PAIR = D_A // PAIR

V7X_VMEM_BYTES = 64 * 2**20
VMEM_LIMIT = V7X_VMEM_BYTES - 8 * 2**20


def _cparams(sem):
    return pltpu.CompilerParams(dimension_semantics=sem, vmem_limit_bytes=VMEM_LIMIT)


def _dot(a, b):
    return jnp.dot(a, b, preferred_element_type=F32)


def _dot_nt(a, b):
    return lax.dot_general(a, b, (((1,), (1,)), ((), ())), preferred_element_type=F32)


def _dot_tn(a, b):
    return lax.dot_general(a, b, (((0,), (0,)), ((), ())), preferred_element_type=F32)


def _mod_row(i, tm):
    n_prompt_tiles = N_PROMPT // tm
    tiles_per_seq = DEC_SEQ // tm
    return jnp.where(i < n_prompt_tiles, 0, 1 + (i - n_prompt_tiles) // tiles_per_seq)


def _mod_kernel(c_ref, w_ref, b_ref, o_ref):
    c = c_ref[...]
    s = c * jax.nn.sigmoid(c)
    o_ref[...] = _dot(s.astype(BF), w_ref[...].astype(BF)) + b_ref[...]


def _modulation(cond8, w_mod, b_mod):
    tn = 1024
    n = w_mod.shape[1]
    return pl.pallas_call(
        _mod_kernel,
        out_shape=jax.ShapeDtypeStruct((8, n), F32),
        grid=(n // tn,),
        in_specs=[pl.BlockSpec((8, D_MODEL), lambda j: (0, 0)),
                  pl.BlockSpec((D_MODEL, tn), lambda j: (0, j)),
                  pl.BlockSpec((1, tn), lambda j: (0, j))],
        out_specs=pl.BlockSpec((8, tn), lambda j: (0, j)),
        compiler_params=_cparams(("arbitrary",)),
    )(cond8, w_mod, b_mod.reshape(1, n))


NORM_ROWS = 16


NORM_COLS = 512


def _normmod(x_ref, nw_ref, mod_ref, o_ref, gs_ref, shift_idx, scale_idx):
    gs_ref[0] = jnp.broadcast_to(nw_ref[...] * (1.0 + mod_ref[scale_idx:scale_idx + 1, :]), (8, D_MODEL))
    gs_ref[1] = jnp.broadcast_to(mod_ref[shift_idx:shift_idx + 1, :], (8, D_MODEL))
    reps = NORM_ROWS // 8

    def body(r, carry):
        rows = pl.ds(pl.multiple_of(r * NORM_ROWS, NORM_ROWS), NORM_ROWS)
        accs = [jnp.zeros((NORM_ROWS, LANES), F32) for _ in range(4)]
        for c in range(D_MODEL // LANES):
            xc = x_ref[rows, c * LANES:(c + 1) * LANES]
            accs[c % 4] = accs[c % 4] + xc * xc
        acc = (accs[0] + accs[1]) + (accs[2] + accs[3])
        inv = lax.rsqrt(jnp.sum(acc, axis=-1, keepdims=True) * (1.0 / D_MODEL) + NORM_EPS)
        for c in range(D_MODEL // NORM_COLS):
            cs = slice(c * NORM_COLS, (c + 1) * NORM_COLS)
            gain = jnp.concatenate([gs_ref[0, :, cs]] * reps, axis=0)
            shift = jnp.concatenate([gs_ref[1, :, cs]] * reps, axis=0)
            o_ref[rows, cs] = (x_ref[rows, cs] * inv * gain + shift).astype(BF)
        return carry

    lax.fori_loop(0, x_ref.shape[0] // NORM_ROWS, body, 0, unroll=4)


def _split_specs(tm, width, col):
    npt = N_PROMPT // tm
    return [pl.BlockSpec((tm, width), lambda i, *_: (jnp.minimum(i, npt - 1), col(*_))),
            pl.BlockSpec((tm, width), lambda i, *_: (jnp.maximum(i - npt, 0), col(*_)))]


def _norm_mod_kernel(*refs, shift_idx, scale_idx, n_prompt_tiles):
    *x_refs, nw_ref, mod_ref, o_ref, gs_ref = refs

    def run(x_ref):
        _normmod(x_ref, nw_ref, mod_ref, o_ref, gs_ref, shift_idx, scale_idx)

    if len(x_refs) == 1:
        run(x_refs[0])
    else:
        is_prompt = pl.program_id(0) < n_prompt_tiles
        pl.when(is_prompt)(lambda: run(x_refs[0]))
        pl.when(jnp.logical_not(is_prompt))(lambda: run(x_refs[1]))


def _norm_mod(xs, norm_w, mod, shift_idx, scale_idx):
    tm = 256
    if len(xs) == 1:
        x_specs = [pl.BlockSpec((tm, D_MODEL), lambda i: (i, 0))]
    else:
        x_specs = _split_specs(tm, D_MODEL, lambda: 0)
    return pl.pallas_call(
        functools.partial(_norm_mod_kernel, shift_idx=shift_idx, scale_idx=scale_idx, n_prompt_tiles=N_PROMPT // tm),
        out_shape=jax.ShapeDtypeStruct((N_TOK, D_MODEL), BF),
        grid=(N_TOK // tm,),
        in_specs=x_specs + [pl.BlockSpec((1, D_MODEL), lambda i: (0, 0)),
                            pl.BlockSpec((None, 6, D_MODEL), lambda i: (_mod_row(i, tm), 0, 0))],
        out_specs=pl.BlockSpec((tm, D_MODEL), lambda i: (i, 0)),
        scratch_shapes=[pltpu.VMEM((2, 8, D_MODEL), F32)],
        compiler_params=_cparams(("parallel",)),
    )(*xs, norm_w.reshape(1, D_MODEL), mod)


CAST_SPLIT = 2


class _CastJob:
    def __init__(self, w, axis, blk, n_real, n_seg, ni, nj):
        self.w, self.axis, self.blk, self.n_real, self.nj = w, axis, blk, n_real, nj
        self.n_parts = n_seg * (n_real + 1) * CAST_SPLIT
        assert self.n_parts <= ni * nj, "the host grid has too few steps to cast every block"
        other = w.shape[1 - axis]
        part = other // CAST_SPLIT
        self.block = (blk, part) if axis == 0 else (part, blk)
        padded = n_seg * (n_real + 1) * blk
        self.out_shape = jax.ShapeDtypeStruct((padded, other) if axis == 0 else (other, padded), BF)

    def _place(self, b, part):
        return (b, part) if self.axis == 0 else (part, b)

    def _block_of(self, i, j):
        t = jnp.minimum(i * self.nj + j, self.n_parts - 1)
        return t // CAST_SPLIT, t % CAST_SPLIT

    def in_spec(self):
        def idx(i, j):
            b, part = self._block_of(i, j)
            seg, jj = b // (self.n_real + 1), b % (self.n_real + 1)
            return self._place(seg * self.n_real + jnp.minimum(jj, self.n_real - 1), part)
        return pl.BlockSpec(self.block, idx)

    def out_spec(self):
        return pl.BlockSpec(self.block, lambda i, j: self._place(*self._block_of(i, j)))

    def step(self, x_ref, o_ref):
        t = pl.program_id(0) * self.nj + pl.program_id(1)
        is_pad = (t // CAST_SPLIT) % (self.n_real + 1) == self.n_real
        active = t < self.n_parts

        @pl.when(active & is_pad)
        def _():
            o_ref[...] = jnp.zeros_like(o_ref)

        @pl.when(active & jnp.logical_not(is_pad))
        def _():
            o_ref[...] = x_ref[...].astype(BF)


def _in_proj_kernel(h_ref, wt_ref, wc_ref, o_ref, oc_ref, *, job):
    o_ref[...] = _dot_nt(h_ref[...], wt_ref[...])
    job.step(wc_ref, oc_ref)


def _in_proj(h, w_in_t, w_ffn_in):
    tm, tn = 1024, 1024
    nj = D_IN_PAD // tn
    job = _CastJob(w_ffn_in, 1, FF_BLK, D_FF // FF_BLK, 2, N_TOK // tm, nj)
    return pl.pallas_call(
        functools.partial(_in_proj_kernel, job=job),
        out_shape=[jax.ShapeDtypeStruct((N_TOK, D_IN_PAD), F32), job.out_shape],
        grid=(N_TOK // tm, nj),
        in_specs=[pl.BlockSpec((tm, D_MODEL), lambda i, j: (i, 0)),
                  pl.BlockSpec((tn, D_MODEL), lambda i, j: (j, 0)),
                  job.in_spec()],
        out_specs=[pl.BlockSpec((tm, tn), lambda i, j: (i, j)), job.out_spec()],
        compiler_params=_cparams(("arbitrary", "arbitrary")),
    )(h, w_in_t, w_ffn_in)


def _ffn_in_kernel(h_ref, wg_ref, wu_ref, wc_ref, o_ref, oc_ref, *, job):
    h = h_ref[...]
    g = _dot(h, wg_ref[...])
    u = _dot(h, wu_ref[...])
    o_ref[...] = (g * jax.nn.sigmoid(g) * u).astype(BF)
    job.step(wc_ref, oc_ref)


def _ffn_in(h, w_ffn_in_p, w_ffn_out):
    tm, tn = 1024, 512
    nj = D_FF_PAD // tn
    job = _CastJob(w_ffn_out, 0, FF_BLK, D_FF // FF_BLK, 1, N_TOK // tm, nj)
    return pl.pallas_call(
        functools.partial(_ffn_in_kernel, job=job),
        out_shape=[jax.ShapeDtypeStruct((N_TOK, D_FF_PAD), BF), job.out_shape],
        grid=(N_TOK // tm, nj),
        in_specs=[pl.BlockSpec((tm, D_MODEL), lambda i, j: (i, 0)),
                  pl.BlockSpec((D_MODEL, tn), lambda i, j: (0, j)),
                  pl.BlockSpec((D_MODEL, tn), lambda i, j: (0, j + nj)),
                  job.in_spec()],
        out_specs=[pl.BlockSpec((tm, tn), lambda i, j: (i, j)), job.out_spec()],
        compiler_params=_cparams(("arbitrary", "arbitrary")),
    )(h, w_ffn_in_p, w_ffn_in_p, w_ffn_out)


def _resid_proj_kernel(a_ref, w_ref, *refs, gate_idx, n_prompt_tiles):
    *x_refs, mod_ref, o_ref = refs
    gated = mod_ref[gate_idx:gate_idx + 1, :] * _dot(a_ref[...], w_ref[...])
    if len(x_refs) == 1:
        o_ref[...] = x_refs[0][...] + gated
    else:
        is_prompt = pl.program_id(0) < n_prompt_tiles

        @pl.when(is_prompt)
        def _():
            o_ref[...] = x_refs[0][...] + gated

        @pl.when(jnp.logical_not(is_prompt))
        def _():
            o_ref[...] = x_refs[1][...] + gated


def _resid_proj(a, w, xs, mod, gate_idx, tm, tn, row0, n_rows):
    kdim = a.shape[1]
    r0 = row0 // tm
    if len(xs) == 1:
        x_specs = [pl.BlockSpec((tm, tn), lambda i, j: (r0 + i, j))]
    else:
        x_specs = _split_specs(tm, tn, lambda j: j)
    return pl.pallas_call(
        functools.partial(_resid_proj_kernel, gate_idx=gate_idx, n_prompt_tiles=N_PROMPT // tm),
        out_shape=jax.ShapeDtypeStruct((n_rows, D_MODEL), F32),
        grid=(n_rows // tm, D_MODEL // tn),
        in_specs=[pl.BlockSpec((tm, kdim), lambda i, j: (r0 + i, 0)),
                  pl.BlockSpec((kdim, tn), lambda i, j: (0, j))] + x_specs + [
                  pl.BlockSpec((None, 6, tn), lambda i, j: (_mod_row(r0 + i, tm), 0, j))],
        out_specs=pl.BlockSpec((tm, tn), lambda i, j: (i, j)),
        compiler_params=_cparams(("parallel", "arbitrary")),
    )(a, w, *xs, mod)


def _merge_kernel(yap_ref, yas_ref, ybp_ref, ybs_ref, wa_ref, wb_ref, gla_ref, glb_ref, o_ref, *, n_prompt_tiles):
    is_prompt = pl.program_id(1) < n_prompt_tiles
    ya = jnp.where(is_prompt, yap_ref[...], yas_ref[...])
    yb = jnp.where(is_prompt, ybp_ref[...], ybs_ref[...])
    pa = _dot(ya, wa_ref[...])
    pb = _dot(yb, wb_ref[...])
    o_ref[...] = (jax.nn.sigmoid(gla_ref[...]) * pa + jax.nn.sigmoid(glb_ref[...]) * pb).astype(BF)


def _merge(y_a, y_b, w_oa, w_ob, proj):
    tm, tn = 512, 1024
    npt = N_PROMPT // tm
    y_specs = [pl.BlockSpec((tm, D_A), lambda j, i: (jnp.minimum(i, npt - 1), 0)),
               pl.BlockSpec((tm, D_A), lambda j, i: (jnp.maximum(i - npt, 0), 0))]
    return pl.pallas_call(
        functools.partial(_merge_kernel, n_prompt_tiles=npt),
        out_shape=jax.ShapeDtypeStruct((N_TOK, D_MODEL), BF),
        grid=(D_MODEL // tn, N_TOK // tm),
        in_specs=y_specs + y_specs + [
                  pl.BlockSpec((D_A, tn), lambda j, i: (0, j)),
                  pl.BlockSpec((D_A, tn), lambda j, i: (0, j)),
                  pl.BlockSpec((tm, tn), lambda j, i: (i, C_GLA // tn + j)),
                  pl.BlockSpec((tm, tn), lambda j, i: (i, C_GLB // tn + j))],
        out_specs=pl.BlockSpec((tm, tn), lambda j, i: (i, j)),
        compiler_params=_cparams(("parallel", "arbitrary")),
    )(*y_a, *y_b, w_oa, w_ob, proj, proj)


def _rope(x, cos_ref, sin_ref):
    swapped = pltpu.roll(x, ROPE_DIM // 2, axis=1) + pltpu.roll(x, LANES - ROPE_DIM // 2, axis=1)
    return x * cos_ref[...] + swapped * sin_ref[...]


def _rope_block(i, tm):
    n_prompt_tiles = N_PROMPT // tm
    tiles_per_seq = DEC_SEQ // tm
    return jnp.where(i < n_prompt_tiles, tiles_per_seq, (i - n_prompt_tiles) % tiles_per_seq)


def _q_kernel(ql_ref, nw_ref, w_ref, hw_ref, cos_ref, sin_ref, o_ref):
    x = ql_ref[...]
    y = x * lax.rsqrt(jnp.mean(x * x, axis=-1, keepdims=True) + NORM_EPS) * nw_ref[...]
    q = _dot(y.astype(BF), w_ref[...])
    hw = hw_ref[...]
    for h in range(H_B):
        qh = q[:, h * QK_PAD:(h + 1) * QK_PAD]
        inv = lax.rsqrt(jnp.sum(qh * qh, axis=-1, keepdims=True) * (1.0 / QK_DIM) + NORM_EPS)
        qn = qh * inv * hw
        o_ref[:, h * QK_PAD:h * QK_PAD + NOPE_DIM] = qn[:, :NOPE_DIM].astype(BF)
        o_ref[:, h * QK_PAD + NOPE_DIM:(h + 1) * QK_PAD] = _rope(qn[:, NOPE_DIM:], cos_ref, sin_ref).astype(BF)


def _q_proj(proj, q_norm_w, w_uq_p, q_head_w_p, cos_t, sin_t):
    tm = QKV_TM
    return pl.pallas_call(
        _q_kernel,
        out_shape=jax.ShapeDtypeStruct((N_TOK, H_B * QK_PAD), BF),
        grid=(N_TOK // tm,),
        in_specs=[pl.BlockSpec((tm, Q_LORA), lambda i: (i, C_QLAT // Q_LORA)),
                  pl.BlockSpec((1, Q_LORA), lambda i: (0, 0)),
                  pl.BlockSpec((Q_LORA, H_B * QK_PAD), lambda i: (0, 0)),
                  pl.BlockSpec((1, QK_PAD), lambda i: (0, 0)),
                  pl.BlockSpec((tm, LANES), lambda i: (_rope_block(i, tm), 0)),
                  pl.BlockSpec((tm, LANES), lambda i: (_rope_block(i, tm), 0))],
        out_specs=pl.BlockSpec((tm, H_B * QK_PAD), lambda i: (i, 0)),
        compiler_params=_cparams(("parallel",)),
    )(proj, q_norm_w.reshape(1, Q_LORA), w_uq_p, q_head_w_p, cos_t, sin_t)


def _kv_kernel(lat_ref, kpe_ref, nw_ref, w_ref, wkn_ref, wkr_ref, cos_ref, sin_ref, *out_refs, pre_norm):
    if pre_norm:
        k_ref, v_ref, ckv_ref = out_refs
        x = lat_ref[...]
        ckv = x * lax.rsqrt(jnp.mean(x * x, axis=-1, keepdims=True) + NORM_EPS) * nw_ref[...]
        ckv_ref[...] = ckv
    else:
        k_ref, v_ref = out_refs
        ckv = lat_ref[...]
    kv = _dot(ckv.astype(BF), w_ref[...])
    v_ref[...] = kv[:, H_B * NOPE_DIM:].astype(BF)
    kpe = kpe_ref[...]
    kpe_ss = jnp.sum(kpe * kpe, axis=-1, keepdims=True)
    kpe_rot = _rope(kpe * wkr_ref[...], cos_ref, sin_ref)
    wkn = wkn_ref[...]
    for h in range(H_B):
        kn = kv[:, h * NOPE_DIM:(h + 1) * NOPE_DIM]
        inv = lax.rsqrt((jnp.sum(kn * kn, axis=-1, keepdims=True) + kpe_ss) * (1.0 / QK_DIM) + NORM_EPS)
        k_ref[:, h * QK_PAD:h * QK_PAD + NOPE_DIM] = (kn * inv * wkn).astype(BF)
        k_ref[:, h * QK_PAD + NOPE_DIM:(h + 1) * QK_PAD] = (kpe_rot * inv).astype(BF)


def _kv_proj(lat, lat_col, kpe, kpe_col, n_rows, kv_norm_w, w_ukv_p, wkn, wkr, cos_t, sin_t, rope_block, pre_norm):
    tm = QKV_TM
    out_shape = [jax.ShapeDtypeStruct((n_rows, H_B * QK_PAD), BF),
                 jax.ShapeDtypeStruct((n_rows, H_B * V_DIM), BF)]
    out_specs = [pl.BlockSpec((tm, H_B * QK_PAD), lambda i: (i, 0)),
                 pl.BlockSpec((tm, H_B * V_DIM), lambda i: (i, 0))]
    if pre_norm:
        out_shape.append(jax.ShapeDtypeStruct((n_rows, KV_LORA), F32))
        out_specs.append(pl.BlockSpec((tm, KV_LORA), lambda i: (i, 0)))
    return pl.pallas_call(
        functools.partial(_kv_kernel, pre_norm=pre_norm),
        out_shape=out_shape,
        grid=(n_rows // tm,),
        in_specs=[pl.BlockSpec((tm, KV_LORA), lambda i: (i, lat_col // KV_LORA)),
                  pl.BlockSpec((tm, LANES), lambda i: (i, kpe_col // LANES)),
                  pl.BlockSpec((1, KV_LORA), lambda i: (0, 0)),
                  pl.BlockSpec((KV_LORA, H_B * (NOPE_DIM + V_DIM)), lambda i: (0, 0)),
                  pl.BlockSpec((1, NOPE_DIM), lambda i: (0, 0)),
                  pl.BlockSpec((1, LANES), lambda i: (0, 0)),
                  pl.BlockSpec((tm, LANES), lambda i: (rope_block(i, tm), 0)),
                  pl.BlockSpec((tm, LANES), lambda i: (rope_block(i, tm), 0))],
        out_specs=out_specs,
        compiler_params=_cparams(("parallel",)),
    )(lat, kpe, kv_norm_w.reshape(1, KV_LORA), w_ukv_p, wkn, wkr, cos_t, sin_t)


_ATTN_SCALE_LOG2 = float(QK_DIM ** -0.5 * np.log2(np.e))


def _attn_prompt_kernel(q_ref, k_ref, v_ref, o_ref, *, heads):
    for h in range(heads):
        q = q_ref[:, h * QK_PAD:(h + 1) * QK_PAD]
        k = k_ref[:, h * QK_PAD:(h + 1) * QK_PAD]
        s = _dot_nt(q, k)
        e = jnp.exp2((s - jnp.max(s, axis=-1, keepdims=True)) * _ATTN_SCALE_LOG2)
        inv = 1.0 / jnp.sum(e, axis=-1, keepdims=True)
        o = _dot(e.astype(BF), v_ref[:, h * V_DIM:(h + 1) * V_DIM]) * inv
        o_ref[:, h * V_DIM:(h + 1) * V_DIM] = o.astype(BF)


def _attn_prompt(q, k, v):
    heads = 8
    return pl.pallas_call(
        functools.partial(_attn_prompt_kernel, heads=heads),
        out_shape=jax.ShapeDtypeStruct((N_PROMPT, H_B * V_DIM), BF),
        grid=(BATCH, H_B // heads),
        in_specs=[pl.BlockSpec((SEQ, heads * QK_PAD), lambda b, g: (b, g)),
                  pl.BlockSpec((SEQ, heads * QK_PAD), lambda b, g: (b, g)),
                  pl.BlockSpec((SEQ, heads * V_DIM), lambda b, g: (b, g))],
        out_specs=pl.BlockSpec((SEQ, heads * V_DIM), lambda b, g: (b, g)),
        compiler_params=_cparams(("parallel", "parallel")),
    )(q, k, v)


def _attn_sample_kernel(q_ref, kc_ref, vc_ref, ks_ref, vs_ref, o_ref, *, heads):
    for h in range(heads):
        qk = slice(h * QK_PAD, (h + 1) * QK_PAD)
        vv = slice(h * V_DIM, (h + 1) * V_DIM)
        q = q_ref[:, qk]
        s1 = _dot_nt(q, kc_ref[:, qk])
        s2 = _dot_nt(q, ks_ref[:, qk])
        m = jnp.maximum(jnp.max(s1, axis=-1, keepdims=True), jnp.max(s2, axis=-1, keepdims=True))
        e1 = jnp.exp2((s1 - m) * _ATTN_SCALE_LOG2)
        e2 = jnp.exp2((s2 - m) * _ATTN_SCALE_LOG2)
        inv = 1.0 / (jnp.sum(e1, axis=-1, keepdims=True) + jnp.sum(e2, axis=-1, keepdims=True))
        o = (_dot(e1.astype(BF), vc_ref[:, vv]) + _dot(e2.astype(BF), vs_ref[:, vv])) * inv
        o_ref[:, vv] = o.astype(BF)


def _attn_sample(q, k, v, k_ctx, v_ctx):
    tq = 512
    heads = 4
    qt = DEC_SEQ // tq
    q0 = N_PROMPT // tq
    s0 = N_PROMPT // DEC_SEQ
    return pl.pallas_call(
        functools.partial(_attn_sample_kernel, heads=heads),
        out_shape=jax.ShapeDtypeStruct((N_SAMPLE, H_B * V_DIM), BF),
        grid=(DEC_BATCH, H_B // heads, qt),
        in_specs=[pl.BlockSpec((tq, heads * QK_PAD), lambda b, g, t: (q0 + b * qt + t, g)),
                  pl.BlockSpec((PAST_LEN, heads * QK_PAD), lambda b, g, t: (b, g)),
                  pl.BlockSpec((PAST_LEN, heads * V_DIM), lambda b, g, t: (b, g)),
                  pl.BlockSpec((DEC_SEQ, heads * QK_PAD), lambda b, g, t: (s0 + b, g)),
                  pl.BlockSpec((DEC_SEQ, heads * V_DIM), lambda b, g, t: (s0 + b, g))],
        out_specs=pl.BlockSpec((tq, heads * V_DIM), lambda b, g, t: (b * qt + t, g)),
        compiler_params=_cparams(("parallel", "parallel", "arbitrary")),
    )(q, k_ctx, v_ctx, k, v)


def _head_sum(x, ones_blk):
    hi = x.astype(BF)
    lo = (x - hi.astype(F32)).astype(BF)
    return _dot(hi, ones_blk) + _dot(lo, ones_blk)


def _wkv_chunks_a(items, strict, incl, lane_lo, eye):
    C = WKV_CHUNK
    n2 = 2 * C

    def stack(x):
        return jnp.concatenate([jnp.where(lane_lo, x, 0.0), jnp.where(lane_lo, 0.0, x)], axis=0)

    row_c = lax.broadcasted_iota(jnp.int32, (C, PAIR), 0)

    def cumsum(it):
        x = it["lw"]
        s = 1
        while s < C:
            if it["d"] == 0:
                x = x + jnp.where(row_c >= s, pltpu.roll(x, s, axis=0), 0.0)
            else:
                x = x + jnp.where(row_c < C - s, pltpu.roll(x, C - s, axis=0), 0.0)
            s *= 2
        return x

    cs = [cumsum(it) for it in items]

    def operands(it, c):
        cl = c[C - 1:C, :] if it["d"] == 0 else c[0:1, :]
        e_inc = jnp.exp(c)
        e_neg = jnp.exp(-c)
        e_exc = jnp.exp(c - it["lw"])
        e_end = jnp.exp(cl - c)
        b, kd = it["b"], it["kd"]
        return dict(d=it["d"], e=jnp.exp(cl),
                    a_t=stack(-it["kk"] * e_exc).astype(BF), r_t=stack(it["r"] * e_inc),
                    b_t=stack(b * e_neg).astype(BF), k_t=stack(kd * e_neg).astype(BF),
                    bk_h=jnp.concatenate([stack(b * e_end), stack(kd * e_end)], axis=0).astype(BF),
                    v_s=stack(it["v"]).astype(BF))

    ops = [operands(it, c) for it, c in zip(items, cs)]
    scs = [_dot_nt(jnp.concatenate([o["a_t"], o["r_t"].astype(BF)], axis=0),
                   jnp.concatenate([o["b_t"], o["k_t"]], axis=0)) for o in ops]
    for o, sc in zip(ops, scs):
        d = o["d"]
        o["lp"] = jnp.where(strict[d], sc[:n2, :n2], 0.0).astype(BF)
        o["l_ak"] = jnp.where(strict[d], sc[:n2, n2:], 0.0).astype(BF)
        o["a_r"] = jnp.concatenate([jnp.where(incl[d], sc[n2:, :n2], 0.0),
                                    jnp.where(incl[d], sc[n2:, n2:], 0.0)], axis=1).astype(BF)

    ts = [eye + o["lp"].astype(F32) for o in ops]
    lps = [_dot(o["lp"], o["lp"]).astype(BF) for o in ops]
    span = 2
    while 2 * span < C:
        rs = [_dot(lp, jnp.concatenate([t.astype(BF), lp], axis=1)) for t, lp in zip(ts, lps)]
        ts = [t + r[:, :n2] for t, r in zip(ts, rs)]
        lps = [r[:, n2:].astype(BF) for r in rs]
        span *= 2
    ts = [t + _dot(lp, t.astype(BF)) for t, lp in zip(ts, lps)]

    lakvs = [_dot(o["l_ak"], o["v_s"]) for o in ops]
    wus = [_dot(t.astype(BF), jnp.concatenate([o["a_t"], lakv.astype(BF)], axis=1))
           for t, o, lakv in zip(ts, ops, lakvs)]
    zero = jnp.zeros((n2, n2), BF)
    outs = []
    for o, wu in zip(ops, wus):
        rhs = jnp.concatenate([wu.astype(BF), jnp.concatenate([zero, o["v_s"]], axis=1)], axis=0)
        qy = _dot(o["a_r"], rhs)
        mn = _dot_tn(o["bk_h"], rhs)
        q = (o["r_t"] + qy[:, :n2]).astype(BF)
        y0 = qy[:C, n2:] + qy[C:, n2:]
        qm = jnp.concatenate([q, mn[:, :n2].astype(BF)], axis=0)
        e_rows = jnp.broadcast_to(o["e"], (n2, n2)).T
        outs.append((qm, y0, mn[:, n2:], e_rows))
    return outs


def _wkv_kernel(*refs, T, n_pairs, has_s0, out_state):
    (r_ref, k_ref, v_ref, lora_ref, lg_ref, w0_ref, wl_ref, a0_ref, wa_ref, wg_ref,
     kkw_ref, ka_ref, rk_ref, lnw_ref, lnb_ref) = refs[:15]
    pos = 15
    if has_s0:
        s0_ref = refs[pos]
        pos += 1
    y_ref = refs[pos]
    pos += 1
    if out_state:
        so_ref = refs[pos]
        pos += 1
    lw_s, b_s, kd_s, kk_s, y_s, qm_s, n_s, e_s, sig_s = refs[pos:]

    C = WKV_CHUNK
    nc = T // C
    lane = lax.broadcasted_iota(jnp.int32, (PAIR, PAIR), 1)
    row = lax.broadcasted_iota(jnp.int32, (PAIR, PAIR), 0)
    ones_blk = jnp.where((lane // HEAD_A) == (row // HEAD_A), 1.0, 0.0).astype(BF)
    eye = jnp.where(lane == row, 1.0, 0.0)
    lane_lo = lax.broadcasted_iota(jnp.int32, (C, PAIR), 1) < HEAD_A
    strict = (lane < row, lane > row)
    incl = (lane <= row, lane >= row)
    pairs = range(n_pairs)

    def cols(pp):
        return slice(pp * PAIR, (pp + 1) * PAIR)

    for pp in pairs:
        k = k_ref[:, cols(pp)]
        kk = k * kkw_ref[:, cols(pp)]
        kk = kk * lax.rsqrt(_head_sum(kk * kk, ones_blk) + 1e-12)
        kk_s[pp] = kk
        for d in range(2):
            lw_in = lora_ref[:, d * R_LORA:(d + 1) * R_LORA]
            la_in = lora_ref[:, (2 + d) * R_LORA:(3 + d) * R_LORA]
            x = w0_ref[d:d + 1, cols(pp)] + _dot(jnp.tanh(lw_in).astype(BF), wl_ref[d, :, cols(pp)].astype(BF))
            lw_s[2 * pp + d] = -DECAY_SCALE * jax.nn.sigmoid(x)
            a = jax.nn.sigmoid(a0_ref[d:d + 1, cols(pp)] + _dot(la_in.astype(BF), wa_ref[d, :, cols(pp)].astype(BF)))
            kd_s[2 * pp + d] = k * (1.0 + (a - 1.0) * ka_ref[:, cols(pp)])
            b_s[2 * pp + d] = kk * a

    def chunk_rows(idx):
        return pl.ds(idx * C, C) if isinstance(idx, int) else pl.ds(pl.multiple_of(idx * C, C), C)

    def phase_a(keys):
        items = []
        for pp, d, idx in keys:
            sl = chunk_rows(idx)
            items.append(dict(d=d, r=r_ref[sl, cols(pp)], v=v_ref[sl, cols(pp)], kk=kk_s[pp, sl, :],
                              lw=lw_s[2 * pp + d, sl, :], b=b_s[2 * pp + d, sl, :], kd=kd_s[2 * pp + d, sl, :]))
        outs = _wkv_chunks_a(items, strict, incl, lane_lo, eye)
        for (pp, d, idx), (qm, y0, nt, e_rows) in zip(keys, outs):
            qm_s[2 * pp + d, idx] = qm
            y_s[2 * pp + d, chunk_rows(idx), :] = y0
            n_s[2 * pp + d, idx] = nt
            e_s[2 * pp + d, idx] = e_rows

    if 2 * nc <= WKV_UNROLL:
        group = min(WKV_UNROLL // (2 * nc), n_pairs)
        for g in range(0, n_pairs, group):
            phase_a([(pp, d, t) for pp in range(g, g + group) for t in range(nc) for d in range(2)])
    else:
        per_iter = WKV_UNROLL // 2
        for pp in pairs:
            def body_a(j, carry, pp=pp):
                phase_a([(pp, d, j * per_iter + t) for t in range(per_iter) for d in range(2)])
                return carry

            lax.fori_loop(0, nc // per_iter, body_a, 0)

    chains = [(pp, d) for pp in pairs for d in range(2)]

    def body_b(i, carry):
        nxt = []
        for (pp, d), s in zip(chains, carry):
            idx = i if d == 0 else nc - 1 - i
            sl = chunk_rows(idx)
            prod = _dot(qm_s[2 * pp + d, idx], s.astype(BF))
            y_s[2 * pp + d, sl, :] += prod[:C] + prod[C:PAIR]
            nxt.append(s * e_s[2 * pp + d, idx] + prod[PAIR:] + n_s[2 * pp + d, idx])
        return tuple(nxt)

    if has_s0:
        init = tuple(s0_ref[d, pp] for pp, d in chains)
    else:
        init = tuple(jnp.zeros((PAIR, PAIR), F32) for _ in chains)
    s_fin = lax.fori_loop(0, nc, body_b, init)
    if out_state:
        for (pp, d), st in zip(chains, s_fin):
            s = st.T
            so_ref[d, 2 * pp] = s[:HEAD_A, :HEAD_A]
            so_ref[d, 2 * pp + 1] = s[HEAD_A:, HEAD_A:]

    @pl.when(pl.program_id(1) == 0)
    def _():
        sig_s[...] = jax.nn.sigmoid(lg_ref[...]).astype(BF)

    sig_lg = sig_s[...]
    inv_n = 1.0 / HEAD_A
    for pp in pairs:
        y = y_s[2 * pp] + y_s[2 * pp + 1]
        mu = _head_sum(y, ones_blk) * inv_n
        yc = y - mu
        var = _head_sum(yc * yc, ones_blk) * inv_n
        y = yc * lax.rsqrt(var + LN_X_EPS) * lnw_ref[:, cols(pp)] + lnb_ref[:, cols(pp)]
        rr = r_ref[:, cols(pp)] * rk_ref[:, cols(pp)]
        bonus = _head_sum(rr * (kd_s[2 * pp] + kd_s[2 * pp + 1]), ones_blk) * v_ref[:, cols(pp)]
        g = _dot(sig_lg, wg_ref[:, cols(pp)].astype(BF))
        y_ref[:, cols(pp)] = ((y + bonus) * g).astype(BF)


def _wkv(proj, row0, n_seq, T, wts, s0, n_pairs):
    (w0, wl, a0, wa, wg, kkw, ka, rk, lnw, lnb) = wts
    sb = row0 // T
    nc = T // WKV_CHUNK
    gw = n_pairs * PAIR
    cb = lambda col: col // gw
    vec = pl.BlockSpec((1, gw), lambda s, p: (0, p))
    in_specs = [pl.BlockSpec((T, gw), lambda s, p: (sb + s, cb(C_R) + p)),
                pl.BlockSpec((T, gw), lambda s, p: (sb + s, cb(C_K) + p)),
                pl.BlockSpec((T, gw), lambda s, p: (sb + s, cb(C_V) + p)),
                pl.BlockSpec((T, 4 * R_LORA), lambda s, p: (sb + s, C_LORA // (4 * R_LORA))),
                pl.BlockSpec((T, R_G_PAD), lambda s, p: (sb + s, C_LG // R_G_PAD)),
                pl.BlockSpec((2, gw), lambda s, p: (0, p)),
                pl.BlockSpec((2, R_LORA, gw), lambda s, p: (0, 0, p)),
                pl.BlockSpec((2, gw), lambda s, p: (0, p)),
                pl.BlockSpec((2, R_LORA, gw), lambda s, p: (0, 0, p)),
                pl.BlockSpec((R_G_PAD, gw), lambda s, p: (0, p)),
                vec, vec, vec, vec, vec]
    args = [proj, proj, proj, proj, proj, w0, wl, a0, wa, wg, kkw, ka, rk, lnw, lnb]
    has_s0 = s0 is not None
    if has_s0:
        in_specs.append(pl.BlockSpec((None, 2, n_pairs, PAIR, PAIR), lambda s, p: (s, 0, p, 0, 0)))
        args.append(s0)
    out_shape = [jax.ShapeDtypeStruct((n_seq * T, D_A), BF)]
    out_specs = [pl.BlockSpec((T, gw), lambda s, p: (s, p))]
    out_state = not has_s0
    if out_state:
        out_shape.append(jax.ShapeDtypeStruct((n_seq, 2, H_A, HEAD_A, HEAD_A), F32))
        out_specs.append(pl.BlockSpec((None, 2, 2 * n_pairs, HEAD_A, HEAD_A), lambda s, p: (s, 0, p, 0, 0)))
    nch = 2 * n_pairs
    return pl.pallas_call(
        functools.partial(_wkv_kernel, T=T, n_pairs=n_pairs, has_s0=has_s0, out_state=out_state),
        out_shape=out_shape,
        grid=(n_seq, N_PAIR // n_pairs),
        in_specs=in_specs,
        out_specs=out_specs,
        scratch_shapes=[pltpu.VMEM((nch, T, PAIR), F32), pltpu.VMEM((nch, T, PAIR), F32),
                        pltpu.VMEM((nch, T, PAIR), F32), pltpu.VMEM((n_pairs, T, PAIR), F32),
                        pltpu.VMEM((nch, T, PAIR), F32),
                        pltpu.VMEM((nch, nc, 2 * PAIR, PAIR), BF),
                        pltpu.VMEM((nch, nc, PAIR, PAIR), F32), pltpu.VMEM((nch, nc, PAIR, PAIR), F32),
                        pltpu.VMEM((T, R_G_PAD), BF)],
        compiler_params=_cparams(("parallel", "arbitrary")),
    )(*args)


W_IN_BLK = 512
W_IN_SIDE = 128
W_IN_B0 = C_QLAT // W_IN_BLK
W_IN_C0 = C_GLA // W_IN_BLK
W_IN_SHIFT = R_G_PAD - R_G


def _w_in_relayout_kernel(main_ref, side_ref, o_ref):
    j = pl.program_id(0)
    row = lax.broadcasted_iota(jnp.int32, o_ref.shape, 0)
    sh = W_IN_SHIFT

    @pl.when(j < W_IN_B0)
    def _():
        valid = jnp.where(j == W_IN_B0 - 1, R_G, W_IN_BLK)
        o_ref[...] = jnp.where(row < valid, main_ref[...], 0.0).astype(BF)

    @pl.when((j >= W_IN_B0) & (j < W_IN_C0))
    def _():
        valid = jnp.where(j == W_IN_C0 - 1, ROPE_DIM, W_IN_BLK)
        x = jnp.concatenate([side_ref[W_IN_SIDE - sh:, :], main_ref[:W_IN_BLK - sh, :]], axis=0)
        o_ref[...] = jnp.where(row < valid, x, 0.0).astype(BF)

    @pl.when(j >= W_IN_C0)
    def _():
        x = jnp.concatenate([main_ref[sh:, :], side_ref[:sh, :]], axis=0)
        o_ref[...] = x.astype(BF)


def _w_in_relayout(w_in_t):
    tc = 2048
    per = W_IN_BLK // W_IN_SIDE

    def main_idx(j, c):
        return jnp.where(j >= W_IN_C0, j - 1, j), c

    def side_idx(j, c):
        return jnp.where(j >= W_IN_C0, per * j, per * jnp.maximum(j, W_IN_B0) - 1), c

    return pl.pallas_call(
        _w_in_relayout_kernel,
        out_shape=jax.ShapeDtypeStruct((D_IN_PAD, D_MODEL), BF),
        grid=(D_IN_PAD // W_IN_BLK, D_MODEL // tc),
        in_specs=[pl.BlockSpec((W_IN_BLK, tc), main_idx), pl.BlockSpec((W_IN_SIDE, tc), side_idx)],
        out_specs=pl.BlockSpec((W_IN_BLK, tc), lambda j, c: (j, c)),
        compiler_params=_cparams(("arbitrary", "arbitrary")),
    )(w_in_t, w_in_t)


def _rope_tables():
    rows = DEC_SEQ // GRID_W
    row = jnp.repeat(jnp.arange(rows), GRID_W).astype(F32)
    col = jnp.tile(jnp.arange(GRID_W), rows).astype(F32)
    n_freq = ROPE_DIM // 4
    inv = ROPE_THETA ** (-jnp.arange(n_freq, dtype=F32) / n_freq)
    ang = jnp.concatenate([row[:, None] * inv, col[:, None] * inv], axis=-1)
    cos, sin = jnp.cos(ang), jnp.sin(ang)
    pad = jnp.zeros((DEC_SEQ, LANES - ROPE_DIM), F32)
    cos_t = jnp.concatenate([cos, cos, pad], axis=-1)
    sin_t = jnp.concatenate([-sin, sin, pad], axis=-1)
    cos_t = jnp.concatenate([cos_t, jnp.ones((QKV_TM, LANES), F32)], axis=0)
    sin_t = jnp.concatenate([sin_t, jnp.zeros((QKV_TM, LANES), F32)], axis=0)
    return cos_t, sin_t


def _pair_state_in(state):
    b = state.shape[0]
    s = jnp.swapaxes(state, -1, -2).reshape(b, 2, N_PAIR, 2, HEAD_A, HEAD_A)
    z = jnp.zeros_like(s[:, :, :, 0])
    top = jnp.concatenate([s[:, :, :, 0], z], axis=-1)
    bot = jnp.concatenate([z, s[:, :, :, 1]], axis=-1)
    return jnp.concatenate([top, bot], axis=-2)


def kernel(x_prompt, x_sample, c, state_rwkv, cache_mla_ckv, cache_mla_kpe, c_ctx, w_mod, b_mod, norm_mix_w, w_in, rwkv_w0, rwkv_w_lora_b, rwkv_a0, rwkv_a_lora_b, rwkv_g_lora_b, rwkv_k_k, rwkv_k_a, rwkv_r_k, rwkv_ln_w, rwkv_ln_b, mla_q_norm_w, mla_w_uq, mla_kv_norm_w, mla_w_ukv, mla_q_head_norm, mla_k_head_norm, w_o_rwkv, w_o_mla, w_out, norm_ffn_w, w_ffn_in, w_ffn_out):
    l = 0
    xp = x_prompt.reshape(N_PROMPT, D_MODEL)
    xs = x_sample.reshape(N_SAMPLE, D_MODEL)

    w_in_t = _w_in_relayout(jnp.swapaxes(w_in[l], 0, 1))
    w_uq_p = jnp.pad(mla_w_uq[l].reshape(Q_LORA, H_B, QK_DIM), ((0, 0), (0, 0), (0, QK_PAD - QK_DIM)))
    w_uq_p = w_uq_p.reshape(Q_LORA, H_B * QK_PAD).astype(BF)
    q_head_w_p = jnp.pad(mla_q_head_norm[l], (0, QK_PAD - QK_DIM)).reshape(1, QK_PAD)
    wkv3 = mla_w_ukv[l].reshape(KV_LORA, H_B, NOPE_DIM + V_DIM)
    w_ukv_p = jnp.concatenate([wkv3[:, :, :NOPE_DIM].reshape(KV_LORA, H_B * NOPE_DIM),
                               wkv3[:, :, NOPE_DIM:].reshape(KV_LORA, H_B * V_DIM)], axis=1).astype(BF)
    wkn = mla_k_head_norm[l, :NOPE_DIM].reshape(1, NOPE_DIM)
    wkr = jnp.pad(mla_k_head_norm[l, NOPE_DIM:], (0, LANES - ROPE_DIM)).reshape(1, LANES)
    w_oa = w_o_rwkv[l].astype(BF)
    w_ob = w_o_mla[l].astype(BF)
    w_out_b = w_out[l].astype(BF)
    wg_p = jnp.pad(rwkv_g_lora_b[l], ((0, R_G_PAD - R_G), (0, 0)))
    row = lambda a: a.reshape(1, D_A)
    wkv_w = (rwkv_w0[l], rwkv_w_lora_b[l], rwkv_a0[l], rwkv_a_lora_b[l], wg_p, row(rwkv_k_k[l]), row(rwkv_k_a[l]),
             row(rwkv_r_k[l]), row(rwkv_ln_w[l]), row(rwkv_ln_b[l]))
    cos_t, sin_t = _rope_tables()

    cond8 = jnp.concatenate([c_ctx[None, :], c, jnp.zeros((8 - 1 - DEC_BATCH, D_MODEL), F32)], axis=0)
    mod = _modulation(cond8, w_mod[l], b_mod[l]).reshape(8, 6, D_MODEL)

    proj, w_ffn_in_p = _in_proj(_norm_mod((xp, xs), norm_mix_w[l], mod, 0, 1), w_in_t, w_ffn_in[l])

    ya_p, s_new = _wkv(proj, 0, BATCH, SEQ, wkv_w, None, WKV_PAIRS_PROMPT)
    ya_s = _wkv(proj, N_PROMPT, DEC_BATCH, DEC_SEQ, wkv_w, _pair_state_in(state_rwkv[:, l]), WKV_PAIRS_SAMPLE)[0]

    q = _q_proj(proj, mla_q_norm_w[l], w_uq_p, q_head_w_p, cos_t, sin_t)
    k, v, ckv = _kv_proj(proj, C_KVLAT, proj, C_KPE, N_TOK, mla_kv_norm_w[l], w_ukv_p, wkn, wkr, cos_t, sin_t,
                         _rope_block, True)
    ctx_ckv = cache_mla_ckv[:, l].reshape(DEC_BATCH * PAST_LEN, KV_LORA)
    ctx_kpe = jnp.pad(cache_mla_kpe[:, l].reshape(DEC_BATCH * PAST_LEN, ROPE_DIM), ((0, 0), (0, LANES - ROPE_DIM)))
    k_ctx, v_ctx = _kv_proj(ctx_ckv, 0, ctx_kpe, 0, DEC_BATCH * PAST_LEN, mla_kv_norm_w[l], w_ukv_p, wkn, wkr,
                            cos_t, sin_t, lambda i, tm: DEC_SEQ // tm, False)
    yb_p = _attn_prompt(q, k, v)
    yb_s = _attn_sample(q, k, v, k_ctx, v_ctx)

    mixed = _merge((ya_p, ya_s), (yb_p, yb_s), w_oa, w_ob, proj)
    x1 = _resid_proj(mixed, w_out_b, (xp, xs), mod, 2, 1024, 512, 0, N_TOK)
    act, w_ffn_out_p = _ffn_in(_norm_mod((x1,), norm_ffn_w[l], mod, 3, 4), w_ffn_in_p, w_ffn_out[l])
    y_p = _resid_proj(act, w_ffn_out_p, (x1,), mod, 5, 512, 512, 0, N_PROMPT).reshape(BATCH, SEQ, D_MODEL)
    y_s = _resid_proj(act, w_ffn_out_p, (x1,), mod, 5, 512, 512, N_PROMPT, N_SAMPLE)
    y_s = y_s.reshape(DEC_BATCH, DEC_SEQ, D_MODEL)
    new_state = s_new[:, None]
    new_ckv = ckv[:N_PROMPT].reshape(BATCH, 1, SEQ, KV_LORA)
    new_kpe = proj[:N_PROMPT, C_KPE:C_KPE + ROPE_DIM].reshape(BATCH, 1, SEQ, ROPE_DIM)
    return y_p, y_s, new_state, new_ckv, new_kpe
```
